```python
import math
import jax, jax.numpy as jnp
from jax import lax
import numpy as np

D_MODEL = 2048
BATCH = 4
SEQ = 4096
DEPTH = 1

MIX_WIDTH = D_MODEL
ATTN_WIDTH = MIX_WIDTH // 2
SSM_WIDTH = MIX_WIDTH - ATTN_WIDTH
N_HEADS = 8
DK = ATTN_WIDTH // N_HEADS // 2
DV = 2 * DK
Q_BLOCK = 128
N_BUCKETS = 32
MAX_DISTANCE = 128
SSM_GROUP = 16
SSM_GROUPS = SSM_WIDTH // SSM_GROUP
SSM_STATE = 64
N_EXPERT_GROUPS = 4
EXPERTS_PER_GROUP = 8
N_EXPERTS = N_EXPERT_GROUPS * EXPERTS_PER_GROUP
TOP_K = 2
D_EXPERT = D_MODEL // 4
EXPERT_BLOCK = 128
PLE_DIM = 256
PROJ_WIDTH = 3 * ATTN_WIDTH + SSM_WIDTH

kernel_name = "hymba_diffattn_s5_hiermoe_ple"


def rms_norm(x, g, eps=1e-6):
    x32 = x.astype(jnp.float32)
    y = x32 * lax.rsqrt(jnp.mean(x32 * x32, axis=-1, keepdims=True) + eps)
    return (y * g.astype(jnp.float32)).astype(x.dtype)


def t5_causal_bucket(n):
    n = jnp.maximum(n, 0)
    max_exact = N_BUCKETS // 2
    is_small = n < max_exact
    nf = jnp.maximum(n, 1).astype(jnp.float32)
    large = max_exact + (jnp.log(nf / max_exact) / math.log(MAX_DISTANCE / max_exact)
                         * (N_BUCKETS - max_exact)).astype(jnp.int32)
    large = jnp.minimum(large, N_BUCKETS - 1)
    return jnp.where(is_small, n, large)


def diff_attention(zq, zk, zv, rel_bias, lq1, lk1, lq2, lk2, subln_g, lam_init):
    bt, s_len = zq.shape[0], zq.shape[1]
    q = zq.reshape(bt, s_len, N_HEADS, 2, DK).transpose(0, 2, 3, 1, 4)
    k = zk.reshape(bt, s_len, N_HEADS, 2, DK).transpose(0, 2, 3, 1, 4)
    v = zv.reshape(bt, s_len, N_HEADS, DV).transpose(0, 2, 1, 3)
    lam = (jnp.exp(jnp.sum(lq1.astype(jnp.float32) * lk1.astype(jnp.float32)))
           - jnp.exp(jnp.sum(lq2.astype(jnp.float32) * lk2.astype(jnp.float32)))
           + lam_init)
    scale = DK ** -0.5
    kpos = jnp.arange(s_len, dtype=jnp.int32)
    table = rel_bias.astype(jnp.float32)

    def block(bi):
        start = bi * Q_BLOCK
        qb = lax.dynamic_slice_in_dim(q, start, Q_BLOCK, axis=3)
        sc = jnp.einsum('bhcqd,bhckd->bhcqk', qb, k).astype(jnp.float32) * scale
        qpos = start + jnp.arange(Q_BLOCK, dtype=jnp.int32)
        rel = qpos[:, None] - kpos[None, :]
        bias = table[t5_causal_bucket(rel)].transpose(2, 0, 1)
        sc = sc + bias[None, :, None]
        sc = jnp.where((rel >= 0)[None, None, None], sc, -1e30)
        pr = jax.nn.softmax(sc, axis=-1)
        a = pr[:, :, 0] - lam * pr[:, :, 1]
        return jnp.einsum('bhqk,bhkd->bhqd', a.astype(v.dtype), v)

    o = lax.map(block, jnp.arange(s_len // Q_BLOCK))
    o = o.transpose(1, 0, 3, 2, 4).reshape(bt, s_len, N_HEADS, DV)
    o = rms_norm(o, subln_g, eps=1e-5) * (1.0 - lam_init)
    return o.reshape(bt, s_len, N_HEADS * DV)


def _ssm_combine(e1, e2):
    ar1, ai1, br1, bi1 = e1
    ar2, ai2, br2, bi2 = e2
    ar = ar2 * ar1 - ai2 * ai1
    ai = ar2 * ai1 + ai2 * ar1
    br = ar2 * br1 - ai2 * bi1 + br2
    bi = ar2 * bi1 + ai2 * br1 + bi2
    return (ar, ai, br, bi)


def s5_ssm(u, lam_re, lam_im, log_dt, b_re, b_im, c_re, c_im, d_skip):
    bt, s_len = u.shape[0], u.shape[1]
    uf = u.reshape(bt, s_len, SSM_GROUPS, SSM_GROUP).astype(jnp.float32)
    lre = lam_re.astype(jnp.float32)
    lim = lam_im.astype(jnp.float32)
    dt = jnp.exp(log_dt.astype(jnp.float32))[:, None]
    mag = jnp.exp(lre * dt)
    ab_re = mag * jnp.cos(lim * dt)
    ab_im = mag * jnp.sin(lim * dt)
    den = lre * lre + lim * lim
    nr, ni = ab_re - 1.0, ab_im
    cr = ((nr * lre + ni * lim) / den)[..., None]
    ci = ((ni * lre - nr * lim) / den)[..., None]
    bre = b_re.astype(jnp.float32)
    bim = b_im.astype(jnp.float32)
    bb_re = cr * bre - ci * bim
    bb_im = cr * bim + ci * bre
    cre = c_re.astype(jnp.float32)
    cim = c_im.astype(jnp.float32)
    dsk = d_skip.astype(jnp.float32)

    def one(ub):
        bur = jnp.einsum('sgh,gph->sgp', ub, bb_re)
        bui = jnp.einsum('sgh,gph->sgp', ub, bb_im)
        ar = jnp.broadcast_to(ab_re, bur.shape)
        ai = jnp.broadcast_to(ab_im, bur.shape)
        _, _, xr, xi = lax.associative_scan(_ssm_combine, (ar, ai, bur, bui), axis=0)
        return (jnp.einsum('sgp,ghp->sgh', xr, cre)
                - jnp.einsum('sgp,ghp->sgh', xi, cim) + dsk * ub)

    y = lax.map(one, uf)
    return y.reshape(bt, s_len, SSM_GROUPS * SSM_GROUP).astype(u.dtype)


def hier_moe(h, w_rg, b_rg, w_re, b_re, w1, w3, w2):
    n_tok, d = h.shape
    tok = jnp.arange(n_tok, dtype=jnp.int32)
    lg = (h @ w_rg).astype(jnp.float32) + b_rg.astype(jnp.float32)
    pg = jax.nn.softmax(lg, axis=-1)
    gsel = jnp.argmax(lg, axis=-1).astype(jnp.int32)
    gate_g = pg[tok, gsel][:, None]
    le = ((h @ w_re).astype(jnp.float32) + b_re.astype(jnp.float32)).reshape(
        n_tok, N_EXPERT_GROUPS, EXPERTS_PER_GROUP)
    pe = jax.nn.softmax(le[tok, gsel], axis=-1)
    top_p, top_i = lax.top_k(pe, TOP_K)
    w = gate_g * top_p / jnp.sum(top_p, axis=-1, keepdims=True)
    eid = gsel[:, None] * EXPERTS_PER_GROUP + top_i.astype(jnp.int32)

    n_assign = n_tok * TOP_K
    flat_e = eid.reshape(-1)
    flat_w = w.reshape(-1)
    flat_t = jnp.repeat(tok, TOP_K)
    counts = jnp.bincount(flat_e, length=N_EXPERTS)
    padded = ((counts + EXPERT_BLOCK - 1) // EXPERT_BLOCK) * EXPERT_BLOCK
    pad_end = jnp.cumsum(padded)
    pad_start = pad_end - padded
    start = jnp.cumsum(counts) - counts
    order = jnp.argsort(flat_e)
    se = flat_e[order]
    dest = pad_start[se] + (jnp.arange(n_assign, dtype=jnp.int32) - start[se])
    m_rows = ((n_assign + EXPERT_BLOCK - 1) // EXPERT_BLOCK) * EXPERT_BLOCK + N_EXPERTS * EXPERT_BLOCK
    row_tok = jnp.full((m_rows,), n_tok, jnp.int32).at[dest].set(flat_t[order])
    row_w = jnp.zeros((m_rows,), jnp.float32).at[dest].set(flat_w[order])
    n_blk = m_rows // EXPERT_BLOCK
    blk_start = jnp.arange(n_blk, dtype=jnp.int32) * EXPERT_BLOCK
    blk_e = jnp.minimum(jnp.searchsorted(pad_end, blk_start, side='right'), N_EXPERTS - 1)
    h_pad = jnp.concatenate([h, jnp.zeros((1, d), h.dtype)], axis=0)
    xb = h_pad[row_tok].reshape(n_blk, EXPERT_BLOCK, d)

    def expert_block(args):
        xe, e = args
        return (jax.nn.silu(xe @ w1[e]) * (xe @ w3[e])) @ w2[e]

    yb = lax.map(expert_block, (xb, blk_e)).reshape(m_rows, d)
    y = jax.ops.segment_sum(yb * row_w[:, None].astype(yb.dtype), row_tok,
                            num_segments=n_tok + 1)
    return y[:n_tok]


def setup_inputs(seed: int = 0) -> dict:
    key = jax.random.key(seed)
    ks = jax.random.split(key, 40)
    f32 = jnp.float32
    nrm = lambda k, shape, s: jax.random.normal(k, shape, f32) * s
    gain = lambda k, shape: 1.0 + 0.01 * jax.random.normal(k, shape, f32)
    L, D, G, P, H = DEPTH, D_MODEL, SSM_GROUPS, SSM_STATE, SSM_GROUP
    n_idx = jnp.arange(P, dtype=f32)[None, None, :]
    return {
        "x": jax.random.normal(ks[0], (BATCH, SEQ, D), f32),
        "p": jax.random.normal(ks[1], (L, BATCH, SEQ, PLE_DIM), f32),
        "rel_bias": nrm(ks[2], (N_BUCKETS, N_HEADS), 0.1),
        "g_mix": gain(ks[3], (L, D)),
        "w_in": nrm(ks[4], (L, D, PROJ_WIDTH), D ** -0.5),
        "lam_q1": nrm(ks[5], (L, DK), 0.1),
        "lam_k1": nrm(ks[6], (L, DK), 0.1),
        "lam_q2": nrm(ks[7], (L, DK), 0.1),
        "lam_k2": nrm(ks[8], (L, DK), 0.1),
        "subln_g": gain(ks[9], (L, DV)),
        "ssm_lam_re": -0.5 + 0.01 * jax.random.normal(ks[10], (L, G, P), f32),
        "ssm_lam_im": math.pi * n_idx + 0.01 * jax.random.normal(ks[11], (L, G, P), f32),
        "ssm_log_dt": jax.random.uniform(ks[12], (L, G), f32, math.log(1e-3), math.log(1e-1)),
        "ssm_b_re": nrm(ks[13], (L, G, P, H), (2.0 * H) ** -0.5),
        "ssm_b_im": nrm(ks[14], (L, G, P, H), (2.0 * H) ** -0.5),
        "ssm_c_re": nrm(ks[15], (L, G, H, P), P ** -0.5),
        "ssm_c_im": nrm(ks[16], (L, G, H, P), P ** -0.5),
        "ssm_d": nrm(ks[17], (L, G, H), 1.0),
        "w_glu": nrm(ks[18], (L, SSM_WIDTH, SSM_WIDTH), SSM_WIDTH ** -0.5),
        "b_glu": nrm(ks[19], (L, SSM_WIDTH), 0.01),
        "ssm_norm_g": gain(ks[20], (L, SSM_WIDTH)),
        "w_o": nrm(ks[21], (L, MIX_WIDTH, D), MIX_WIDTH ** -0.5),
        "g_ffn": gain(ks[22], (L, D)),
        "w_router_g": nrm(ks[23], (L, D, N_EXPERT_GROUPS), D ** -0.5),
        "b_router_g": nrm(ks[24], (L, N_EXPERT_GROUPS), 0.01),
        "w_router_e": nrm(ks[25], (L, D, N_EXPERTS), D ** -0.5),
        "b_router_e": nrm(ks[26], (L, N_EXPERTS), 0.01),
        "w1": nrm(ks[27], (L, N_EXPERTS, D, D_EXPERT), D ** -0.5),
        "w3": nrm(ks[28], (L, N_EXPERTS, D, D_EXPERT), D ** -0.5),
        "w2": nrm(ks[29], (L, N_EXPERTS, D_EXPERT, D), D_EXPERT ** -0.5),
        "g_ple": gain(ks[30], (L, D)),
        "w_ple_gate": nrm(ks[31], (L, D, D), D ** -0.5),
        "w_ple_proj": nrm(ks[32], (L, PLE_DIM, D), PLE_DIM ** -0.5),
        "g_final": gain(ks[33], (D,)),
    }


def reference(x, p, rel_bias, g_mix, w_in, lam_q1, lam_k1, lam_q2, lam_k2, subln_g,
              ssm_lam_re, ssm_lam_im, ssm_log_dt, ssm_b_re, ssm_b_im, ssm_c_re, ssm_c_im,
              ssm_d, w_glu, b_glu, ssm_norm_g, w_o, g_ffn, w_router_g, b_router_g,
              w_router_e, b_router_e, w1, w3, w2, g_ple, w_ple_gate, w_ple_proj, g_final):
    bt, s_len, d = x.shape
    for i in range(DEPTH):
        lam_init = 0.8 - 0.6 * math.exp(-0.3 * i)
        h = rms_norm(x, g_mix[i])
        z = h @ w_in[i]
        zq = z[..., :ATTN_WIDTH]
        zk = z[..., ATTN_WIDTH:2 * ATTN_WIDTH]
        zv = z[..., 2 * ATTN_WIDTH:3 * ATTN_WIDTH]
        zu = z[..., 3 * ATTN_WIDTH:]
        a = diff_attention(zq, zk, zv, rel_bias, lam_q1[i], lam_k1[i], lam_q2[i],
                           lam_k2[i], subln_g[i], lam_init)
        s = s5_ssm(zu, ssm_lam_re[i], ssm_lam_im[i], ssm_log_dt[i], ssm_b_re[i],
                   ssm_b_im[i], ssm_c_re[i], ssm_c_im[i], ssm_d[i])
        s = jax.nn.gelu(s)
        s = s * jax.nn.sigmoid(s @ w_glu[i] + b_glu[i])
        s = rms_norm(s, ssm_norm_g[i])
        x = x + jnp.concatenate([a, s], axis=-1) @ w_o[i]
        h = rms_norm(x, g_ffn[i]).reshape(bt * s_len, d)
        x = x + hier_moe(h, w_router_g[i], b_router_g[i], w_router_e[i], b_router_e[i],
                         w1[i], w3[i], w2[i]).reshape(bt, s_len, d)
        gate = jax.nn.sigmoid(rms_norm(x, g_ple[i]) @ w_ple_gate[i])
        x = x + gate * (p[i] @ w_ple_proj[i])
    return rms_norm(x, g_final)
```

```python
import functools
import math

import jax
import jax.numpy as jnp
from jax import lax
from jax.experimental import pallas as pl
from jax.experimental.pallas import tpu as pltpu

F32 = jnp.float32
BF16 = jnp.bfloat16
HIGHEST = lax.Precision.HIGHEST

N_HEADS = 8
DK = 64
DV = 128
N_BUCKETS = 32
MAX_DISTANCE = 128
SSM_GROUP = 16
SSM_GROUPS = 64
SSM_STATE = 64
N_EXPERT_GROUPS = 4
EXPERTS_PER_GROUP = 8
N_EXPERTS = 32
TOP_K = 2
MASK_VALUE = -1e30

SSM_CHUNK = 16
ATTN_TILE = 256
EXPERT_ROWS = 256
ROUTER_LANES = 128
VMEM_LIMIT = 56 << 20


def _params(semantics, **kw):
    return pltpu.CompilerParams(dimension_semantics=semantics, vmem_limit_bytes=VMEM_LIMIT, **kw)


def _rms(x, g, eps):
    return x * lax.rsqrt(jnp.mean(x * x, axis=-1, keepdims=True) + eps) * g


def _sigmoid(x):
    return 1.0 / (1.0 + jnp.exp(-x))


def _inproj_kernel(x_ref, g_ref, w_ref, o_ref, h_scr):
    @pl.when(pl.program_id(1) == 0)
    def _():
        h_scr[...] = _rms(x_ref[...], g_ref[...], 1e-6).astype(BF16)

    o_ref[...] = jnp.dot(h_scr[...], w_ref[...], preferred_element_type=F32).astype(o_ref.dtype)


def _inproj(x2d, g, w, tm=512, tn=1024):
    n, d = x2d.shape
    pw = w.shape[1]
    return pl.pallas_call(
        _inproj_kernel,
        grid=(n // tm, pw // tn),
        in_specs=[
            pl.BlockSpec((tm, d), lambda i, j: (i, 0)),
            pl.BlockSpec((1, d), lambda i, j: (0, 0)),
            pl.BlockSpec((d, tn), lambda i, j: (0, j)),
        ],
        out_specs=pl.BlockSpec((tm, tn), lambda i, j: (i, j)),
        out_shape=jax.ShapeDtypeStruct((n, pw), BF16),
        scratch_shapes=[pltpu.VMEM((tm, d), BF16)],
        compiler_params=_params(("parallel", "arbitrary")),
        name="inproj",
    )(x2d, g.reshape(1, d), w)


def _attn_kernel(lam_ref, q_ref, k_ref, v_ref, bd_ref, bs_ref, g_ref, o_ref,
                 qs, m_scr, l_scr, acc, *, tile, out_scale):
    t = tile
    qi = pl.program_id(2)
    q = q_ref[0]
    lane = lax.broadcasted_iota(jnp.int32, q.shape, 1)
    zero = jnp.zeros_like(q)
    qs[0:t, :] = jnp.where(lane < DK, q, zero)
    qs[t:2 * t, :] = jnp.where(lane >= DK, q, zero)
    m_scr[...] = jnp.full(m_scr.shape, MASK_VALUE, F32)
    l_scr[...] = jnp.zeros(l_scr.shape, F32)
    acc[...] = jnp.zeros(acc.shape, F32)

    def step(j, bias_ref):
        off = pl.multiple_of(j * t, t)
        kj = k_ref[0, pl.ds(off, t), :]
        vj = v_ref[0, pl.ds(off, t), :]
        s = lax.dot_general(qs[...], kj, (((1,), (1,)), ((), ())), preferred_element_type=F32)
        if bias_ref is not None:
            s = (s.reshape(2, t, t) + bias_ref[0][None]).reshape(2 * t, t)
        m_old = m_scr[...]
        m_new = jnp.maximum(m_old, jnp.max(s, axis=1, keepdims=True))
        alpha = jnp.exp(m_old - m_new)
        p = jnp.exp(s - m_new)
        l_scr[...] = alpha * l_scr[...] + jnp.sum(p, axis=1, keepdims=True)
        acc[...] = alpha * acc[...] + jnp.dot(p.astype(BF16), vj, preferred_element_type=F32)
        m_scr[...] = m_new

    def far(j, c):
        step(j, None)
        return c

    lax.fori_loop(0, jnp.maximum(qi - 1, 0), far, 0)

    @pl.when(qi >= 1)
    def _():
        step(qi - 1, bs_ref)

    step(qi, bd_ref)

    lam = lam_ref[0, 0]
    o = acc[0:t, :] / l_scr[0:t, :] - lam * (acc[t:2 * t, :] / l_scr[t:2 * t, :])
    o_ref[0] = (_rms(o, g_ref[...], 1e-5) * out_scale).astype(o_ref.dtype)


def _t5_bucket(n):
    n = jnp.maximum(n, 0)
    max_exact = N_BUCKETS // 2
    nf = jnp.maximum(n, 1).astype(F32)
    large = max_exact + (jnp.log(nf / max_exact) / math.log(MAX_DISTANCE / max_exact)
                         * (N_BUCKETS - max_exact)).astype(jnp.int32)
    large = jnp.minimum(large, N_BUCKETS - 1)
    return jnp.where(n < max_exact, n, large)


def _attn_bias_tiles(rel_bias, t):
    assert t >= MAX_DISTANCE
    table = rel_bias.astype(F32)
    rel_table = table - table[N_BUCKETS - 1][None, :]
    i = jnp.arange(t, dtype=jnp.int32)
    d_diag = i[:, None] - i[None, :]
    d_sub = d_diag + t
    b_diag = rel_table[_t5_bucket(d_diag)].transpose(2, 0, 1)
    b_diag = jnp.where((d_diag >= 0)[None], b_diag, MASK_VALUE)
    b_sub = rel_table[_t5_bucket(d_sub)].transpose(2, 0, 1)
    return b_diag, b_sub


def _attention(z3, rel_bias, lam, subln_g, lam_init, tile=ATTN_TILE):
    bt, s_len, _ = z3.shape
    t = tile
    b_diag, b_sub = _attn_bias_tiles(rel_bias, t)
    kern = functools.partial(_attn_kernel, tile=t, out_scale=1.0 - lam_init)
    return pl.pallas_call(
        kern,
        grid=(bt, N_HEADS, s_len // t),
        in_specs=[
            pl.BlockSpec(memory_space=pltpu.SMEM),
            pl.BlockSpec((1, t, 2 * DK), lambda b, h, i: (b, i, h)),
            pl.BlockSpec((1, s_len, 2 * DK), lambda b, h, i: (b, 0, N_HEADS + h)),
            pl.BlockSpec((1, s_len, DV), lambda b, h, i: (b, 0, 2 * N_HEADS + h)),
            pl.BlockSpec((1, t, t), lambda b, h, i: (h, 0, 0)),
            pl.BlockSpec((1, t, t), lambda b, h, i: (h, 0, 0)),
            pl.BlockSpec((1, DV), lambda b, h, i: (0, 0)),
        ],
        out_specs=pl.BlockSpec((1, t, DV), lambda b, h, i: (b, i, h)),
        out_shape=jax.ShapeDtypeStruct((bt, s_len, N_HEADS * DV), BF16),
        scratch_shapes=[
            pltpu.VMEM((2 * t, 2 * DK), BF16),
            pltpu.VMEM((2 * t, 1), F32),
            pltpu.VMEM((2 * t, 1), F32),
            pltpu.VMEM((2 * t, DV), F32),
        ],
        compiler_params=_params(("parallel", "parallel", "arbitrary")),
        name="diff_attention",
    )(lam.reshape(1, 1).astype(F32), z3, z3, z3, b_diag, b_sub, subln_g.reshape(1, DV).astype(F32))


def _ssm_operators(lam_re, lam_im, log_dt, b_re, b_im, c_re, c_im, d_skip):
    L, H, P = SSM_CHUNK, SSM_GROUP, SSM_STATE
    lre = lam_re.astype(F32)
    lim = lam_im.astype(F32)
    dt = jnp.exp(log_dt.astype(F32))[:, None]
    mag = jnp.exp(lre * dt)
    ab_re = mag * jnp.cos(lim * dt)
    ab_im = mag * jnp.sin(lim * dt)
    den = lre * lre + lim * lim
    nr, ni = ab_re - 1.0, ab_im
    cr = ((nr * lre + ni * lim) / den)[..., None]
    ci = ((ni * lre - nr * lim) / den)[..., None]
    bre = b_re.astype(F32)
    bim = b_im.astype(F32)
    bb_re = cr * bre - ci * bim
    bb_im = cr * bim + ci * bre
    cre = c_re.astype(F32)
    cim = c_im.astype(F32)

    def mul(carry, _):
        pr, pi = carry
        return (pr * ab_re - pi * ab_im, pr * ab_im + pi * ab_re), (pr, pi)

    (al_re, al_im), (pw_re, pw_im) = lax.scan(
        mul, (jnp.ones_like(ab_re), jnp.zeros_like(ab_re)), None, length=L + 1)
    del al_re, al_im

    ca_re = cre[None] * pw_re[:, :, None, :] - cim[None] * pw_im[:, :, None, :]
    ca_im = cre[None] * pw_im[:, :, None, :] + cim[None] * pw_re[:, :, None, :]
    k_tau = (jnp.einsum('tghp,gpk->tghk', ca_re[:L], bb_re, precision=HIGHEST)
             - jnp.einsum('tghp,gpk->tghk', ca_im[:L], bb_im, precision=HIGHEST))
    ti = jnp.arange(L)
    lag = ti[None, :] - ti[:, None]
    k_ij = k_tau[jnp.clip(lag, 0, L - 1)]
    k_ij = jnp.where((lag >= 0)[:, :, None, None, None], k_ij, 0.0)
    t_op = k_ij.transpose(2, 0, 4, 1, 3).reshape(SSM_GROUPS, L * H, L * H)

    rev_re = pw_re[L - 1 - ti]
    rev_im = pw_im[L - 1 - ti]
    w_re = rev_re[..., None] * bb_re[None] - rev_im[..., None] * bb_im[None]
    w_im = rev_re[..., None] * bb_im[None] + rev_im[..., None] * bb_re[None]
    w_op = jnp.concatenate([w_re, w_im], axis=2)
    w_op = w_op.transpose(1, 0, 3, 2).reshape(SSM_GROUPS, L * H, 2 * P)

    e_re = ca_re[1:L + 1]
    e_im = -ca_im[1:L + 1]
    e_op = jnp.concatenate([e_re, e_im], axis=3)
    e_op = e_op.transpose(1, 3, 0, 2).reshape(SSM_GROUPS, 2 * P, L * H)

    a1 = jnp.concatenate([pw_re[L], pw_re[L]], axis=-1)
    a2 = jnp.concatenate([-pw_im[L], pw_im[L]], axis=-1)
    d_flat = jnp.tile(d_skip.astype(F32), (1, L)).reshape(SSM_GROUPS, 1, L * H)
    return t_op.astype(BF16), w_op.astype(BF16), e_op.astype(BF16), a1, a2, d_flat


def _ssm_in_kernel(u_ref, w_ref, v_ref):
    v_ref[...] = jnp.dot(u_ref[...], w_ref[0], preferred_element_type=F32)


def _ssm_scan_kernel(v_ref, a1_ref, a2_ref, o_ref, st):
    @pl.when(pl.program_id(0) == 0)
    def _():
        st[...] = jnp.zeros(st.shape, F32)

    a1 = a1_ref[...][None]
    a2 = a2_ref[...][None]
    n_chunk = v_ref.shape[1]
    half = v_ref.shape[3] // 2

    def body(c, s):
        o_ref[:, pl.ds(c, 1)] = s[:, None].astype(o_ref.dtype)
        v = v_ref[:, pl.ds(c, 1)][:, 0]
        return a1 * s + a2 * pltpu.roll(s, half, axis=2) + v

    st[...] = lax.fori_loop(0, n_chunk, body, st[...])


def _gelu_tanh(x):
    c = math.sqrt(2.0 / math.pi)
    return x * (0.5 * (1.0 + jnp.tanh(c * (x + 0.044715 * (x * x * x)))))


def _ssm_out_kernel(u_ref, t_ref, s_ref, e_ref, d_ref, y_ref):
    u = u_ref[...]
    y = jnp.dot(u, t_ref[0], preferred_element_type=F32)
    y = y + jnp.dot(s_ref[...], e_ref[0], preferred_element_type=F32)
    y = y + d_ref[0] * u.astype(F32)
    y_ref[...] = _gelu_tanh(y).astype(y_ref.dtype)


def _ssm(u, ops, scan_block=32):
    t_op, w_op, e_op, a1, a2, d_flat = ops
    bt, s_len, width = u.shape
    L, H, G, P2 = SSM_CHUNK, SSM_GROUP, SSM_GROUPS, 2 * SSM_STATE
    n_c = s_len // L
    nc = bt * n_c
    lw = L * H
    u2 = u.reshape(bt, n_c, L, G, H).transpose(0, 1, 3, 2, 4).reshape(nc, G * lw)

    v = pl.pallas_call(
        _ssm_in_kernel,
        grid=(G,),
        in_specs=[pl.BlockSpec((nc, lw), lambda g: (0, g)),
                  pl.BlockSpec((1, lw, P2), lambda g: (g, 0, 0))],
        out_specs=pl.BlockSpec((nc, P2), lambda g: (0, g)),
        out_shape=jax.ShapeDtypeStruct((nc, G * P2), F32),
        compiler_params=_params(("parallel",)),
        name="ssm_chunk_state",
    )(u2, w_op)

    cb = min(scan_block, n_c)
    s_prev = pl.pallas_call(
        _ssm_scan_kernel,
        grid=(n_c // cb,),
        in_specs=[pl.BlockSpec((bt, cb, G, P2), lambda c: (0, c, 0, 0)),
                  pl.BlockSpec((G, P2), lambda c: (0, 0)),
                  pl.BlockSpec((G, P2), lambda c: (0, 0))],
        out_specs=pl.BlockSpec((bt, cb, G, P2), lambda c: (0, c, 0, 0)),
        out_shape=jax.ShapeDtypeStruct((bt, n_c, G, P2), BF16),
        scratch_shapes=[pltpu.VMEM((bt, G, P2), F32)],
        compiler_params=_params(("arbitrary",)),
        name="ssm_scan",
    )(v.reshape(bt, n_c, G, P2), a1, a2)

    y2 = pl.pallas_call(
        _ssm_out_kernel,
        grid=(G,),
        in_specs=[pl.BlockSpec((nc, lw), lambda g: (0, g)),
                  pl.BlockSpec((1, lw, lw), lambda g: (g, 0, 0)),
                  pl.BlockSpec((nc, P2), lambda g: (0, g)),
                  pl.BlockSpec((1, P2, lw), lambda g: (g, 0, 0)),
                  pl.BlockSpec((1, 1, lw), lambda g: (g, 0, 0))],
        out_specs=pl.BlockSpec((nc, lw), lambda g: (0, g)),
        out_shape=jax.ShapeDtypeStruct((nc, G * lw), BF16),
        compiler_params=_params(("parallel",)),
        name="ssm_output",
    )(u2, t_op, s_prev.reshape(nc, G * P2), e_op, d_flat)

    return y2.reshape(bt, n_c, G, L, H).transpose(0, 1, 3, 2, 4).reshape(bt, s_len, width)


def _postmix_kernel(a_ref, s_ref, x_ref, wg_ref, bg_ref, gs_ref, woa_ref, wos_ref, gf_ref,
                    wrh_ref, wrl_ref, br_ref, x1_ref, h_ref, lg_ref):
    s = s_ref[...]
    sf = s.astype(F32)
    gate = _sigmoid(jnp.dot(s, wg_ref[...], preferred_element_type=F32) + bg_ref[...])
    sn = _rms(sf * gate, gs_ref[...], 1e-6).astype(BF16)
    x1 = (x_ref[...]
          + jnp.dot(a_ref[...], woa_ref[...], preferred_element_type=F32)
          + jnp.dot(sn, wos_ref[...], preferred_element_type=F32))
    x1_ref[...] = x1
    h = _rms(x1, gf_ref[...], 1e-6)
    h_hi = h.astype(BF16)
    h_lo = (h - h_hi.astype(F32)).astype(BF16)
    h_ref[...] = h_hi.astype(h_ref.dtype)
    lg_ref[...] = (jnp.dot(h_hi, wrh_ref[...], preferred_element_type=F32)
                   + jnp.dot(h_lo, wrh_ref[...], preferred_element_type=F32)
                   + jnp.dot(h_hi, wrl_ref[...], preferred_element_type=F32)
                   + br_ref[...])


def _postmix(a, s, x2d, w_glu, b_glu, g_s, w_o, g_ffn, w_r, b_r, tm=256):
    n, d = x2d.shape
    wa = a.shape[1]
    ws = s.shape[1]
    wr_hi = w_r.astype(BF16)
    wr_lo = (w_r - wr_hi.astype(F32)).astype(BF16)
    row = lambda i: (i, 0)
    fixed = lambda i: (0, 0)
    return pl.pallas_call(
        _postmix_kernel,
        grid=(n // tm,),
        in_specs=[
            pl.BlockSpec((tm, wa), row),
            pl.BlockSpec((tm, ws), row),
            pl.BlockSpec((tm, d), row),
            pl.BlockSpec((ws, ws), fixed),
            pl.BlockSpec((1, ws), fixed),
            pl.BlockSpec((1, ws), fixed),
            pl.BlockSpec((wa, d), fixed),
            pl.BlockSpec((ws, d), fixed),
            pl.BlockSpec((1, d), fixed),
            pl.BlockSpec((d, ROUTER_LANES), fixed),
            pl.BlockSpec((d, ROUTER_LANES), fixed),
            pl.BlockSpec((1, ROUTER_LANES), fixed),
        ],
        out_specs=[pl.BlockSpec((tm, d), row), pl.BlockSpec((tm, d), row),
                   pl.BlockSpec((tm, ROUTER_LANES), row)],
        out_shape=[jax.ShapeDtypeStruct((n, d), F32), jax.ShapeDtypeStruct((n, d), F32),
                   jax.ShapeDtypeStruct((n, ROUTER_LANES), F32)],
        compiler_params=_params(("parallel",)),
        name="postmix",
    )(a, s, x2d, w_glu.astype(BF16), b_glu.reshape(1, ws).astype(F32), g_s.reshape(1, ws).astype(F32),
      w_o[:wa].astype(BF16), w_o[wa:].astype(BF16), g_ffn.reshape(1, d).astype(F32),
      wr_hi, wr_lo, b_r.reshape(1, ROUTER_LANES).astype(F32))


def _route(logits, rows):
    n_tok = logits.shape[0]
    tok = jnp.arange(n_tok, dtype=jnp.int32)
    lg = logits[:, :N_EXPERT_GROUPS]
    le = logits[:, N_EXPERT_GROUPS:N_EXPERT_GROUPS + N_EXPERTS].reshape(
        n_tok, N_EXPERT_GROUPS, EXPERTS_PER_GROUP)
    pg = jax.nn.softmax(lg, axis=-1)
    gsel = jnp.argmax(lg, axis=-1).astype(jnp.int32)
    gate_g = pg[tok, gsel][:, None]
    pe = jax.nn.softmax(le[tok, gsel], axis=-1)
    top_p, top_i = lax.top_k(pe, TOP_K)
    w = gate_g * top_p / jnp.sum(top_p, axis=-1, keepdims=True)
    eid = gsel[:, None] * EXPERTS_PER_GROUP + top_i.astype(jnp.int32)

    n_assign = n_tok * TOP_K
    flat_e = eid.reshape(-1)
    counts = jnp.sum((flat_e[:, None] == jnp.arange(N_EXPERTS, dtype=jnp.int32)[None, :])
                     .astype(jnp.int32), axis=0)
    n_blk_e = (counts + rows - 1) // rows
    blk_end = jnp.cumsum(n_blk_e)
    blk_first = blk_end - n_blk_e
    start = jnp.cumsum(counts) - counts
    order = jnp.argsort(flat_e, stable=True).astype(jnp.int32)
    n_blk = (n_assign + rows - 1) // rows + N_EXPERTS
    blk = jnp.arange(n_blk, dtype=jnp.int32)
    blk_e = jnp.minimum(jnp.searchsorted(blk_end, blk, side='right'), N_EXPERTS - 1).astype(jnp.int32)
    used = blk < blk_end[-1]
    in_e = (blk - blk_first[blk_e]) * rows
    blk_cnt = jnp.where(used, jnp.clip(counts[blk_e] - in_e, 0, rows), 0).astype(jnp.int32)
    blk_src = (start[blk_e] + in_e).astype(jnp.int32)
    last_e = blk_e[jnp.maximum(blk_end[-1] - 1, 0)]
    blk_e = jnp.where(used, blk_e, last_e)
    return w, order, blk_e, blk_cnt, blk_src


def _expert_kernel(blk_e_ref, blk_cnt_ref, blk_src_ref, order_ref,
                   h_hbm, w1_ref, w3_ref, w2_ref, y_hbm, xbuf, ybuf, gsem, ssem):
    del blk_e_ref
    b = pl.program_id(0)
    cnt = blk_cnt_ref[b]
    src = blk_src_ref[b]

    @pl.when(b == 0)
    def _():
        xbuf[...] = jnp.zeros(xbuf.shape, xbuf.dtype)

    def gather_copy(r):
        a = order_ref[src + r]
        return pltpu.make_async_copy(h_hbm.at[pl.ds(a // TOP_K, 1)], xbuf.at[pl.ds(r, 1)], gsem)

    def scatter_copy(r):
        a = order_ref[src + r]
        return pltpu.make_async_copy(ybuf.at[pl.ds(r, 1)], y_hbm.at[pl.ds(a, 1)], ssem)

    def start(copy):
        def body(r, c):
            copy(r).start()
            return c
        return body

    def wait(copy):
        def body(r, c):
            copy(r).wait()
            return c
        return body

    @pl.when(cnt > 0)
    def _():
        lax.fori_loop(0, cnt, start(gather_copy), 0)
        lax.fori_loop(0, cnt, wait(gather_copy), 0)
        x = xbuf[...].astype(BF16)
        h1 = jnp.dot(x, w1_ref[0], preferred_element_type=F32)
        h3 = jnp.dot(x, w3_ref[0], preferred_element_type=F32)
        act = (h1 * _sigmoid(h1) * h3).astype(BF16)
        ybuf[...] = jnp.dot(act, w2_ref[0], preferred_element_type=F32)
        lax.fori_loop(0, cnt, start(scatter_copy), 0)
        lax.fori_loop(0, cnt, wait(scatter_copy), 0)


def _experts(h, w1, w3, w2, order, blk_e, blk_cnt, blk_src, rows):
    n, d = h.shape
    f = w1.shape[2]
    n_blk = blk_e.shape[0]
    wmap = lambda b, be, bc, bs, od: (be[b], 0, 0)
    grid_spec = pltpu.PrefetchScalarGridSpec(
        num_scalar_prefetch=4,
        grid=(n_blk,),
        in_specs=[
            pl.BlockSpec(memory_space=pl.ANY),
            pl.BlockSpec((1, d, f), wmap),
            pl.BlockSpec((1, d, f), wmap),
            pl.BlockSpec((1, f, d), wmap),
        ],
        out_specs=pl.BlockSpec(memory_space=pl.ANY),
        scratch_shapes=[
            pltpu.VMEM((rows, d), F32),
            pltpu.VMEM((rows, d), F32),
            pltpu.SemaphoreType.DMA,
            pltpu.SemaphoreType.DMA,
        ],
    )
    return pl.pallas_call(
        _expert_kernel,
        grid_spec=grid_spec,
        out_shape=jax.ShapeDtypeStruct((n * TOP_K, d), F32),
        compiler_params=_params(("arbitrary",)),
        name="experts",
    )(blk_e, blk_cnt, blk_src, order, h, w1, w3, w2)


def _ple_kernel(x1_ref, y_ref, w_ref, p_ref, gp_ref, wg_ref, wp_ref, gf_ref, o_ref, *, d, final):
    w = w_ref[...]
    x2 = x1_ref[...] + w[:, 0:1] * y_ref[:, 0:d] + w[:, 1:2] * y_ref[:, d:2 * d]
    hn = _rms(x2, gp_ref[...], 1e-6).astype(BF16)
    gate = _sigmoid(jnp.dot(hn, wg_ref[...], preferred_element_type=F32))
    pp = jnp.dot(p_ref[...].astype(BF16), wp_ref[...], preferred_element_type=F32)
    x3 = x2 + gate * pp
    o_ref[...] = _rms(x3, gf_ref[...], 1e-6) if final else x3


def _ple(x1, y2, w, p2d, g_ple, w_gate, w_proj, g_final, final, tm=256):
    n, d = x1.shape
    pd = p2d.shape[1]
    row = lambda i: (i, 0)
    fixed = lambda i: (0, 0)
    return pl.pallas_call(
        functools.partial(_ple_kernel, d=d, final=final),
        grid=(n // tm,),
        in_specs=[
            pl.BlockSpec((tm, d), row),
            pl.BlockSpec((tm, TOP_K * d), row),
            pl.BlockSpec((tm, TOP_K), row),
            pl.BlockSpec((tm, pd), row),
            pl.BlockSpec((1, d), fixed),
            pl.BlockSpec((d, d), fixed),
            pl.BlockSpec((pd, d), fixed),
            pl.BlockSpec((1, d), fixed),
        ],
        out_specs=pl.BlockSpec((tm, d), row),
        out_shape=jax.ShapeDtypeStruct((n, d), F32),
        compiler_params=_params(("parallel",)),
        name="ple_final",
    )(x1, y2, w, p2d, g_ple.reshape(1, d).astype(F32), w_gate.astype(BF16), w_proj.astype(BF16),
      g_final.reshape(1, d).astype(F32))


def kernel(x, p, rel_bias, g_mix, w_in, lam_q1, lam_k1, lam_q2, lam_k2, subln_g, ssm_lam_re, ssm_lam_im, ssm_log_dt, ssm_b_re, ssm_b_im, ssm_c_re, ssm_c_im, ssm_d, w_glu, b_glu, ssm_norm_g, w_o, g_ffn, w_router_g, b_router_g, w_router_e, b_router_e, w1, w3, w2, g_ple, w_ple_gate, w_ple_proj, g_final):
    bt, s_len, d = x.shape
    n = bt * s_len
    depth = g_mix.shape[0]
    attn_w = N_HEADS * DV
    xc = x.reshape(n, d).astype(F32)
    for i in range(depth):
        lam_init = 0.8 - 0.6 * math.exp(-0.3 * i)
        col_scale = jnp.concatenate([jnp.full((attn_w,), DK ** -0.5, F32),
                                     jnp.ones((w_in.shape[2] - attn_w,), F32)])
        w_in_b = (w_in[i].astype(F32) * col_scale[None, :]).astype(BF16)
        z = _inproj(xc, g_mix[i].astype(F32), w_in_b)
        z3 = z.reshape(bt, s_len, z.shape[1])

        lam = (jnp.exp(jnp.sum(lam_q1[i].astype(F32) * lam_k1[i].astype(F32)))
               - jnp.exp(jnp.sum(lam_q2[i].astype(F32) * lam_k2[i].astype(F32))) + lam_init)
        a = _attention(z3, rel_bias, lam, subln_g[i], lam_init)

        ops = _ssm_operators(ssm_lam_re[i], ssm_lam_im[i], ssm_log_dt[i], ssm_b_re[i], ssm_b_im[i],
                             ssm_c_re[i], ssm_c_im[i], ssm_d[i])
        s = _ssm(z3[:, :, 3 * attn_w:], ops)

        w_r = jnp.zeros((d, ROUTER_LANES), F32)
        w_r = w_r.at[:, :N_EXPERT_GROUPS].set(w_router_g[i].astype(F32))
        w_r = w_r.at[:, N_EXPERT_GROUPS:N_EXPERT_GROUPS + N_EXPERTS].set(w_router_e[i].astype(F32))
        b_r = jnp.zeros((ROUTER_LANES,), F32)
        b_r = b_r.at[:N_EXPERT_GROUPS].set(b_router_g[i].astype(F32))
        b_r = b_r.at[N_EXPERT_GROUPS:N_EXPERT_GROUPS + N_EXPERTS].set(b_router_e[i].astype(F32))
        x1, h, logits = _postmix(a.reshape(n, attn_w), s.reshape(n, -1), xc, w_glu[i], b_glu[i],
                                 ssm_norm_g[i], w_o[i], g_ffn[i], w_r, b_r)

        gate_w, order, blk_e, blk_cnt, blk_src = _route(logits, EXPERT_ROWS)
        y2 = _experts(h, w1[i].astype(BF16), w3[i].astype(BF16), w2[i].astype(BF16),
                      order, blk_e, blk_cnt, blk_src, EXPERT_ROWS)
        xc = _ple(x1, y2.reshape(n, TOP_K * d), gate_w.astype(F32), p[i].reshape(n, -1).astype(F32),
                  g_ple[i], w_ple_gate[i], w_ple_proj[i], g_final, final=(i == depth - 1))
    return xc.reshape(bt, s_len, d)
```

```python
import functools
import math

import jax
import jax.numpy as jnp
from jax import lax
from jax.experimental import pallas as pl
from jax.experimental.pallas import tpu as pltpu

F32 = jnp.float32
BF16 = jnp.bfloat16
HIGHEST = lax.Precision.HIGHEST

N_HEADS = 8
DK = 64
DV = 128
N_BUCKETS = 32
MAX_DISTANCE = 128
SSM_GROUP = 16
SSM_GROUPS = 64
SSM_STATE = 64
N_EXPERT_GROUPS = 4
EXPERTS_PER_GROUP = 8
N_EXPERTS = 32
TOP_K = 2
MASK_VALUE = -1e30
LOG2E = math.log2(math.e)

SSM_CHUNK = 16
ATTN_TQ = 512
ONES_ROWS = 16
ATTN_CB = 256
ATTN_TK = 256
EXPERT_ROWS = 256
ROUTER_LANES = 128
VMEM_LIMIT = 56 << 20


def _params(semantics, **kw):
    return pltpu.CompilerParams(dimension_semantics=semantics, vmem_limit_bytes=VMEM_LIMIT, **kw)


def _rms(x, g, eps):
    return x * lax.rsqrt(jnp.mean(x * x, axis=-1, keepdims=True) + eps) * g


def _sigmoid(x):
    return 1.0 / (1.0 + jnp.exp(-x))


def _inproj_kernel(x_ref, g_ref, w_ref, o_ref, h_scr):
    @pl.when(pl.program_id(1) == 0)
    def _():
        h_scr[...] = _rms(x_ref[...], g_ref[...], 1e-6).astype(BF16)

    o_ref[...] = jnp.dot(h_scr[...], w_ref[...], preferred_element_type=F32).astype(o_ref.dtype)


def _inproj(x2d, g, w, tm=512, tn=1024):
    n, d = x2d.shape
    pw = w.shape[1]
    return pl.pallas_call(
        _inproj_kernel,
        grid=(n // tm, pw // tn),
        in_specs=[
            pl.BlockSpec((tm, d), lambda i, j: (i, 0)),
            pl.BlockSpec((1, d), lambda i, j: (0, 0)),
            pl.BlockSpec((d, tn), lambda i, j: (0, j)),
        ],
        out_specs=pl.BlockSpec((tm, tn), lambda i, j: (i, j)),
        out_shape=jax.ShapeDtypeStruct((n, pw), BF16),
        scratch_shapes=[pltpu.VMEM((tm, d), BF16)],
        compiler_params=_params(("parallel", "arbitrary")),
        name="inproj",
    )(x2d, g.reshape(1, d), w)


def _attn_kernel(lam_ref, q_ref, k_ref, v_ref, bias_ref, g_ref, o_ref, qs, vt, *state,
                 tq, tk, cb, out_scale):
    n_blk = 2 * tq // cb
    m_scr, acc, s_scr = (state[i * n_blk:(i + 1) * n_blk] for i in range(3))
    qi = pl.program_id(2)
    n_sub = tq // tk
    n_kv = v_ref.shape[1] // tk

    @pl.when(qi == 0)
    def _():
        for c in range(n_kv):
            vt[c, 0:DV, :] = v_ref[0, c * tk:(c + 1) * tk, :].astype(F32).T.astype(BF16)
            vt[c, DV:DV + ONES_ROWS, :] = jnp.ones((ONES_ROWS, tk), BF16)

    q = q_ref[0]
    lane = lax.broadcasted_iota(jnp.int32, q.shape, 1)
    zero = jnp.zeros_like(q)
    qs[0:tq, :] = jnp.where(lane < DK, q, zero)
    qs[tq:2 * tq, :] = jnp.where(lane >= DK, q, zero)
    for c in range(n_blk):
        m_scr[c][...] = jnp.full(m_scr[c].shape, MASK_VALUE, F32)
        acc[c][...] = jnp.zeros(acc[c].shape, F32)

    def scores(j):
        kj = k_ref[0, pl.ds(pl.multiple_of(j * tk, tk), tk), :]
        return [lax.dot_general(kj, qs[c * cb:(c + 1) * cb, :], (((1,), (1,)), ((), ())),
                                preferred_element_type=F32) for c in range(n_blk)]

    def consume(j, bias, prefetch):
        nxt = scores(j + 1) if prefetch else None
        vtj = vt[j]
        alphas, ps = [], []
        for c in range(n_blk):
            s = s_scr[c][...]
            if bias is not None:
                qb = (c * cb) % tq
                s = s + bias[:, qb:qb + cb]
            m_old = m_scr[c][...]
            m_new = jnp.maximum(m_old, jnp.max(s, axis=0, keepdims=True))
            alpha = jnp.exp2(m_old - m_new)
            p = jnp.exp2(s - m_new)
            m_scr[c][...] = m_new
            alphas.append(alpha)
            ps.append(p.astype(BF16))
        pvs = [jnp.dot(vtj, ps[c], preferred_element_type=F32) for c in range(n_blk)]
        for c in range(n_blk):
            acc[c][...] = alphas[c] * acc[c][...] + pvs[c]
        if prefetch:
            for c in range(n_blk):
                s_scr[c][...] = nxt[c]

    first = qi * n_sub - 1
    for c, sc in enumerate(scores(0)):
        s_scr[c][...] = sc

    def far(j, carry):
        consume(j, None, True)
        return carry

    lax.fori_loop(0, jnp.maximum(first, 0), far, 0)

    @pl.when(qi >= 1)
    def _():
        consume(first, bias_ref[0, 0], True)

    for r in range(1, n_sub + 1):
        consume(first + r, bias_ref[0, r], r < n_sub)

    lam = lam_ref[0, 0]
    half = n_blk // 2
    for b in range(half):
        a1, a2 = acc[b], acc[half + b]
        ot = (a1[0:DV, :] / a1[DV:DV + 1, :] - lam * (a2[0:DV, :] / a2[DV:DV + 1, :]))
        ot = ot * lax.rsqrt(jnp.mean(ot * ot, axis=0, keepdims=True) + 1e-5)
        o_ref[0, b * cb:(b + 1) * cb, :] = (ot.T * (g_ref[...] * out_scale)).astype(o_ref.dtype)


def _t5_bucket(n):
    n = jnp.maximum(n, 0)
    max_exact = N_BUCKETS // 2
    nf = jnp.maximum(n, 1).astype(F32)
    large = max_exact + (jnp.log(nf / max_exact) / math.log(MAX_DISTANCE / max_exact)
                         * (N_BUCKETS - max_exact)).astype(jnp.int32)
    large = jnp.minimum(large, N_BUCKETS - 1)
    return jnp.where(n < max_exact, n, large)


def _attn_bias_tiles(rel_bias, tq, tk):
    assert tk >= MAX_DISTANCE and tq % tk == 0
    table = rel_bias.astype(F32)
    rel_table = (table - table[N_BUCKETS - 1][None, :]) * LOG2E
    r = jnp.arange(tk, dtype=jnp.int32)[:, None]
    c = jnp.arange(tq, dtype=jnp.int32)[None, :]
    tiles = []
    for t in range(tq // tk + 1):
        dist = c - (r + (t - 1) * tk)
        b = rel_table[_t5_bucket(dist)].transpose(2, 0, 1)
        tiles.append(jnp.where((dist >= 0)[None], b, MASK_VALUE))
    return jnp.stack(tiles, axis=1)


def _attention(z3, rel_bias, lam, subln_g, lam_init, tq=ATTN_TQ, tk=ATTN_TK, cb=ATTN_CB):
    bt, s_len, _ = z3.shape
    bias = _attn_bias_tiles(rel_bias, tq, tk)
    n_sp = bias.shape[1]
    kern = functools.partial(_attn_kernel, tq=tq, tk=tk, cb=cb, out_scale=1.0 - lam_init)
    n_blk = 2 * tq // cb
    return pl.pallas_call(
        kern,
        grid=(bt, N_HEADS, s_len // tq),
        in_specs=[
            pl.BlockSpec(memory_space=pltpu.SMEM),
            pl.BlockSpec((1, tq, 2 * DK), lambda b, h, i: (b, i, h)),
            pl.BlockSpec((1, s_len, 2 * DK), lambda b, h, i: (b, 0, N_HEADS + h)),
            pl.BlockSpec((1, s_len, DV), lambda b, h, i: (b, 0, 2 * N_HEADS + h)),
            pl.BlockSpec((1, n_sp, tk, tq), lambda b, h, i: (h, 0, 0, 0)),
            pl.BlockSpec((1, DV), lambda b, h, i: (0, 0)),
        ],
        out_specs=pl.BlockSpec((1, tq, DV), lambda b, h, i: (b, i, h)),
        out_shape=jax.ShapeDtypeStruct((bt, s_len, N_HEADS * DV), BF16),
        scratch_shapes=[
            pltpu.VMEM((2 * tq, 2 * DK), BF16),
            pltpu.VMEM((s_len // tk, DV + ONES_ROWS, tk), BF16),
        ] + [pltpu.VMEM((1, cb), F32)] * n_blk + [pltpu.VMEM((DV + ONES_ROWS, cb), F32)] * n_blk
        + [pltpu.VMEM((tk, cb), F32)] * n_blk,
        compiler_params=_params(("parallel", "parallel", "arbitrary")),
        name="diff_attention",
    )(lam.reshape(1, 1).astype(F32), z3, z3, z3, bias, subln_g.reshape(1, DV).astype(F32))


def _ssm_operators(lam_re, lam_im, log_dt, b_re, b_im, c_re, c_im, d_skip):
    L, H, P = SSM_CHUNK, SSM_GROUP, SSM_STATE
    lre = lam_re.astype(F32)
    lim = lam_im.astype(F32)
    dt = jnp.exp(log_dt.astype(F32))[:, None]
    mag = jnp.exp(lre * dt)
    ab_re = mag * jnp.cos(lim * dt)
    ab_im = mag * jnp.sin(lim * dt)
    den = lre * lre + lim * lim
    nr, ni = ab_re - 1.0, ab_im
    cr = ((nr * lre + ni * lim) / den)[..., None]
    ci = ((ni * lre - nr * lim) / den)[..., None]
    bre = b_re.astype(F32)
    bim = b_im.astype(F32)
    bb_re = cr * bre - ci * bim
    bb_im = cr * bim + ci * bre
    cre = c_re.astype(F32)
    cim = c_im.astype(F32)

    tau = jnp.arange(L + 1, dtype=F32)[:, None, None]
    pw_mag = jnp.exp(tau * (lre * dt)[None])
    pw_re = pw_mag * jnp.cos(tau * (lim * dt)[None])
    pw_im = pw_mag * jnp.sin(tau * (lim * dt)[None])

    ca_re = cre[None] * pw_re[:, :, None, :] - cim[None] * pw_im[:, :, None, :]
    ca_im = cre[None] * pw_im[:, :, None, :] + cim[None] * pw_re[:, :, None, :]
    k_tau = (jnp.einsum('tghp,gpk->tghk', ca_re[:L], bb_re, precision=HIGHEST)
             - jnp.einsum('tghp,gpk->tghk', ca_im[:L], bb_im, precision=HIGHEST))
    ti = jnp.arange(L)
    lag = ti[None, :] - ti[:, None]
    k_ij = k_tau[jnp.clip(lag, 0, L - 1)]
    k_ij = jnp.where((lag >= 0)[:, :, None, None, None], k_ij, 0.0)
    t_op = k_ij.transpose(2, 0, 4, 1, 3).reshape(SSM_GROUPS, L * H, L * H)

    rev_re = pw_re[L - 1 - ti]
    rev_im = pw_im[L - 1 - ti]
    w_re = rev_re[..., None] * bb_re[None] - rev_im[..., None] * bb_im[None]
    w_im = rev_re[..., None] * bb_im[None] + rev_im[..., None] * bb_re[None]
    w_op = jnp.concatenate([w_re, w_im], axis=2)
    w_op = w_op.transpose(1, 0, 3, 2).reshape(SSM_GROUPS, L * H, 2 * P)

    e_re = ca_re[1:L + 1]
    e_im = -ca_im[1:L + 1]
    e_op = jnp.concatenate([e_re, e_im], axis=3)
    e_op = e_op.transpose(1, 3, 0, 2).reshape(SSM_GROUPS, 2 * P, L * H)

    a1 = jnp.concatenate([pw_re[L], pw_re[L]], axis=-1)
    a2 = jnp.concatenate([-pw_im[L], pw_im[L]], axis=-1)
    d_flat = jnp.tile(d_skip.astype(F32), (1, L)).reshape(SSM_GROUPS, 1, L * H)
    return t_op.astype(BF16), w_op.astype(BF16), e_op.astype(BF16), a1, a2, d_flat


def _ssm_in_kernel(u_ref, w_ref, v_ref):
    v_ref[...] = jnp.dot(u_ref[...], w_ref[0], preferred_element_type=F32)


def _ssm_scan_kernel(v_ref, a1_ref, a2_ref, o_ref, st):
    @pl.when(pl.program_id(0) == 0)
    def _():
        st[...] = jnp.zeros(st.shape, F32)

    a1 = a1_ref[...][None]
    a2 = a2_ref[...][None]
    n_chunk = v_ref.shape[1]
    half = v_ref.shape[3] // 2

    def body(c, s):
        o_ref[:, pl.ds(c, 1)] = s[:, None].astype(o_ref.dtype)
        v = v_ref[:, pl.ds(c, 1)][:, 0]
        return a1 * s + a2 * pltpu.roll(s, half, axis=2) + v

    st[...] = lax.fori_loop(0, n_chunk, body, st[...])


def _gelu_tanh(x):
    c = math.sqrt(2.0 / math.pi)
    return x * (0.5 * (1.0 + jnp.tanh(c * (x + 0.044715 * (x * x * x)))))


def _ssm_out_kernel(u_ref, t_ref, s_ref, e_ref, d_ref, y_ref):
    u = u_ref[...]
    y = jnp.dot(u, t_ref[0], preferred_element_type=F32)
    y = y + jnp.dot(s_ref[...], e_ref[0], preferred_element_type=F32)
    y = y + d_ref[0] * u.astype(F32)
    y_ref[...] = _gelu_tanh(y).astype(y_ref.dtype)


def _ssm(u, ops, scan_block=32):
    t_op, w_op, e_op, a1, a2, d_flat = ops
    bt, s_len, width = u.shape
    L, H, G, P2 = SSM_CHUNK, SSM_GROUP, SSM_GROUPS, 2 * SSM_STATE
    n_c = s_len // L
    nc = bt * n_c
    lw = L * H
    u2 = u.reshape(bt, n_c, L, G, H).transpose(0, 1, 3, 2, 4).reshape(nc, G * lw)

    v = pl.pallas_call(
        _ssm_in_kernel,
        grid=(G,),
        in_specs=[pl.BlockSpec((nc, lw), lambda g: (0, g)),
                  pl.BlockSpec((1, lw, P2), lambda g: (g, 0, 0))],
        out_specs=pl.BlockSpec((nc, P2), lambda g: (0, g)),
        out_shape=jax.ShapeDtypeStruct((nc, G * P2), F32),
        compiler_params=_params(("parallel",)),
        name="ssm_chunk_state",
    )(u2, w_op)

    cb = min(scan_block, n_c)
    s_prev = pl.pallas_call(
        _ssm_scan_kernel,
        grid=(n_c // cb,),
        in_specs=[pl.BlockSpec((bt, cb, G, P2), lambda c: (0, c, 0, 0)),
                  pl.BlockSpec((G, P2), lambda c: (0, 0)),
                  pl.BlockSpec((G, P2), lambda c: (0, 0))],
        out_specs=pl.BlockSpec((bt, cb, G, P2), lambda c: (0, c, 0, 0)),
        out_shape=jax.ShapeDtypeStruct((bt, n_c, G, P2), BF16),
        scratch_shapes=[pltpu.VMEM((bt, G, P2), F32)],
        compiler_params=_params(("arbitrary",)),
        name="ssm_scan",
    )(v.reshape(bt, n_c, G, P2), a1, a2)

    y2 = pl.pallas_call(
        _ssm_out_kernel,
        grid=(G,),
        in_specs=[pl.BlockSpec((nc, lw), lambda g: (0, g)),
                  pl.BlockSpec((1, lw, lw), lambda g: (g, 0, 0)),
                  pl.BlockSpec((nc, P2), lambda g: (0, g)),
                  pl.BlockSpec((1, P2, lw), lambda g: (g, 0, 0)),
                  pl.BlockSpec((1, 1, lw), lambda g: (g, 0, 0))],
        out_specs=pl.BlockSpec((nc, lw), lambda g: (0, g)),
        out_shape=jax.ShapeDtypeStruct((nc, G * lw), BF16),
        compiler_params=_params(("parallel",)),
        name="ssm_output",
    )(u2, t_op, s_prev.reshape(nc, G * P2), e_op, d_flat)

    return y2.reshape(bt, n_c, G, L, H).transpose(0, 1, 3, 2, 4).reshape(bt, s_len, width)


def _postmix_kernel(a_ref, s_ref, x_ref, wg_ref, bg_ref, gs_ref, woa_ref, wos_ref, gf_ref,
                    wrh_ref, wrl_ref, br_ref, x1_ref, h_ref, lg_ref):
    s = s_ref[...]
    sf = s.astype(F32)
    gate = _sigmoid(jnp.dot(s, wg_ref[...], preferred_element_type=F32) + bg_ref[...])
    sn = _rms(sf * gate, gs_ref[...], 1e-6).astype(BF16)
    x1 = (x_ref[...]
          + jnp.dot(a_ref[...], woa_ref[...], preferred_element_type=F32)
          + jnp.dot(sn, wos_ref[...], preferred_element_type=F32))
    x1_ref[...] = x1
    h = _rms(x1, gf_ref[...], 1e-6)
    h_hi = h.astype(BF16)
    h_lo = (h - h_hi.astype(F32)).astype(BF16)
    h_ref[...] = h_hi.astype(h_ref.dtype)
    lg_ref[...] = (jnp.dot(h_hi, wrh_ref[...], preferred_element_type=F32)
                   + jnp.dot(h_lo, wrh_ref[...], preferred_element_type=F32)
                   + jnp.dot(h_hi, wrl_ref[...], preferred_element_type=F32)
                   + br_ref[...])


def _postmix(a, s, x2d, w_glu, b_glu, g_s, w_o, g_ffn, w_r, b_r, tm=256):
    n, d = x2d.shape
    wa = a.shape[1]
    ws = s.shape[1]
    wr_hi = w_r.astype(BF16)
    wr_lo = (w_r - wr_hi.astype(F32)).astype(BF16)
    row = lambda i: (i, 0)
    fixed = lambda i: (0, 0)
    return pl.pallas_call(
        _postmix_kernel,
        grid=(n // tm,),
        in_specs=[
            pl.BlockSpec((tm, wa), row),
            pl.BlockSpec((tm, ws), row),
            pl.BlockSpec((tm, d), row),
            pl.BlockSpec((ws, ws), fixed),
            pl.BlockSpec((1, ws), fixed),
            pl.BlockSpec((1, ws), fixed),
            pl.BlockSpec((wa, d), fixed),
            pl.BlockSpec((ws, d), fixed),
            pl.BlockSpec((1, d), fixed),
            pl.BlockSpec((d, ROUTER_LANES), fixed),
            pl.BlockSpec((d, ROUTER_LANES), fixed),
            pl.BlockSpec((1, ROUTER_LANES), fixed),
        ],
        out_specs=[pl.BlockSpec((tm, d), row), pl.BlockSpec((tm, d), row),
                   pl.BlockSpec((tm, ROUTER_LANES), row)],
        out_shape=[jax.ShapeDtypeStruct((n, d), F32), jax.ShapeDtypeStruct((n, d), F32),
                   jax.ShapeDtypeStruct((n, ROUTER_LANES), F32)],
        compiler_params=_params(("parallel",)),
        name="postmix",
    )(a, s, x2d, w_glu.astype(BF16), b_glu.reshape(1, ws).astype(F32), g_s.reshape(1, ws).astype(F32),
      w_o[:wa].astype(BF16), w_o[wa:].astype(BF16), g_ffn.reshape(1, d).astype(F32),
      wr_hi, wr_lo, b_r.reshape(1, ROUTER_LANES).astype(F32))


def _route(logits, rows):
    n_tok = logits.shape[0]
    lg = logits[:, :N_EXPERT_GROUPS]
    le = logits[:, N_EXPERT_GROUPS:N_EXPERT_GROUPS + N_EXPERTS].reshape(
        n_tok, N_EXPERT_GROUPS, EXPERTS_PER_GROUP)
    pg = jax.nn.softmax(lg, axis=-1)
    gsel = jnp.argmax(lg, axis=-1).astype(jnp.int32)
    gate_g = jnp.max(pg, axis=-1, keepdims=True)
    sel = gsel[:, None] == jnp.arange(N_EXPERT_GROUPS, dtype=jnp.int32)[None, :]
    le_sel = jnp.sum(jnp.where(sel[:, :, None], le, 0.0), axis=1)
    pe = jax.nn.softmax(le_sel, axis=-1)
    top_p, top_i = lax.top_k(pe, TOP_K)
    w = gate_g * top_p / jnp.sum(top_p, axis=-1, keepdims=True)
    eid = gsel[:, None] * EXPERTS_PER_GROUP + top_i.astype(jnp.int32)

    n_assign = n_tok * TOP_K
    flat_e = eid.reshape(-1)
    counts = jnp.sum((flat_e[:, None] == jnp.arange(N_EXPERTS, dtype=jnp.int32)[None, :])
                     .astype(jnp.int32), axis=0)
    n_blk_e = (counts + rows - 1) // rows
    blk_end = jnp.cumsum(n_blk_e)
    blk_first = blk_end - n_blk_e
    start = jnp.cumsum(counts) - counts
    order = jnp.argsort(flat_e, stable=True).astype(jnp.int32)
    n_blk = (n_assign + rows - 1) // rows + N_EXPERTS
    blk = jnp.arange(n_blk, dtype=jnp.int32)
    blk_e = jnp.minimum(jnp.searchsorted(blk_end, blk, side='right'), N_EXPERTS - 1).astype(jnp.int32)
    used = blk < blk_end[-1]
    in_e = (blk - blk_first[blk_e]) * rows
    blk_cnt = jnp.where(used, jnp.clip(counts[blk_e] - in_e, 0, rows), 0).astype(jnp.int32)
    blk_src = (start[blk_e] + in_e).astype(jnp.int32)
    last_e = blk_e[jnp.maximum(blk_end[-1] - 1, 0)]
    blk_e = jnp.where(used, blk_e, last_e)
    return w, order, blk_e, blk_cnt, blk_src


def _expert_kernel(blk_e_ref, blk_cnt_ref, blk_src_ref, order_ref,
                   h_hbm, w1_ref, w3_ref, w2_ref, y_hbm, xbuf, ybuf, gsem, ssem):
    del blk_e_ref
    b = pl.program_id(0)
    cnt = blk_cnt_ref[b]
    src = blk_src_ref[b]

    @pl.when(b == 0)
    def _():
        xbuf[...] = jnp.zeros(xbuf.shape, xbuf.dtype)

    def gather_copy(r):
        a = order_ref[src + r]
        return pltpu.make_async_copy(h_hbm.at[pl.ds(a // TOP_K, 1)], xbuf.at[pl.ds(r, 1)], gsem)

    def scatter_copy(r):
        a = order_ref[src + r]
        return pltpu.make_async_copy(ybuf.at[pl.ds(r, 1)], y_hbm.at[pl.ds(a, 1)], ssem)

    def start(copy):
        def body(r, c):
            copy(r).start()
            return c
        return body

    def wait(copy):
        def body(r, c):
            copy(r).wait()
            return c
        return body

    @pl.when(cnt > 0)
    def _():
        lax.fori_loop(0, cnt, start(gather_copy), 0)
        lax.fori_loop(0, cnt, wait(gather_copy), 0)
        x = xbuf[...].astype(BF16)
        h1 = jnp.dot(x, w1_ref[0], preferred_element_type=F32)
        h3 = jnp.dot(x, w3_ref[0], preferred_element_type=F32)
        act = (h1 * _sigmoid(h1) * h3).astype(BF16)
        ybuf[...] = jnp.dot(act, w2_ref[0], preferred_element_type=F32)
        lax.fori_loop(0, cnt, start(scatter_copy), 0)
        lax.fori_loop(0, cnt, wait(scatter_copy), 0)


def _experts(h, w1, w3, w2, order, blk_e, blk_cnt, blk_src, rows):
    n, d = h.shape
    f = w1.shape[2]
    n_blk = blk_e.shape[0]
    wmap = lambda b, be, bc, bs, od: (be[b], 0, 0)
    grid_spec = pltpu.PrefetchScalarGridSpec(
        num_scalar_prefetch=4,
        grid=(n_blk,),
        in_specs=[
            pl.BlockSpec(memory_space=pl.ANY),
            pl.BlockSpec((1, d, f), wmap),
            pl.BlockSpec((1, d, f), wmap),
            pl.BlockSpec((1, f, d), wmap),
        ],
        out_specs=pl.BlockSpec(memory_space=pl.ANY),
        scratch_shapes=[
            pltpu.VMEM((rows, d), F32),
            pltpu.VMEM((rows, d), F32),
            pltpu.SemaphoreType.DMA,
            pltpu.SemaphoreType.DMA,
        ],
    )
    return pl.pallas_call(
        _expert_kernel,
        grid_spec=grid_spec,
        out_shape=jax.ShapeDtypeStruct((n * TOP_K, d), F32),
        compiler_params=_params(("arbitrary",)),
        name="experts",
    )(blk_e, blk_cnt, blk_src, order, h, w1, w3, w2)


def _ple_kernel(x1_ref, y_ref, w_ref, p_ref, gp_ref, wg_ref, wp_ref, gf_ref, o_ref, *, d, final):
    w = w_ref[...]
    x2 = x1_ref[...] + w[:, 0:1] * y_ref[:, 0:d] + w[:, 1:2] * y_ref[:, d:2 * d]
    hn = _rms(x2, gp_ref[...], 1e-6).astype(BF16)
    gate = _sigmoid(jnp.dot(hn, wg_ref[...], preferred_element_type=F32))
    pp = jnp.dot(p_ref[...].astype(BF16), wp_ref[...], preferred_element_type=F32)
    x3 = x2 + gate * pp
    o_ref[...] = _rms(x3, gf_ref[...], 1e-6) if final else x3


def _ple(x1, y2, w, p2d, g_ple, w_gate, w_proj, g_final, final, tm=256):
    n, d = x1.shape
    pd = p2d.shape[1]
    row = lambda i: (i, 0)
    fixed = lambda i: (0, 0)
    return pl.pallas_call(
        functools.partial(_ple_kernel, d=d, final=final),
        grid=(n // tm,),
        in_specs=[
            pl.BlockSpec((tm, d), row),
            pl.BlockSpec((tm, TOP_K * d), row),
            pl.BlockSpec((tm, TOP_K), row),
            pl.BlockSpec((tm, pd), row),
            pl.BlockSpec((1, d), fixed),
            pl.BlockSpec((d, d), fixed),
            pl.BlockSpec((pd, d), fixed),
            pl.BlockSpec((1, d), fixed),
        ],
        out_specs=pl.BlockSpec((tm, d), row),
        out_shape=jax.ShapeDtypeStruct((n, d), F32),
        compiler_params=_params(("parallel",)),
        name="ple_final",
    )(x1, y2, w, p2d, g_ple.reshape(1, d).astype(F32), w_gate.astype(BF16), w_proj.astype(BF16),
      g_final.reshape(1, d).astype(F32))


def kernel(x, p, rel_bias, g_mix, w_in, lam_q1, lam_k1, lam_q2, lam_k2, subln_g, ssm_lam_re, ssm_lam_im, ssm_log_dt, ssm_b_re, ssm_b_im, ssm_c_re, ssm_c_im, ssm_d, w_glu, b_glu, ssm_norm_g, w_o, g_ffn, w_router_g, b_router_g, w_router_e, b_router_e, w1, w3, w2, g_ple, w_ple_gate, w_ple_proj, g_final):
    bt, s_len, d = x.shape
    n = bt * s_len
    depth = g_mix.shape[0]
    attn_w = N_HEADS * DV
    xc = x.reshape(n, d).astype(F32)
    for i in range(depth):
        lam_init = 0.8 - 0.6 * math.exp(-0.3 * i)
        col_scale = jnp.concatenate([jnp.full((attn_w,), LOG2E * DK ** -0.5, F32),
                                     jnp.ones((w_in.shape[2] - attn_w,), F32)])
        w_in_b = (w_in[i].astype(F32) * col_scale[None, :]).astype(BF16)
        z = _inproj(xc, g_mix[i].astype(F32), w_in_b)
        z3 = z.reshape(bt, s_len, z.shape[1])

        lam = (jnp.exp(jnp.sum(lam_q1[i].astype(F32) * lam_k1[i].astype(F32)))
               - jnp.exp(jnp.sum(lam_q2[i].astype(F32) * lam_k2[i].astype(F32))) + lam_init)
        a = _attention(z3, rel_bias, lam, subln_g[i], lam_init)

        ops = _ssm_operators(ssm_lam_re[i], ssm_lam_im[i], ssm_log_dt[i], ssm_b_re[i], ssm_b_im[i],
                             ssm_c_re[i], ssm_c_im[i], ssm_d[i])
        s = _ssm(z3[:, :, 3 * attn_w:], ops)

        w_r = jnp.zeros((d, ROUTER_LANES), F32)
        w_r = w_r.at[:, :N_EXPERT_GROUPS].set(w_router_g[i].astype(F32))
        w_r = w_r.at[:, N_EXPERT_GROUPS:N_EXPERT_GROUPS + N_EXPERTS].set(w_router_e[i].astype(F32))
        b_r = jnp.zeros((ROUTER_LANES,), F32)
        b_r = b_r.at[:N_EXPERT_GROUPS].set(b_router_g[i].astype(F32))
        b_r = b_r.at[N_EXPERT_GROUPS:N_EXPERT_GROUPS + N_EXPERTS].set(b_router_e[i].astype(F32))
        x1, h, logits = _postmix(a.reshape(n, attn_w), s.reshape(n, -1), xc, w_glu[i], b_glu[i],
                                 ssm_norm_g[i], w_o[i], g_ffn[i], w_r, b_r)

        gate_w, order, blk_e, blk_cnt, blk_src = _route(logits, EXPERT_ROWS)
        y2 = _experts(h, w1[i].astype(BF16), w3[i].astype(BF16), w2[i].astype(BF16),
                      order, blk_e, blk_cnt, blk_src, EXPERT_ROWS)
        xc = _ple(x1, y2.reshape(n, TOP_K * d), gate_w.astype(F32), p[i].reshape(n, -1).astype(F32),
                  g_ple[i], w_ple_gate[i], w_ple_proj[i], g_final, final=(i == depth - 1))
    return xc.reshape(bt, s_len, d)
```

```python
import functools
import math

import jax
import jax.numpy as jnp
from jax import lax
from jax.experimental import pallas as pl
from jax.experimental.pallas import tpu as pltpu

F32 = jnp.float32
BF16 = jnp.bfloat16
HIGHEST = lax.Precision.HIGHEST

N_HEADS = 8
DK = 64
DV = 128
N_BUCKETS = 32
MAX_DISTANCE = 128
SSM_GROUP = 16
SSM_GROUPS = 64
SSM_STATE = 64
N_EXPERT_GROUPS = 4
EXPERTS_PER_GROUP = 8
N_EXPERTS = 32
TOP_K = 2
MASK_VALUE = -1e30
LOG2E = math.log2(math.e)

SSM_CHUNK = 16
ATTN_TQ = 512
ONES_ROWS = 16
ATTN_CB = 256
ATTN_TK = 256
EXPERT_ROWS = 256
ISSUE_UNROLL = 8
LANES = 128
ROUTER_LANES = 128
VMEM_LIMIT = 56 << 20


def _params(semantics, **kw):
    return pltpu.CompilerParams(dimension_semantics=semantics, vmem_limit_bytes=VMEM_LIMIT, **kw)


def _rms(x, g, eps):
    return x * lax.rsqrt(jnp.mean(x * x, axis=-1, keepdims=True) + eps) * g


def _sigmoid(x):
    return 1.0 / (1.0 + jnp.exp(-x))


def _inproj_kernel(x_ref, g_ref, w_ref, o_ref, h_scr):
    @pl.when(pl.program_id(1) == 0)
    def _():
        h_scr[...] = _rms(x_ref[...], g_ref[...], 1e-6).astype(BF16)

    o_ref[...] = jnp.dot(h_scr[...], w_ref[...], preferred_element_type=F32).astype(o_ref.dtype)


def _inproj(x2d, g, w, tm=512, tn=1024):
    n, d = x2d.shape
    pw = w.shape[1]
    return pl.pallas_call(
        _inproj_kernel,
        grid=(n // tm, pw // tn),
        in_specs=[
            pl.BlockSpec((tm, d), lambda i, j: (i, 0)),
            pl.BlockSpec((1, d), lambda i, j: (0, 0)),
            pl.BlockSpec((d, tn), lambda i, j: (0, j)),
        ],
        out_specs=pl.BlockSpec((tm, tn), lambda i, j: (i, j)),
        out_shape=jax.ShapeDtypeStruct((n, pw), BF16),
        scratch_shapes=[pltpu.VMEM((tm, d), BF16)],
        compiler_params=_params(("parallel", "arbitrary")),
        name="inproj",
    )(x2d, g.reshape(1, d), w)


def _attn_kernel(lam_ref, q_ref, k_ref, v_ref, bias_ref, g_ref, o_ref, qs, vt, *state,
                 tq, tk, cb, out_scale):
    n_blk = 2 * tq // cb
    m_scr, acc, s_scr = (state[i * n_blk:(i + 1) * n_blk] for i in range(3))
    qi = pl.program_id(2)
    n_sub = tq // tk
    n_kv = v_ref.shape[1] // tk

    @pl.when(qi == 0)
    def _():
        for c in range(n_kv):
            vt[c, 0:DV, :] = v_ref[0, c * tk:(c + 1) * tk, :].astype(F32).T.astype(BF16)
            vt[c, DV:DV + ONES_ROWS, :] = jnp.ones((ONES_ROWS, tk), BF16)

    q = q_ref[0]
    lane = lax.broadcasted_iota(jnp.int32, q.shape, 1)
    zero = jnp.zeros_like(q)
    qs[0:tq, :] = jnp.where(lane < DK, q, zero)
    qs[tq:2 * tq, :] = jnp.where(lane >= DK, q, zero)
    for c in range(n_blk):
        m_scr[c][...] = jnp.full(m_scr[c].shape, MASK_VALUE, F32)
        acc[c][...] = jnp.zeros(acc[c].shape, F32)

    def scores(j):
        kj = k_ref[0, pl.ds(pl.multiple_of(j * tk, tk), tk), :]
        return [lax.dot_general(kj, qs[c * cb:(c + 1) * cb, :], (((1,), (1,)), ((), ())),
                                preferred_element_type=F32) for c in range(n_blk)]

    def consume(j, bias, prefetch):
        nxt = scores(j + 1) if prefetch else None
        vtj = vt[j]
        alphas, ps = [], []
        for c in range(n_blk):
            s = s_scr[c][...]
            if bias is not None:
                qb = (c * cb) % tq
                s = s + bias[:, qb:qb + cb]
            m_old = m_scr[c][...]
            m_new = jnp.maximum(m_old, jnp.max(s, axis=0, keepdims=True))
            alpha = jnp.exp2(m_old - m_new)
            p = jnp.exp2(s - m_new)
            m_scr[c][...] = m_new
            alphas.append(alpha)
            ps.append(p.astype(BF16))
        pvs = [jnp.dot(vtj, ps[c], preferred_element_type=F32) for c in range(n_blk)]
        for c in range(n_blk):
            acc[c][...] = alphas[c] * acc[c][...] + pvs[c]
        if prefetch:
            for c in range(n_blk):
                s_scr[c][...] = nxt[c]

    first = qi * n_sub - 1
    for c, sc in enumerate(scores(0)):
        s_scr[c][...] = sc

    def far(j, carry):
        consume(j, None, True)
        return carry

    lax.fori_loop(0, jnp.maximum(first, 0), far, 0)

    @pl.when(qi >= 1)
    def _():
        consume(first, bias_ref[0, 0], True)

    for r in range(1, n_sub + 1):
        consume(first + r, bias_ref[0, r], r < n_sub)

    lam = lam_ref[0, 0]
    half = n_blk // 2
    for b in range(half):
        a1, a2 = acc[b], acc[half + b]
        ot = (a1[0:DV, :] / a1[DV:DV + 1, :] - lam * (a2[0:DV, :] / a2[DV:DV + 1, :]))
        ot = ot * lax.rsqrt(jnp.mean(ot * ot, axis=0, keepdims=True) + 1e-5)
        o_ref[0, b * cb:(b + 1) * cb, :] = (ot.T * (g_ref[...] * out_scale)).astype(o_ref.dtype)


def _t5_bucket(n):
    n = jnp.maximum(n, 0)
    max_exact = N_BUCKETS // 2
    nf = jnp.maximum(n, 1).astype(F32)
    large = max_exact + (jnp.log(nf / max_exact) / math.log(MAX_DISTANCE / max_exact)
                         * (N_BUCKETS - max_exact)).astype(jnp.int32)
    large = jnp.minimum(large, N_BUCKETS - 1)
    return jnp.where(n < max_exact, n, large)


def _attn_bias_tiles(rel_bias, tq, tk):
    assert tk >= MAX_DISTANCE and tq % tk == 0
    table = rel_bias.astype(F32)
    rel_table = (table - table[N_BUCKETS - 1][None, :]) * LOG2E
    r = jnp.arange(tk, dtype=jnp.int32)[:, None]
    c = jnp.arange(tq, dtype=jnp.int32)[None, :]
    tiles = []
    for t in range(tq // tk + 1):
        dist = c - (r + (t - 1) * tk)
        onehot = (_t5_bucket(dist)[:, :, None] == jnp.arange(N_BUCKETS, dtype=jnp.int32)).astype(F32)
        b = jnp.einsum('rcn,nh->hrc', onehot, rel_table, precision=HIGHEST)
        tiles.append(jnp.where((dist >= 0)[None], b, MASK_VALUE))
    return jnp.stack(tiles, axis=1)


def _attention(z3, rel_bias, lam, subln_g, lam_init, tq=ATTN_TQ, tk=ATTN_TK, cb=ATTN_CB):
    bt, s_len, _ = z3.shape
    bias = _attn_bias_tiles(rel_bias, tq, tk)
    n_sp = bias.shape[1]
    kern = functools.partial(_attn_kernel, tq=tq, tk=tk, cb=cb, out_scale=1.0 - lam_init)
    n_blk = 2 * tq // cb
    return pl.pallas_call(
        kern,
        grid=(bt, N_HEADS, s_len // tq),
        in_specs=[
            pl.BlockSpec(memory_space=pltpu.SMEM),
            pl.BlockSpec((1, tq, 2 * DK), lambda b, h, i: (b, i, h)),
            pl.BlockSpec((1, s_len, 2 * DK), lambda b, h, i: (b, 0, N_HEADS + h)),
            pl.BlockSpec((1, s_len, DV), lambda b, h, i: (b, 0, 2 * N_HEADS + h)),
            pl.BlockSpec((1, n_sp, tk, tq), lambda b, h, i: (h, 0, 0, 0)),
            pl.BlockSpec((1, DV), lambda b, h, i: (0, 0)),
        ],
        out_specs=pl.BlockSpec((1, tq, DV), lambda b, h, i: (b, i, h)),
        out_shape=jax.ShapeDtypeStruct((bt, s_len, N_HEADS * DV), BF16),
        scratch_shapes=[
            pltpu.VMEM((2 * tq, 2 * DK), BF16),
            pltpu.VMEM((s_len // tk, DV + ONES_ROWS, tk), BF16),
        ] + [pltpu.VMEM((1, cb), F32)] * n_blk + [pltpu.VMEM((DV + ONES_ROWS, cb), F32)] * n_blk
        + [pltpu.VMEM((tk, cb), F32)] * n_blk,
        compiler_params=_params(("parallel", "parallel", "arbitrary")),
        name="diff_attention",
    )(lam.reshape(1, 1).astype(F32), z3, z3, z3, bias, subln_g.reshape(1, DV).astype(F32))


def _ssm_operators(lam_re, lam_im, log_dt, b_re, b_im, c_re, c_im, d_skip):
    L, H, P = SSM_CHUNK, SSM_GROUP, SSM_STATE
    lre = lam_re.astype(F32)
    lim = lam_im.astype(F32)
    dt = jnp.exp(log_dt.astype(F32))[:, None]
    mag = jnp.exp(lre * dt)
    ab_re = mag * jnp.cos(lim * dt)
    ab_im = mag * jnp.sin(lim * dt)
    den = lre * lre + lim * lim
    nr, ni = ab_re - 1.0, ab_im
    cr = ((nr * lre + ni * lim) / den)[..., None]
    ci = ((ni * lre - nr * lim) / den)[..., None]
    bre = b_re.astype(F32)
    bim = b_im.astype(F32)
    bb_re = cr * bre - ci * bim
    bb_im = cr * bim + ci * bre
    cre = c_re.astype(F32)
    cim = c_im.astype(F32)

    tau = jnp.arange(L + 1, dtype=F32)[:, None, None]
    pw_mag = jnp.exp(tau * (lre * dt)[None])
    pw_re = pw_mag * jnp.cos(tau * (lim * dt)[None])
    pw_im = pw_mag * jnp.sin(tau * (lim * dt)[None])

    ca_re = cre[None] * pw_re[:, :, None, :] - cim[None] * pw_im[:, :, None, :]
    ca_im = cre[None] * pw_im[:, :, None, :] + cim[None] * pw_re[:, :, None, :]
    k_tau = (jnp.einsum('tghp,gpk->tghk', ca_re[:L], bb_re, precision=HIGHEST)
             - jnp.einsum('tghp,gpk->tghk', ca_im[:L], bb_im, precision=HIGHEST))
    ti = jnp.arange(L)
    lag = ti[None, :] - ti[:, None]
    lag_sel = (lag[:, :, None] == ti[None, None, :]).astype(F32)
    k_ij = jnp.einsum('ijt,tghk->ijghk', lag_sel, k_tau, precision=HIGHEST)
    t_op = k_ij.transpose(2, 0, 4, 1, 3).reshape(SSM_GROUPS, L * H, L * H)

    rev_re = pw_re[L - 1 - ti]
    rev_im = pw_im[L - 1 - ti]
    w_re = rev_re[..., None] * bb_re[None] - rev_im[..., None] * bb_im[None]
    w_im = rev_re[..., None] * bb_im[None] + rev_im[..., None] * bb_re[None]
    w_op = jnp.concatenate([w_re, w_im], axis=2)
    w_op = w_op.transpose(1, 0, 3, 2).reshape(SSM_GROUPS, L * H, 2 * P)

    e_re = ca_re[1:L + 1]
    e_im = -ca_im[1:L + 1]
    e_op = jnp.concatenate([e_re, e_im], axis=3)
    e_op = e_op.transpose(1, 3, 0, 2).reshape(SSM_GROUPS, 2 * P, L * H)

    a1 = jnp.concatenate([pw_re[L], pw_re[L]], axis=-1)
    a2 = jnp.concatenate([-pw_im[L], pw_im[L]], axis=-1)
    d_flat = jnp.tile(d_skip.astype(F32), (1, L)).reshape(SSM_GROUPS, 1, L * H)
    return t_op.astype(BF16), w_op.astype(BF16), e_op.astype(BF16), a1, a2, d_flat


def _ssm_in_kernel(u_ref, w_ref, v_ref):
    v_ref[...] = jnp.dot(u_ref[...], w_ref[0], preferred_element_type=F32)


def _ssm_scan_kernel(v_ref, a1_ref, a2_ref, o_ref, st):
    @pl.when(pl.program_id(0) == 0)
    def _():
        st[...] = jnp.zeros(st.shape, F32)

    a1 = a1_ref[...][None]
    a2 = a2_ref[...][None]
    n_chunk = v_ref.shape[1]
    half = v_ref.shape[3] // 2

    def body(c, s):
        o_ref[:, pl.ds(c, 1)] = s[:, None].astype(o_ref.dtype)
        v = v_ref[:, pl.ds(c, 1)][:, 0]
        return a1 * s + a2 * pltpu.roll(s, half, axis=2) + v

    st[...] = lax.fori_loop(0, n_chunk, body, st[...])


def _gelu_tanh(x):
    c = math.sqrt(2.0 / math.pi)
    return x * (0.5 * (1.0 + jnp.tanh(c * (x + 0.044715 * (x * x * x)))))


def _ssm_out_kernel(u_ref, t_ref, s_ref, e_ref, d_ref, y_ref):
    u = u_ref[...]
    y = jnp.dot(u, t_ref[0], preferred_element_type=F32)
    y = y + jnp.dot(s_ref[...], e_ref[0], preferred_element_type=F32)
    y = y + d_ref[0] * u.astype(F32)
    y_ref[...] = _gelu_tanh(y).astype(y_ref.dtype)


def _ssm(u, ops, scan_block=32):
    t_op, w_op, e_op, a1, a2, d_flat = ops
    bt, s_len, width = u.shape
    L, H, G, P2 = SSM_CHUNK, SSM_GROUP, SSM_GROUPS, 2 * SSM_STATE
    n_c = s_len // L
    nc = bt * n_c
    lw = L * H
    u2 = u.reshape(bt, n_c, L, G, H).transpose(0, 1, 3, 2, 4).reshape(nc, G * lw)

    v = pl.pallas_call(
        _ssm_in_kernel,
        grid=(G,),
        in_specs=[pl.BlockSpec((nc, lw), lambda g: (0, g)),
                  pl.BlockSpec((1, lw, P2), lambda g: (g, 0, 0))],
        out_specs=pl.BlockSpec((nc, P2), lambda g: (0, g)),
        out_shape=jax.ShapeDtypeStruct((nc, G * P2), F32),
        compiler_params=_params(("parallel",)),
        name="ssm_chunk_state",
    )(u2, w_op)

    cb = min(scan_block, n_c)
    s_prev = pl.pallas_call(
        _ssm_scan_kernel,
        grid=(n_c // cb,),
        in_specs=[pl.BlockSpec((bt, cb, G, P2), lambda c: (0, c, 0, 0)),
                  pl.BlockSpec((G, P2), lambda c: (0, 0)),
                  pl.BlockSpec((G, P2), lambda c: (0, 0))],
        out_specs=pl.BlockSpec((bt, cb, G, P2), lambda c: (0, c, 0, 0)),
        out_shape=jax.ShapeDtypeStruct((bt, n_c, G, P2), BF16),
        scratch_shapes=[pltpu.VMEM((bt, G, P2), F32)],
        compiler_params=_params(("arbitrary",)),
        name="ssm_scan",
    )(v.reshape(bt, n_c, G, P2), a1, a2)

    y2 = pl.pallas_call(
        _ssm_out_kernel,
        grid=(G,),
        in_specs=[pl.BlockSpec((nc, lw), lambda g: (0, g)),
                  pl.BlockSpec((1, lw, lw), lambda g: (g, 0, 0)),
                  pl.BlockSpec((nc, P2), lambda g: (0, g)),
                  pl.BlockSpec((1, P2, lw), lambda g: (g, 0, 0)),
                  pl.BlockSpec((1, 1, lw), lambda g: (g, 0, 0))],
        out_specs=pl.BlockSpec((nc, lw), lambda g: (0, g)),
        out_shape=jax.ShapeDtypeStruct((nc, G * lw), BF16),
        compiler_params=_params(("parallel",)),
        name="ssm_output",
    )(u2, t_op, s_prev.reshape(nc, G * P2), e_op, d_flat)

    return y2.reshape(bt, n_c, G, L, H).transpose(0, 1, 3, 2, 4).reshape(bt, s_len, width)


def _postmix_kernel(a_ref, s_ref, x_ref, wg_ref, bg_ref, gs_ref, woa_ref, wos_ref, gf_ref,
                    wrh_ref, wrl_ref, br_ref, x1_ref, h_ref, lg_ref):
    s = s_ref[...]
    sf = s.astype(F32)
    gate = _sigmoid(jnp.dot(s, wg_ref[...], preferred_element_type=F32) + bg_ref[...])
    sn = _rms(sf * gate, gs_ref[...], 1e-6).astype(BF16)
    x1 = (x_ref[...]
          + jnp.dot(a_ref[...], woa_ref[...], preferred_element_type=F32)
          + jnp.dot(sn, wos_ref[...], preferred_element_type=F32))
    x1_ref[...] = x1
    h = _rms(x1, gf_ref[...], 1e-6)
    h_hi = h.astype(BF16)
    h_lo = (h - h_hi.astype(F32)).astype(BF16)
    hf = h_hi.astype(F32)
    for j in range(hf.shape[1] // LANES):
        h_ref[pl.ds(j, hf.shape[0], stride=hf.shape[1] // LANES), :] = hf[:, j * LANES:(j + 1) * LANES]
    lg_ref[...] = (jnp.dot(h_hi, wrh_ref[...], preferred_element_type=F32)
                   + jnp.dot(h_lo, wrh_ref[...], preferred_element_type=F32)
                   + jnp.dot(h_hi, wrl_ref[...], preferred_element_type=F32)
                   + br_ref[...])


def _postmix(a, s, x2d, w_glu, b_glu, g_s, w_o, g_ffn, w_r, b_r, tm=256):
    n, d = x2d.shape
    wa = a.shape[1]
    ws = s.shape[1]
    wr_hi = w_r.astype(BF16)
    wr_lo = (w_r - wr_hi.astype(F32)).astype(BF16)
    row = lambda i: (i, 0)
    fixed = lambda i: (0, 0)
    return pl.pallas_call(
        _postmix_kernel,
        grid=(n // tm,),
        in_specs=[
            pl.BlockSpec((tm, wa), row),
            pl.BlockSpec((tm, ws), row),
            pl.BlockSpec((tm, d), row),
            pl.BlockSpec((ws, ws), fixed),
            pl.BlockSpec((1, ws), fixed),
            pl.BlockSpec((1, ws), fixed),
            pl.BlockSpec((wa, d), fixed),
            pl.BlockSpec((ws, d), fixed),
            pl.BlockSpec((1, d), fixed),
            pl.BlockSpec((d, ROUTER_LANES), fixed),
            pl.BlockSpec((d, ROUTER_LANES), fixed),
            pl.BlockSpec((1, ROUTER_LANES), fixed),
        ],
        out_specs=[pl.BlockSpec((tm, d), row), pl.BlockSpec((tm * (d // LANES), LANES), row),
                   pl.BlockSpec((tm, ROUTER_LANES), row)],
        out_shape=[jax.ShapeDtypeStruct((n, d), F32), jax.ShapeDtypeStruct((n * (d // LANES), LANES), F32),
                   jax.ShapeDtypeStruct((n, ROUTER_LANES), F32)],
        compiler_params=_params(("parallel",)),
        name="postmix",
    )(a, s, x2d, w_glu.astype(BF16), b_glu.reshape(1, ws).astype(F32), g_s.reshape(1, ws).astype(F32),
      w_o[:wa].astype(BF16), w_o[wa:].astype(BF16), g_ffn.reshape(1, d).astype(F32),
      wr_hi, wr_lo, b_r.reshape(1, ROUTER_LANES).astype(F32))


def _route(logits, rows, slab_rows):
    n_tok = logits.shape[0]
    lg = logits[:, :N_EXPERT_GROUPS]
    le = logits[:, N_EXPERT_GROUPS:N_EXPERT_GROUPS + N_EXPERTS].reshape(
        n_tok, N_EXPERT_GROUPS, EXPERTS_PER_GROUP)
    pg = jax.nn.softmax(lg, axis=-1)
    gsel = jnp.argmax(lg, axis=-1).astype(jnp.int32)
    gate_g = jnp.max(pg, axis=-1, keepdims=True)
    sel = gsel[:, None] == jnp.arange(N_EXPERT_GROUPS, dtype=jnp.int32)[None, :]
    le_sel = jnp.sum(jnp.where(sel[:, :, None], le, 0.0), axis=1)
    pe = jax.nn.softmax(le_sel, axis=-1)
    top_p, top_i = lax.top_k(pe, TOP_K)
    w = gate_g * top_p / jnp.sum(top_p, axis=-1, keepdims=True)
    eid = gsel[:, None] * EXPERTS_PER_GROUP + top_i.astype(jnp.int32)

    n_assign = n_tok * TOP_K
    flat_e = eid.reshape(-1)
    counts = jnp.sum((flat_e[:, None] == jnp.arange(N_EXPERTS, dtype=jnp.int32)[None, :])
                     .astype(jnp.int32), axis=0)
    n_blk_e = (counts + rows - 1) // rows
    blk_end = jnp.cumsum(n_blk_e)
    blk_first = blk_end - n_blk_e
    start = jnp.cumsum(counts) - counts
    order = jnp.argsort(flat_e, stable=True).astype(jnp.int32)
    n_blk = (n_assign + rows - 1) // rows + N_EXPERTS
    blk = jnp.arange(n_blk, dtype=jnp.int32)
    blk_e = jnp.minimum(jnp.sum((blk[:, None] >= blk_end[None, :]).astype(jnp.int32), axis=1),
                        N_EXPERTS - 1)
    used = blk < blk_end[-1]
    in_e = (blk - blk_first[blk_e]) * rows
    blk_cnt = jnp.where(used, jnp.clip(counts[blk_e] - in_e, 0, rows), 0).astype(jnp.int32)
    blk_src = (start[blk_e] + in_e).astype(jnp.int32)
    last_e = blk_e[jnp.maximum(blk_end[-1] - 1, 0)]
    blk_e = jnp.where(used, blk_e, last_e)
    row_tok = order // TOP_K
    row_dst = (order % TOP_K) * n_tok + row_tok
    return w, row_tok * slab_rows, row_dst * slab_rows, blk_e, blk_cnt, blk_src


def _expert_kernel(blk_e_ref, blk_cnt_ref, blk_src_ref, tok_ref, dst_ref,
                   h_hbm, w1_ref, w3_ref, w2_ref, y_hbm, xbuf, ybuf, gsem, ssem):
    del blk_e_ref
    b = pl.program_id(0)
    nb = pl.num_programs(0)
    slot = b % 2
    cnt = blk_cnt_ref[b]
    spr = w1_ref.shape[1] // LANES
    rows = xbuf.shape[1] // spr

    def for_rows(n, body):
        n_main = n // ISSUE_UNROLL

        def main(i, c):
            for u in range(ISSUE_UNROLL):
                body(i * ISSUE_UNROLL + u)
            return c

        def tail(r, c):
            body(r)
            return c

        lax.fori_loop(0, n_main, main, 0)
        lax.fori_loop(n_main * ISSUE_UNROLL, n, tail, 0)

    def gather_copy(src, r, sl):
        return pltpu.make_async_copy(h_hbm.at[pl.ds(pl.multiple_of(tok_ref[src + r], spr), spr)],
                                     xbuf.at[sl, pl.ds(pl.multiple_of(r * spr, spr), spr)], gsem.at[sl])

    def scatter_copy(src, r, sl):
        return pltpu.make_async_copy(ybuf.at[sl, pl.ds(pl.multiple_of(r * spr, spr), spr)],
                                     y_hbm.at[pl.ds(pl.multiple_of(dst_ref[src + r], spr), spr)], ssem.at[sl])

    def start_gathers(blk, sl):
        src = blk_src_ref[blk]
        for_rows(blk_cnt_ref[blk], lambda r: gather_copy(src, r, sl).start())

    def wait_gathers(blk, sl):
        src = blk_src_ref[blk]
        for_rows(blk_cnt_ref[blk], lambda r: gather_copy(src, r, sl).wait())

    def start_scatters(blk, sl):
        src = blk_src_ref[blk]
        for_rows(blk_cnt_ref[blk], lambda r: scatter_copy(src, r, sl).start())

    def wait_scatters(blk, sl):
        src = blk_src_ref[blk]
        for_rows(blk_cnt_ref[blk], lambda r: scatter_copy(src, r, sl).wait())

    @pl.when(b == 0)
    def _():
        xbuf[...] = jnp.zeros(xbuf.shape, xbuf.dtype)
        start_gathers(0, 0)

    @pl.when(b + 1 < nb)
    def _():
        start_gathers(b + 1, 1 - slot)

    @pl.when(b >= 2)
    def _():
        wait_scatters(b - 2, slot)

    @pl.when(cnt > 0)
    def _():
        wait_gathers(b, slot)
        x = jnp.concatenate([xbuf[slot, pl.ds(j, rows, stride=spr), :] for j in range(spr)], axis=1).astype(BF16)
        h1 = jnp.dot(x, w1_ref[0], preferred_element_type=F32)
        h3 = jnp.dot(x, w3_ref[0], preferred_element_type=F32)
        act = (h1 * _sigmoid(h1) * h3).astype(BF16)
        y = jnp.dot(act, w2_ref[0], preferred_element_type=F32)
        for j in range(spr):
            ybuf[slot, pl.ds(j, rows, stride=spr), :] = y[:, j * LANES:(j + 1) * LANES]
        start_scatters(b, slot)

    @pl.when(b == nb - 1)
    def _():
        @pl.when(b >= 1)
        def _():
            wait_scatters(b - 1, 1 - slot)
        wait_scatters(b, slot)


def _experts(h, w1, w3, w2, row_tok, row_dst, blk_e, blk_cnt, blk_src, rows):
    d, f = w1.shape[1], w1.shape[2]
    spr = d // LANES
    n_blk = blk_e.shape[0]
    wmap = lambda b, be, bc, bs, rt, rd: (be[b], 0, 0)
    grid_spec = pltpu.PrefetchScalarGridSpec(
        num_scalar_prefetch=5,
        grid=(n_blk,),
        in_specs=[
            pl.BlockSpec(memory_space=pl.ANY),
            pl.BlockSpec((1, d, f), wmap),
            pl.BlockSpec((1, d, f), wmap),
            pl.BlockSpec((1, f, d), wmap),
        ],
        out_specs=pl.BlockSpec(memory_space=pl.ANY),
        scratch_shapes=[
            pltpu.VMEM((2, rows * spr, LANES), F32),
            pltpu.VMEM((2, rows * spr, LANES), F32),
            pltpu.SemaphoreType.DMA((2,)),
            pltpu.SemaphoreType.DMA((2,)),
        ],
    )
    return pl.pallas_call(
        _expert_kernel,
        grid_spec=grid_spec,
        out_shape=jax.ShapeDtypeStruct((h.shape[0] * TOP_K, LANES), F32),
        compiler_params=_params(("arbitrary",), disable_bounds_checks=True),
        name="experts",
    )(blk_e, blk_cnt, blk_src, row_tok, row_dst, h, w1, w3, w2)


def _ple_kernel(x1_ref, y0_ref, y1_ref, w_ref, p_ref, gp_ref, wg_ref, wp_ref, gf_ref, o_ref, *, final):
    w = w_ref[...]
    tm, d = x1_ref.shape
    spr = d // LANES
    y0 = jnp.concatenate([y0_ref[pl.ds(j, tm, stride=spr), :] for j in range(spr)], axis=1)
    y1 = jnp.concatenate([y1_ref[pl.ds(j, tm, stride=spr), :] for j in range(spr)], axis=1)
    x2 = x1_ref[...] + w[:, 0:1] * y0 + w[:, 1:2] * y1
    hn = _rms(x2, gp_ref[...], 1e-6).astype(BF16)
    gate = _sigmoid(jnp.dot(hn, wg_ref[...], preferred_element_type=F32))
    pp = jnp.dot(p_ref[...].astype(BF16), wp_ref[...], preferred_element_type=F32)
    x3 = x2 + gate * pp
    o_ref[...] = _rms(x3, gf_ref[...], 1e-6) if final else x3


def _ple(x1, y2, w, p2d, g_ple, w_gate, w_proj, g_final, final, tm=256):
    n, d = x1.shape
    pd = p2d.shape[1]
    row = lambda i: (i, 0)
    fixed = lambda i: (0, 0)
    return pl.pallas_call(
        functools.partial(_ple_kernel, final=final),
        grid=(n // tm,),
        in_specs=[
            pl.BlockSpec((tm, d), row),
            pl.BlockSpec((tm * (d // LANES), LANES), row),
            pl.BlockSpec((tm * (d // LANES), LANES), lambda i: (n // tm + i, 0)),
            pl.BlockSpec((tm, TOP_K), row),
            pl.BlockSpec((tm, pd), row),
            pl.BlockSpec((1, d), fixed),
            pl.BlockSpec((d, d), fixed),
            pl.BlockSpec((pd, d), fixed),
            pl.BlockSpec((1, d), fixed),
        ],
        out_specs=pl.BlockSpec((tm, d), row),
        out_shape=jax.ShapeDtypeStruct((n, d), F32),
        compiler_params=_params(("parallel",)),
        name="ple_final",
    )(x1, y2, y2, w, p2d, g_ple.reshape(1, d).astype(F32), w_gate.astype(BF16), w_proj.astype(BF16),
      g_final.reshape(1, d).astype(F32))


def kernel(x, p, rel_bias, g_mix, w_in, lam_q1, lam_k1, lam_q2, lam_k2, subln_g, ssm_lam_re, ssm_lam_im, ssm_log_dt, ssm_b_re, ssm_b_im, ssm_c_re, ssm_c_im, ssm_d, w_glu, b_glu, ssm_norm_g, w_o, g_ffn, w_router_g, b_router_g, w_router_e, b_router_e, w1, w3, w2, g_ple, w_ple_gate, w_ple_proj, g_final):
    bt, s_len, d = x.shape
    n = bt * s_len
    depth = g_mix.shape[0]
    attn_w = N_HEADS * DV
    xc = x.reshape(n, d).astype(F32)
    for i in range(depth):
        lam_init = 0.8 - 0.6 * math.exp(-0.3 * i)
        col_scale = jnp.concatenate([jnp.full((attn_w,), LOG2E * DK ** -0.5, F32),
                                     jnp.ones((w_in.shape[2] - attn_w,), F32)])
        w_in_b = (w_in[i].astype(F32) * col_scale[None, :]).astype(BF16)
        z = _inproj(xc, g_mix[i].astype(F32), w_in_b)
        z3 = z.reshape(bt, s_len, z.shape[1])

        lam = (jnp.exp(jnp.sum(lam_q1[i].astype(F32) * lam_k1[i].astype(F32)))
               - jnp.exp(jnp.sum(lam_q2[i].astype(F32) * lam_k2[i].astype(F32))) + lam_init)
        a = _attention(z3, rel_bias, lam, subln_g[i], lam_init)

        ops = _ssm_operators(ssm_lam_re[i], ssm_lam_im[i], ssm_log_dt[i], ssm_b_re[i], ssm_b_im[i],
                             ssm_c_re[i], ssm_c_im[i], ssm_d[i])
        s = _ssm(z3[:, :, 3 * attn_w:], ops)

        w_r = jnp.zeros((d, ROUTER_LANES), F32)
        w_r = w_r.at[:, :N_EXPERT_GROUPS].set(w_router_g[i].astype(F32))
        w_r = w_r.at[:, N_EXPERT_GROUPS:N_EXPERT_GROUPS + N_EXPERTS].set(w_router_e[i].astype(F32))
        b_r = jnp.zeros((ROUTER_LANES,), F32)
        b_r = b_r.at[:N_EXPERT_GROUPS].set(b_router_g[i].astype(F32))
        b_r = b_r.at[N_EXPERT_GROUPS:N_EXPERT_GROUPS + N_EXPERTS].set(b_router_e[i].astype(F32))
        x1, h, logits = _postmix(a.reshape(n, attn_w), s.reshape(n, -1), xc, w_glu[i], b_glu[i],
                                 ssm_norm_g[i], w_o[i], g_ffn[i], w_r, b_r)

        gate_w, row_tok, row_dst, blk_e, blk_cnt, blk_src = _route(logits, EXPERT_ROWS, d // LANES)
        y2 = _experts(h, w1[i].astype(BF16), w3[i].astype(BF16), w2[i].astype(BF16),
                      row_tok, row_dst, blk_e, blk_cnt, blk_src, EXPERT_ROWS)
        xc = _ple(x1, y2, gate_w.astype(F32), p[i].reshape(n, -1).astype(F32),
                  g_ple[i], w_ple_gate[i], w_ple_proj[i], g_final, final=(i == depth - 1))
    return xc.reshape(bt, s_len, d)
```

```python
import functools
import math

import jax
import jax.numpy as jnp
from jax import lax
from jax.experimental import pallas as pl
from jax.experimental.pallas import tpu as pltpu

F32 = jnp.float32
BF16 = jnp.bfloat16
HIGHEST = lax.Precision.HIGHEST

N_HEADS = 8
DK = 64
DV = 128
N_BUCKETS = 32
MAX_DISTANCE = 128
SSM_GROUP = 16
SSM_GROUPS = 64
SSM_STATE = 64
N_EXPERT_GROUPS = 4
EXPERTS_PER_GROUP = 8
N_EXPERTS = 32
TOP_K = 2
MASK_VALUE = -1e30
LOG2E = math.log2(math.e)

SSM_CHUNK = 16
SSM_BUNDLE = 8
ATTN_TQ = 512
ONES_ROWS = 16
ATTN_CB = 256
ATTN_TK = 256
EXPERT_ROWS = 256
ISSUE_UNROLL = 8
LANES = 128
ROUTER_LANES = 128
VMEM_LIMIT = 56 << 20


def _params(semantics, **kw):
    return pltpu.CompilerParams(dimension_semantics=semantics, vmem_limit_bytes=VMEM_LIMIT, **kw)


def _rms(x, g, eps):
    return x * lax.rsqrt(jnp.mean(x * x, axis=-1, keepdims=True) + eps) * g


def _sigmoid(x):
    return 1.0 / (1.0 + jnp.exp(-x))


def _inproj_kernel(x_ref, g_ref, w_ref, z_ref, u_ref, h_scr):
    j = pl.program_id(1)
    last = pl.num_programs(1) - 1

    @pl.when(j == 0)
    def _():
        h_scr[...] = _rms(x_ref[...], g_ref[...], 1e-6).astype(BF16)

    acc = jnp.dot(h_scr[...], w_ref[...], preferred_element_type=F32)

    @pl.when(j < last)
    def _():
        z_ref[...] = acc.astype(z_ref.dtype)

    @pl.when(j == last)
    def _():
        u_ref[...] = acc


def _inproj(x2d, g, w, tm=512, tn=1024):
    n, d = x2d.shape
    pw = w.shape[1]
    n_col = pw // tn
    return pl.pallas_call(
        _inproj_kernel,
        grid=(n // tm, n_col),
        in_specs=[
            pl.BlockSpec((tm, d), lambda i, j: (i, 0)),
            pl.BlockSpec((1, d), lambda i, j: (0, 0)),
            pl.BlockSpec((d, tn), lambda i, j: (0, j)),
        ],
        out_specs=[pl.BlockSpec((tm, tn), lambda i, j: (i, jnp.minimum(j, n_col - 2))),
                   pl.BlockSpec((tm, tn), lambda i, j: (i, 0))],
        out_shape=[jax.ShapeDtypeStruct((n, pw - tn), BF16), jax.ShapeDtypeStruct((n, tn), F32)],
        scratch_shapes=[pltpu.VMEM((tm, d), BF16)],
        compiler_params=_params(("parallel", "arbitrary")),
        name="inproj",
    )(x2d, g.reshape(1, d), w)


def _attn_kernel(lam_ref, q_ref, k_ref, v_ref, bias_ref, g_ref, o_ref, qs, vt, *state,
                 tq, tk, cb, out_scale):
    n_blk = 2 * tq // cb
    m_scr, acc, s_scr = (state[i * n_blk:(i + 1) * n_blk] for i in range(3))
    qi = pl.program_id(2)
    n_sub = tq // tk
    n_kv = v_ref.shape[1] // tk

    @pl.when(qi == 0)
    def _():
        for c in range(n_kv):
            vt[c, 0:DV, :] = v_ref[0, c * tk:(c + 1) * tk, :].astype(F32).T.astype(BF16)
            vt[c, DV:DV + ONES_ROWS, :] = jnp.ones((ONES_ROWS, tk), BF16)

    q = q_ref[0]
    lane = lax.broadcasted_iota(jnp.int32, q.shape, 1)
    zero = jnp.zeros_like(q)
    qs[0:tq, :] = jnp.where(lane < DK, q, zero)
    qs[tq:2 * tq, :] = jnp.where(lane >= DK, q, zero)
    for c in range(n_blk):
        m_scr[c][...] = jnp.full(m_scr[c].shape, MASK_VALUE, F32)
        acc[c][...] = jnp.zeros(acc[c].shape, F32)

    def scores(j):
        kj = k_ref[0, pl.ds(pl.multiple_of(j * tk, tk), tk), :]
        return [lax.dot_general(kj, qs[c * cb:(c + 1) * cb, :], (((1,), (1,)), ((), ())),
                                preferred_element_type=F32) for c in range(n_blk)]

    def consume(j, bias, prefetch):
        nxt = scores(j + 1) if prefetch else None
        vtj = vt[j]
        alphas, ps = [], []
        for c in range(n_blk):
            s = s_scr[c][...]
            if bias is not None:
                qb = (c * cb) % tq
                s = s + bias[:, qb:qb + cb]
            m_old = m_scr[c][...]
            m_new = jnp.maximum(m_old, jnp.max(s, axis=0, keepdims=True))
            alpha = jnp.exp2(m_old - m_new)
            p = jnp.exp2(s - m_new)
            m_scr[c][...] = m_new
            alphas.append(alpha)
            ps.append(p.astype(BF16))
        pvs = [jnp.dot(vtj, ps[c], preferred_element_type=F32) for c in range(n_blk)]
        for c in range(n_blk):
            acc[c][...] = alphas[c] * acc[c][...] + pvs[c]
        if prefetch:
            for c in range(n_blk):
                s_scr[c][...] = nxt[c]

    first = qi * n_sub - 1
    for c, sc in enumerate(scores(0)):
        s_scr[c][...] = sc

    def far(j, carry):
        consume(j, None, True)
        return carry

    lax.fori_loop(0, jnp.maximum(first, 0), far, 0)

    @pl.when(qi >= 1)
    def _():
        consume(first, bias_ref[0, 0], True)

    for r in range(1, n_sub + 1):
        consume(first + r, bias_ref[0, r], r < n_sub)

    lam = lam_ref[0, 0]
    half = n_blk // 2
    for b in range(half):
        a1, a2 = acc[b], acc[half + b]
        ot = (a1[0:DV, :] / a1[DV:DV + 1, :] - lam * (a2[0:DV, :] / a2[DV:DV + 1, :]))
        ot = ot * lax.rsqrt(jnp.mean(ot * ot, axis=0, keepdims=True) + 1e-5)
        o_ref[0, b * cb:(b + 1) * cb, :] = (ot.T * (g_ref[...] * out_scale)).astype(o_ref.dtype)


def _t5_bucket(n):
    n = jnp.maximum(n, 0)
    max_exact = N_BUCKETS // 2
    nf = jnp.maximum(n, 1).astype(F32)
    large = max_exact + (jnp.log(nf / max_exact) / math.log(MAX_DISTANCE / max_exact)
                         * (N_BUCKETS - max_exact)).astype(jnp.int32)
    large = jnp.minimum(large, N_BUCKETS - 1)
    return jnp.where(n < max_exact, n, large)


def _attn_bias_tiles(rel_bias, tq, tk):
    assert tk >= MAX_DISTANCE and tq % tk == 0
    table = rel_bias.astype(F32)
    rel_table = (table - table[N_BUCKETS - 1][None, :]) * LOG2E
    r = jnp.arange(tk, dtype=jnp.int32)[:, None]
    c = jnp.arange(tq, dtype=jnp.int32)[None, :]
    tiles = []
    for t in range(tq // tk + 1):
        dist = c - (r + (t - 1) * tk)
        onehot = (_t5_bucket(dist)[:, :, None] == jnp.arange(N_BUCKETS, dtype=jnp.int32)).astype(F32)
        b = jnp.einsum('rcn,nh->hrc', onehot, rel_table, precision=HIGHEST)
        tiles.append(jnp.where((dist >= 0)[None], b, MASK_VALUE))
    return jnp.stack(tiles, axis=1)


def _attention(z3, rel_bias, lam, subln_g, lam_init, tq=ATTN_TQ, tk=ATTN_TK, cb=ATTN_CB):
    bt, s_len, _ = z3.shape
    bias = _attn_bias_tiles(rel_bias, tq, tk)
    n_sp = bias.shape[1]
    kern = functools.partial(_attn_kernel, tq=tq, tk=tk, cb=cb, out_scale=1.0 - lam_init)
    n_blk = 2 * tq // cb
    return pl.pallas_call(
        kern,
        grid=(bt, N_HEADS, s_len // tq),
        in_specs=[
            pl.BlockSpec(memory_space=pltpu.SMEM),
            pl.BlockSpec((1, tq, 2 * DK), lambda b, h, i: (b, i, h)),
            pl.BlockSpec((1, s_len, 2 * DK), lambda b, h, i: (b, 0, N_HEADS + h)),
            pl.BlockSpec((1, s_len, DV), lambda b, h, i: (b, 0, 2 * N_HEADS + h)),
            pl.BlockSpec((1, n_sp, tk, tq), lambda b, h, i: (h, 0, 0, 0)),
            pl.BlockSpec((1, DV), lambda b, h, i: (0, 0)),
        ],
        out_specs=pl.BlockSpec((1, tq, DV), lambda b, h, i: (b, i, h)),
        out_shape=jax.ShapeDtypeStruct((bt, s_len, N_HEADS * DV), BF16),
        scratch_shapes=[
            pltpu.VMEM((2 * tq, 2 * DK), BF16),
            pltpu.VMEM((s_len // tk, DV + ONES_ROWS, tk), BF16),
        ] + [pltpu.VMEM((1, cb), F32)] * n_blk + [pltpu.VMEM((DV + ONES_ROWS, cb), F32)] * n_blk
        + [pltpu.VMEM((tk, cb), F32)] * n_blk,
        compiler_params=_params(("parallel", "parallel", "arbitrary")),
        name="diff_attention",
    )(lam.reshape(1, 1).astype(F32), z3, z3, z3, bias, subln_g.reshape(1, DV).astype(F32))


def _ssm_operators(lam_re, lam_im, log_dt, b_re, b_im, c_re, c_im, d_skip):
    L, H, P = SSM_CHUNK, SSM_GROUP, SSM_STATE
    lre = lam_re.astype(F32)
    lim = lam_im.astype(F32)
    dt = jnp.exp(log_dt.astype(F32))[:, None]
    mag = jnp.exp(lre * dt)
    ab_re = mag * jnp.cos(lim * dt)
    ab_im = mag * jnp.sin(lim * dt)
    den = lre * lre + lim * lim
    nr, ni = ab_re - 1.0, ab_im
    cr = ((nr * lre + ni * lim) / den)[..., None]
    ci = ((ni * lre - nr * lim) / den)[..., None]
    bre = b_re.astype(F32)
    bim = b_im.astype(F32)
    bb_re = cr * bre - ci * bim
    bb_im = cr * bim + ci * bre
    cre = c_re.astype(F32)
    cim = c_im.astype(F32)

    tau = jnp.arange(L + 1, dtype=F32)[:, None, None]
    pw_mag = jnp.exp(tau * (lre * dt)[None])
    pw_re = pw_mag * jnp.cos(tau * (lim * dt)[None])
    pw_im = pw_mag * jnp.sin(tau * (lim * dt)[None])

    ca_re = cre[None] * pw_re[:, :, None, :] - cim[None] * pw_im[:, :, None, :]
    ca_im = cre[None] * pw_im[:, :, None, :] + cim[None] * pw_re[:, :, None, :]
    k_tau = (jnp.einsum('tghp,gpk->tghk', ca_re[:L], bb_re, precision=HIGHEST)
             - jnp.einsum('tghp,gpk->tghk', ca_im[:L], bb_im, precision=HIGHEST))
    ti = jnp.arange(L)
    nb = SSM_GROUPS // SSM_BUNDLE
    eye = jnp.eye(SSM_BUNDLE, dtype=F32)[None, None, :, None, :, None]

    def bundle_diag(src):
        x = src.shape[-1]
        blk = src.reshape(L, nb, SSM_BUNDLE, H, 1, x) * eye
        return blk.transpose(1, 0, 2, 3, 4, 5).reshape(nb, L, SSM_BUNDLE * H, SSM_BUNDLE * x)

    t_blk = bundle_diag(k_tau.transpose(0, 1, 3, 2))

    rev_re = pw_re[L - 1 - ti]
    rev_im = pw_im[L - 1 - ti]
    w_re = rev_re[..., None] * bb_re[None] - rev_im[..., None] * bb_im[None]
    w_im = rev_re[..., None] * bb_im[None] + rev_im[..., None] * bb_re[None]
    w_src = jnp.concatenate([w_re, w_im], axis=2).transpose(0, 1, 3, 2)
    w_op = bundle_diag(w_src).reshape(nb, L * SSM_BUNDLE * H, SSM_BUNDLE * 2 * P)

    e_src = jnp.concatenate([ca_re[1:L + 1], -ca_im[1:L + 1]], axis=3)
    et_op = bundle_diag(e_src).reshape(nb, L * SSM_BUNDLE * H, SSM_BUNDLE * 2 * P)

    a1 = jnp.concatenate([pw_re[L], pw_re[L]], axis=-1)
    a2 = jnp.concatenate([-pw_im[L], pw_im[L]], axis=-1)
    d_tile = jnp.tile(d_skip.astype(F32).reshape(nb, 1, SSM_BUNDLE * H), (1, 1, L))
    return t_blk.astype(BF16), w_op.astype(BF16), et_op.astype(BF16), a1, a2, d_tile


def _chunk_rows(u_ref, n_chunk):
    return jnp.concatenate([u_ref[pl.ds(t, n_chunk, stride=SSM_CHUNK), :] for t in range(SSM_CHUNK)], axis=1)


def _ssm_in_kernel(u_ref, w_ref, v_ref):
    x = _chunk_rows(u_ref, v_ref.shape[0]).astype(BF16)
    v_ref[...] = jnp.dot(x, w_ref[0], preferred_element_type=F32)


def _ssm_scan_kernel(v_ref, a1_ref, a2_ref, o_ref, st):
    @pl.when(pl.program_id(0) == 0)
    def _():
        st[...] = jnp.zeros(st.shape, F32)

    a1 = a1_ref[...][None]
    a2 = a2_ref[...][None]
    n_chunk = v_ref.shape[1]
    half = v_ref.shape[3] // 2

    def body(c, s):
        o_ref[:, pl.ds(c, 1)] = s[:, None].astype(o_ref.dtype)
        v = v_ref[:, pl.ds(c, 1)][:, 0]
        return a1 * s + a2 * pltpu.roll(s, half, axis=2) + v

    st[...] = lax.fori_loop(0, n_chunk, body, st[...])


def _gelu_tanh(x):
    c = math.sqrt(2.0 / math.pi)
    return x * (0.5 * (1.0 + jnp.tanh(c * (x + 0.044715 * (x * x * x)))))


def _ssm_out_kernel(u_ref, tb_ref, s_ref, et_ref, d_ref, y_ref, t_scr):
    L = SSM_CHUNK
    lw = tb_ref.shape[2]

    @pl.when(pl.program_id(1) == 0)
    def _():
        t_scr[...] = jnp.zeros(t_scr.shape, t_scr.dtype)
        for i in range(L):
            for j in range(i, L):
                t_scr[i * lw:(i + 1) * lw, j * lw:(j + 1) * lw] = tb_ref[0, j - i]

    n_chunk = s_ref.shape[0]
    xf = _chunk_rows(u_ref, n_chunk)
    y = jnp.dot(xf.astype(BF16), t_scr[...], preferred_element_type=F32)
    y = y + lax.dot_general(s_ref[...], et_ref[0], (((1,), (1,)), ((), ())), preferred_element_type=F32)
    y = _gelu_tanh(y + d_ref[0] * xf)
    for t in range(L):
        y_ref[pl.ds(t, n_chunk, stride=L), :] = y[:, t * lw:(t + 1) * lw]


def _ssm(u, ops, bt, scan_block=32, chunk_tile=256):
    t_blk, w_op, et_op, a1, a2, d_tile = ops
    n, width = u.shape
    L, G, P2 = SSM_CHUNK, SSM_GROUPS, 2 * SSM_STATE
    nb = t_blk.shape[0]
    lw = width // nb
    sw = SSM_BUNDLE * P2
    nc = n // L
    n_c = nc // bt
    ct = min(chunk_tile, nc)

    v = pl.pallas_call(
        _ssm_in_kernel,
        grid=(nb, nc // ct),
        in_specs=[pl.BlockSpec((ct * L, lw), lambda b, i: (i, b)),
                  pl.BlockSpec((1, L * lw, sw), lambda b, i: (b, 0, 0))],
        out_specs=pl.BlockSpec((ct, sw), lambda b, i: (i, b)),
        out_shape=jax.ShapeDtypeStruct((nc, G * P2), F32),
        compiler_params=_params(("parallel", "parallel")),
        name="ssm_chunk_state",
    )(u, w_op)

    cb = min(scan_block, n_c)
    s_prev = pl.pallas_call(
        _ssm_scan_kernel,
        grid=(n_c // cb,),
        in_specs=[pl.BlockSpec((bt, cb, G, P2), lambda c: (0, c, 0, 0)),
                  pl.BlockSpec((G, P2), lambda c: (0, 0)),
                  pl.BlockSpec((G, P2), lambda c: (0, 0))],
        out_specs=pl.BlockSpec((bt, cb, G, P2), lambda c: (0, c, 0, 0)),
        out_shape=jax.ShapeDtypeStruct((bt, n_c, G, P2), BF16),
        scratch_shapes=[pltpu.VMEM((bt, G, P2), F32)],
        compiler_params=_params(("arbitrary",)),
        name="ssm_scan",
    )(v.reshape(bt, n_c, G, P2), a1, a2)

    return pl.pallas_call(
        _ssm_out_kernel,
        grid=(nb, nc // ct),
        in_specs=[pl.BlockSpec((ct * L, lw), lambda b, i: (i, b)),
                  pl.BlockSpec((1, L, lw, lw), lambda b, i: (b, 0, 0, 0)),
                  pl.BlockSpec((ct, sw), lambda b, i: (i, b)),
                  pl.BlockSpec((1, L * lw, sw), lambda b, i: (b, 0, 0)),
                  pl.BlockSpec((1, 1, L * lw), lambda b, i: (b, 0, 0))],
        out_specs=pl.BlockSpec((ct * L, lw), lambda b, i: (i, b)),
        out_shape=jax.ShapeDtypeStruct((n, width), F32),
        scratch_shapes=[pltpu.VMEM((L * lw, L * lw), BF16)],
        compiler_params=_params(("parallel", "arbitrary")),
        name="ssm_output",
    )(u, t_blk, s_prev.reshape(nc, G * P2), et_op, d_tile)


def _postmix_kernel(a_ref, s_ref, x_ref, wg_ref, bg_ref, gs_ref, woa_ref, wos_ref, gf_ref,
                    wrh_ref, wrl_ref, br_ref, x1_ref, h_ref, lg_ref):
    sf = s_ref[...]
    gate = _sigmoid(jnp.dot(sf.astype(BF16), wg_ref[...], preferred_element_type=F32) + bg_ref[...])
    sn = _rms(sf * gate, gs_ref[...], 1e-6).astype(BF16)
    x1 = (x_ref[...]
          + jnp.dot(a_ref[...], woa_ref[...], preferred_element_type=F32)
          + jnp.dot(sn, wos_ref[...], preferred_element_type=F32))
    x1_ref[...] = x1
    h = _rms(x1, gf_ref[...], 1e-6)
    h_hi = h.astype(BF16)
    h_lo = (h - h_hi.astype(F32)).astype(BF16)
    hf = h_hi.astype(F32)
    for j in range(hf.shape[1] // LANES):
        h_ref[pl.ds(j, hf.shape[0], stride=hf.shape[1] // LANES), :] = hf[:, j * LANES:(j + 1) * LANES]
    lg_ref[...] = (jnp.dot(h_hi, wrh_ref[...], preferred_element_type=F32)
                   + jnp.dot(h_lo, wrh_ref[...], preferred_element_type=F32)
                   + jnp.dot(h_hi, wrl_ref[...], preferred_element_type=F32)
                   + br_ref[...])


def _postmix(a, s, x2d, w_glu, b_glu, g_s, w_o, g_ffn, w_r, b_r, tm=256):
    n, d = x2d.shape
    wa = a.shape[1]
    ws = s.shape[1]
    wr_hi = w_r.astype(BF16)
    wr_lo = (w_r - wr_hi.astype(F32)).astype(BF16)
    row = lambda i: (i, 0)
    fixed = lambda i: (0, 0)
    return pl.pallas_call(
        _postmix_kernel,
        grid=(n // tm,),
        in_specs=[
            pl.BlockSpec((tm, wa), row),
            pl.BlockSpec((tm, ws), row),
            pl.BlockSpec((tm, d), row),
            pl.BlockSpec((ws, ws), fixed),
            pl.BlockSpec((1, ws), fixed),
            pl.BlockSpec((1, ws), fixed),
            pl.BlockSpec((wa, d), fixed),
            pl.BlockSpec((ws, d), fixed),
            pl.BlockSpec((1, d), fixed),
            pl.BlockSpec((d, ROUTER_LANES), fixed),
            pl.BlockSpec((d, ROUTER_LANES), fixed),
            pl.BlockSpec((1, ROUTER_LANES), fixed),
        ],
        out_specs=[pl.BlockSpec((tm, d), row), pl.BlockSpec((tm * (d // LANES), LANES), row),
                   pl.BlockSpec((tm, ROUTER_LANES), row)],
        out_shape=[jax.ShapeDtypeStruct((n, d), F32), jax.ShapeDtypeStruct((n * (d // LANES), LANES), F32),
                   jax.ShapeDtypeStruct((n, ROUTER_LANES), F32)],
        compiler_params=_params(("parallel",)),
        name="postmix",
    )(a, s, x2d, w_glu.astype(BF16), b_glu.reshape(1, ws).astype(F32), g_s.reshape(1, ws).astype(F32),
      w_o[:wa].astype(BF16), w_o[wa:].astype(BF16), g_ffn.reshape(1, d).astype(F32),
      wr_hi, wr_lo, b_r.reshape(1, ROUTER_LANES).astype(F32))


def _route(logits, rows, slab_rows):
    n_tok = logits.shape[0]
    lg = logits[:, :N_EXPERT_GROUPS]
    le = logits[:, N_EXPERT_GROUPS:N_EXPERT_GROUPS + N_EXPERTS].reshape(
        n_tok, N_EXPERT_GROUPS, EXPERTS_PER_GROUP)
    pg = jax.nn.softmax(lg, axis=-1)
    gsel = jnp.argmax(lg, axis=-1).astype(jnp.int32)
    gate_g = jnp.max(pg, axis=-1, keepdims=True)
    sel = gsel[:, None] == jnp.arange(N_EXPERT_GROUPS, dtype=jnp.int32)[None, :]
    le_sel = jnp.sum(jnp.where(sel[:, :, None], le, 0.0), axis=1)
    pe = jax.nn.softmax(le_sel, axis=-1)
    top_p, top_i = lax.top_k(pe, TOP_K)
    w = gate_g * top_p / jnp.sum(top_p, axis=-1, keepdims=True)
    eid = gsel[:, None] * EXPERTS_PER_GROUP + top_i.astype(jnp.int32)

    n_assign = n_tok * TOP_K
    flat_e = eid.reshape(-1)
    counts = jnp.sum((flat_e[:, None] == jnp.arange(N_EXPERTS, dtype=jnp.int32)[None, :])
                     .astype(jnp.int32), axis=0)
    n_blk_e = (counts + rows - 1) // rows
    blk_end = jnp.cumsum(n_blk_e)
    blk_first = blk_end - n_blk_e
    start = jnp.cumsum(counts) - counts
    order = jnp.argsort(flat_e, stable=True).astype(jnp.int32)
    n_blk = (n_assign + rows - 1) // rows + N_EXPERTS
    blk = jnp.arange(n_blk, dtype=jnp.int32)
    blk_e = jnp.minimum(jnp.sum((blk[:, None] >= blk_end[None, :]).astype(jnp.int32), axis=1),
                        N_EXPERTS - 1)
    used = blk < blk_end[-1]
    in_e = (blk - blk_first[blk_e]) * rows
    blk_cnt = jnp.where(used, jnp.clip(counts[blk_e] - in_e, 0, rows), 0).astype(jnp.int32)
    blk_src = (start[blk_e] + in_e).astype(jnp.int32)
    last_e = blk_e[jnp.maximum(blk_end[-1] - 1, 0)]
    blk_e = jnp.where(used, blk_e, last_e)
    row_tok = order // TOP_K
    row_dst = (order % TOP_K) * n_tok + row_tok
    return w, row_tok * slab_rows, row_dst * slab_rows, blk_e, blk_cnt, blk_src


def _expert_kernel(blk_e_ref, blk_cnt_ref, blk_src_ref, tok_ref, dst_ref,
                   h_hbm, w1_ref, w3_ref, w2_ref, y_hbm, xbuf, ybuf, gsem, ssem):
    del blk_e_ref
    b = pl.program_id(0)
    nb = pl.num_programs(0)
    slot = b % 2
    cnt = blk_cnt_ref[b]
    spr = w1_ref.shape[1] // LANES
    rows = xbuf.shape[1] // spr

    def for_rows(n, body):
        n_main = n // ISSUE_UNROLL

        def main(i, c):
            for u in range(ISSUE_UNROLL):
                body(i * ISSUE_UNROLL + u)
            return c

        def tail(r, c):
            body(r)
            return c

        lax.fori_loop(0, n_main, main, 0)
        lax.fori_loop(n_main * ISSUE_UNROLL, n, tail, 0)

    def gather_copy(src, r, sl):
        return pltpu.make_async_copy(h_hbm.at[pl.ds(pl.multiple_of(tok_ref[src + r], spr), spr)],
                                     xbuf.at[sl, pl.ds(pl.multiple_of(r * spr, spr), spr)], gsem.at[sl])

    def scatter_copy(src, r, sl):
        return pltpu.make_async_copy(ybuf.at[sl, pl.ds(pl.multiple_of(r * spr, spr), spr)],
                                     y_hbm.at[pl.ds(pl.multiple_of(dst_ref[src + r], spr), spr)], ssem.at[sl])

    def start_gathers(blk, sl):
        src = blk_src_ref[blk]
        for_rows(blk_cnt_ref[blk], lambda r: gather_copy(src, r, sl).start())

    def wait_gathers(blk, sl):
        src = blk_src_ref[blk]
        for_rows(blk_cnt_ref[blk], lambda r: gather_copy(src, r, sl).wait())

    def start_scatters(blk, sl):
        src = blk_src_ref[blk]
        for_rows(blk_cnt_ref[blk], lambda r: scatter_copy(src, r, sl).start())

    def wait_scatters(blk, sl):
        src = blk_src_ref[blk]
        for_rows(blk_cnt_ref[blk], lambda r: scatter_copy(src, r, sl).wait())

    @pl.when(b == 0)
    def _():
        xbuf[...] = jnp.zeros(xbuf.shape, xbuf.dtype)
        start_gathers(0, 0)

    @pl.when(b + 1 < nb)
    def _():
        start_gathers(b + 1, 1 - slot)

    @pl.when(b >= 2)
    def _():
        wait_scatters(b - 2, slot)

    @pl.when(cnt > 0)
    def _():
        wait_gathers(b, slot)
        x = jnp.concatenate([xbuf[slot, pl.ds(j, rows, stride=spr), :] for j in range(spr)], axis=1).astype(BF16)
        h1 = jnp.dot(x, w1_ref[0], preferred_element_type=F32)
        h3 = jnp.dot(x, w3_ref[0], preferred_element_type=F32)
        act = (h1 * _sigmoid(h1) * h3).astype(BF16)
        y = jnp.dot(act, w2_ref[0], preferred_element_type=F32)
        for j in range(spr):
            ybuf[slot, pl.ds(j, rows, stride=spr), :] = y[:, j * LANES:(j + 1) * LANES]
        start_scatters(b, slot)

    @pl.when(b == nb - 1)
    def _():
        @pl.when(b >= 1)
        def _():
            wait_scatters(b - 1, 1 - slot)
        wait_scatters(b, slot)


def _experts(h, w1, w3, w2, row_tok, row_dst, blk_e, blk_cnt, blk_src, rows):
    d, f = w1.shape[1], w1.shape[2]
    spr = d // LANES
    n_blk = blk_e.shape[0]
    wmap = lambda b, be, bc, bs, rt, rd: (be[b], 0, 0)
    grid_spec = pltpu.PrefetchScalarGridSpec(
        num_scalar_prefetch=5,
        grid=(n_blk,),
        in_specs=[
            pl.BlockSpec(memory_space=pl.ANY),
            pl.BlockSpec((1, d, f), wmap),
            pl.BlockSpec((1, d, f), wmap),
            pl.BlockSpec((1, f, d), wmap),
        ],
        out_specs=pl.BlockSpec(memory_space=pl.ANY),
        scratch_shapes=[
            pltpu.VMEM((2, rows * spr, LANES), F32),
            pltpu.VMEM((2, rows * spr, LANES), F32),
            pltpu.SemaphoreType.DMA((2,)),
            pltpu.SemaphoreType.DMA((2,)),
        ],
    )
    return pl.pallas_call(
        _expert_kernel,
        grid_spec=grid_spec,
        out_shape=jax.ShapeDtypeStruct((h.shape[0] * TOP_K, LANES), F32),
        compiler_params=_params(("arbitrary",), disable_bounds_checks=True),
        name="experts",
    )(blk_e, blk_cnt, blk_src, row_tok, row_dst, h, w1, w3, w2)


def _ple_kernel(x1_ref, y0_ref, y1_ref, w_ref, p_ref, gp_ref, wg_ref, wp_ref, gf_ref, o_ref, *, final):
    w = w_ref[...]
    tm, d = x1_ref.shape
    spr = d // LANES
    y0 = jnp.concatenate([y0_ref[pl.ds(j, tm, stride=spr), :] for j in range(spr)], axis=1)
    y1 = jnp.concatenate([y1_ref[pl.ds(j, tm, stride=spr), :] for j in range(spr)], axis=1)
    x2 = x1_ref[...] + w[:, 0:1] * y0 + w[:, 1:2] * y1
    hn = _rms(x2, gp_ref[...], 1e-6).astype(BF16)
    gate = _sigmoid(jnp.dot(hn, wg_ref[...], preferred_element_type=F32))
    pp = jnp.dot(p_ref[...].astype(BF16), wp_ref[...], preferred_element_type=F32)
    x3 = x2 + gate * pp
    o_ref[...] = _rms(x3, gf_ref[...], 1e-6) if final else x3


def _ple(x1, y2, w, p2d, g_ple, w_gate, w_proj, g_final, final, tm=256):
    n, d = x1.shape
    pd = p2d.shape[1]
    row = lambda i: (i, 0)
    fixed = lambda i: (0, 0)
    return pl.pallas_call(
        functools.partial(_ple_kernel, final=final),
        grid=(n // tm,),
        in_specs=[
            pl.BlockSpec((tm, d), row),
            pl.BlockSpec((tm * (d // LANES), LANES), row),
            pl.BlockSpec((tm * (d // LANES), LANES), lambda i: (n // tm + i, 0)),
            pl.BlockSpec((tm, TOP_K), row),
            pl.BlockSpec((tm, pd), row),
            pl.BlockSpec((1, d), fixed),
            pl.BlockSpec((d, d), fixed),
            pl.BlockSpec((pd, d), fixed),
            pl.BlockSpec((1, d), fixed),
        ],
        out_specs=pl.BlockSpec((tm, d), row),
        out_shape=jax.ShapeDtypeStruct((n, d), F32),
        compiler_params=_params(("parallel",)),
        name="ple_final",
    )(x1, y2, y2, w, p2d, g_ple.reshape(1, d).astype(F32), w_gate.astype(BF16), w_proj.astype(BF16),
      g_final.reshape(1, d).astype(F32))


def kernel(x, p, rel_bias, g_mix, w_in, lam_q1, lam_k1, lam_q2, lam_k2, subln_g, ssm_lam_re, ssm_lam_im, ssm_log_dt, ssm_b_re, ssm_b_im, ssm_c_re, ssm_c_im, ssm_d, w_glu, b_glu, ssm_norm_g, w_o, g_ffn, w_router_g, b_router_g, w_router_e, b_router_e, w1, w3, w2, g_ple, w_ple_gate, w_ple_proj, g_final):
    bt, s_len, d = x.shape
    n = bt * s_len
    depth = g_mix.shape[0]
    attn_w = N_HEADS * DV
    xc = x.reshape(n, d).astype(F32)
    for i in range(depth):
        lam_init = 0.8 - 0.6 * math.exp(-0.3 * i)
        col_scale = jnp.concatenate([jnp.full((attn_w,), LOG2E * DK ** -0.5, F32),
                                     jnp.ones((w_in.shape[2] - attn_w,), F32)])
        w_in_b = (w_in[i].astype(F32) * col_scale[None, :]).astype(BF16)
        z, u = _inproj(xc, g_mix[i].astype(F32), w_in_b)
        z3 = z.reshape(bt, s_len, z.shape[1])

        lam = (jnp.exp(jnp.sum(lam_q1[i].astype(F32) * lam_k1[i].astype(F32)))
               - jnp.exp(jnp.sum(lam_q2[i].astype(F32) * lam_k2[i].astype(F32))) + lam_init)
        a = _attention(z3, rel_bias, lam, subln_g[i], lam_init)

        ops = _ssm_operators(ssm_lam_re[i], ssm_lam_im[i], ssm_log_dt[i], ssm_b_re[i], ssm_b_im[i],
                             ssm_c_re[i], ssm_c_im[i], ssm_d[i])
        s = _ssm(u, ops, bt)

        w_r = jnp.zeros((d, ROUTER_LANES), F32)
        w_r = w_r.at[:, :N_EXPERT_GROUPS].set(w_router_g[i].astype(F32))
        w_r = w_r.at[:, N_EXPERT_GROUPS:N_EXPERT_GROUPS + N_EXPERTS].set(w_router_e[i].astype(F32))
        b_r = jnp.zeros((ROUTER_LANES,), F32)
        b_r = b_r.at[:N_EXPERT_GROUPS].set(b_router_g[i].astype(F32))
        b_r = b_r.at[N_EXPERT_GROUPS:N_EXPERT_GROUPS + N_EXPERTS].set(b_router_e[i].astype(F32))
        x1, h, logits = _postmix(a.reshape(n, attn_w), s, xc, w_glu[i], b_glu[i],
                                 ssm_norm_g[i], w_o[i], g_ffn[i], w_r, b_r)

        gate_w, row_tok, row_dst, blk_e, blk_cnt, blk_src = _route(logits, EXPERT_ROWS, d // LANES)
        y2 = _experts(h, w1[i].astype(BF16), w3[i].astype(BF16), w2[i].astype(BF16),
                      row_tok, row_dst, blk_e, blk_cnt, blk_src, EXPERT_ROWS)
        xc = _ple(x1, y2, gate_w.astype(F32), p[i].reshape(n, -1).astype(F32),
                  g_ple[i], w_ple_gate[i], w_ple_proj[i], g_final, final=(i == depth - 1))
    return xc.reshape(bt, s_len, d)
```

```python
import functools
import math

import jax
import jax.numpy as jnp
from jax import lax
from jax.experimental import pallas as pl
from jax.experimental.pallas import tpu as pltpu

F32 = jnp.float32
BF16 = jnp.bfloat16
HIGHEST = lax.Precision.HIGHEST

N_HEADS = 8
DK = 64
DV = 128
N_BUCKETS = 32
MAX_DISTANCE = 128
SSM_GROUP = 16
SSM_GROUPS = 64
SSM_STATE = 64
N_EXPERT_GROUPS = 4
EXPERTS_PER_GROUP = 8
N_EXPERTS = 32
TOP_K = 2
MASK_VALUE = -1e30
LOG2E = math.log2(math.e)

SSM_CHUNK = 16
SSM_BUNDLE = 8
ATTN_TQ = 512
ONES_ROWS = 16
ATTN_CB = 256
ATTN_TK = 256
EXPERT_ROWS = 256
ISSUE_UNROLL = 8
LANES = 128
ROUTER_LANES = 128
VMEM_LIMIT = 56 << 20


def _params(semantics, **kw):
    return pltpu.CompilerParams(dimension_semantics=semantics, vmem_limit_bytes=VMEM_LIMIT, **kw)


def _rms(x, g, eps):
    return x * lax.rsqrt(jnp.mean(x * x, axis=-1, keepdims=True) + eps) * g


def _sigmoid(x):
    return 1.0 / (1.0 + jnp.exp(-x))


def _inproj_kernel(x_ref, g_ref, w_ref, z_ref, u_ref, h_scr):
    j = pl.program_id(1)
    last = pl.num_programs(1) - 1

    @pl.when(j == 0)
    def _():
        h_scr[...] = _rms(x_ref[...], g_ref[...], 1e-6).astype(BF16)

    acc = jnp.dot(h_scr[...], w_ref[...], preferred_element_type=F32)

    @pl.when(j < last)
    def _():
        z_ref[...] = acc.astype(z_ref.dtype)

    @pl.when(j == last)
    def _():
        u_ref[...] = acc


def _inproj(x2d, g, w, tm=512, tn=1024):
    n, d = x2d.shape
    pw = w.shape[1]
    n_col = pw // tn
    return pl.pallas_call(
        _inproj_kernel,
        grid=(n // tm, n_col),
        in_specs=[
            pl.BlockSpec((tm, d), lambda i, j: (i, 0)),
            pl.BlockSpec((1, d), lambda i, j: (0, 0)),
            pl.BlockSpec((d, tn), lambda i, j: (0, j)),
        ],
        out_specs=[pl.BlockSpec((tm, tn), lambda i, j: (i, jnp.minimum(j, n_col - 2))),
                   pl.BlockSpec((tm, tn), lambda i, j: (i, 0))],
        out_shape=[jax.ShapeDtypeStruct((n, pw - tn), BF16), jax.ShapeDtypeStruct((n, tn), F32)],
        scratch_shapes=[pltpu.VMEM((tm, d), BF16)],
        compiler_params=_params(("parallel", "arbitrary")),
        name="inproj",
    )(x2d, g.reshape(1, d), w)


def _attn_kernel(lam_ref, q_ref, k_ref, v_ref, bias_ref, g_ref, o_ref, qs, vt, *state,
                 tq, tk, cb, out_scale):
    n_blk = 2 * tq // cb
    m_scr, acc, s_scr = (state[i * n_blk:(i + 1) * n_blk] for i in range(3))
    qi = pl.program_id(2)
    n_sub = tq // tk
    n_kv = v_ref.shape[1] // tk

    @pl.when(qi == 0)
    def _():
        for c in range(n_kv):
            vt[c, 0:DV, :] = v_ref[0, c * tk:(c + 1) * tk, :].astype(F32).T.astype(BF16)
            vt[c, DV:DV + ONES_ROWS, :] = jnp.ones((ONES_ROWS, tk), BF16)

    q = q_ref[0]
    lane = lax.broadcasted_iota(jnp.int32, q.shape, 1)
    zero = jnp.zeros_like(q)
    qs[0:tq, :] = jnp.where(lane < DK, q, zero)
    qs[tq:2 * tq, :] = jnp.where(lane >= DK, q, zero)
    for c in range(n_blk):
        m_scr[c][...] = jnp.full(m_scr[c].shape, MASK_VALUE, F32)
        acc[c][...] = jnp.zeros(acc[c].shape, F32)

    def scores(j):
        kj = k_ref[0, pl.ds(pl.multiple_of(j * tk, tk), tk), :]
        return [lax.dot_general(kj, qs[c * cb:(c + 1) * cb, :], (((1,), (1,)), ((), ())),
                                preferred_element_type=F32) for c in range(n_blk)]

    def consume(j, bias, prefetch):
        nxt = scores(j + 1) if prefetch else None
        vtj = vt[j]
        alphas, ps = [], []
        for c in range(n_blk):
            s = s_scr[c][...]
            if bias is not None:
                qb = (c * cb) % tq
                s = s + bias[:, qb:qb + cb]
            m_old = m_scr[c][...]
            m_new = jnp.maximum(m_old, jnp.max(s, axis=0, keepdims=True))
            alpha = jnp.exp2(m_old - m_new)
            p = jnp.exp2(s - m_new)
            m_scr[c][...] = m_new
            alphas.append(alpha)
            ps.append(p.astype(BF16))
        pvs = [jnp.dot(vtj, ps[c], preferred_element_type=F32) for c in range(n_blk)]
        for c in range(n_blk):
            acc[c][...] = alphas[c] * acc[c][...] + pvs[c]
        if prefetch:
            for c in range(n_blk):
                s_scr[c][...] = nxt[c]

    first = qi * n_sub - 1
    for c, sc in enumerate(scores(0)):
        s_scr[c][...] = sc

    def far(j, carry):
        consume(j, None, True)
        return carry

    lax.fori_loop(0, jnp.maximum(first, 0), far, 0)

    @pl.when(qi >= 1)
    def _():
        consume(first, bias_ref[0, 0], True)

    for r in range(1, n_sub + 1):
        consume(first + r, bias_ref[0, r], r < n_sub)

    lam = lam_ref[0, 0]
    half = n_blk // 2
    for b in range(half):
        a1, a2 = acc[b], acc[half + b]
        ot = (a1[0:DV, :] / a1[DV:DV + 1, :] - lam * (a2[0:DV, :] / a2[DV:DV + 1, :]))
        ot = ot * lax.rsqrt(jnp.mean(ot * ot, axis=0, keepdims=True) + 1e-5)
        o_ref[0, b * cb:(b + 1) * cb, :] = (ot.T * (g_ref[...] * out_scale)).astype(o_ref.dtype)


def _t5_bucket(n):
    n = jnp.maximum(n, 0)
    max_exact = N_BUCKETS // 2
    nf = jnp.maximum(n, 1).astype(F32)
    large = max_exact + (jnp.log(nf / max_exact) / math.log(MAX_DISTANCE / max_exact)
                         * (N_BUCKETS - max_exact)).astype(jnp.int32)
    large = jnp.minimum(large, N_BUCKETS - 1)
    return jnp.where(n < max_exact, n, large)


def _attn_bias_tiles(rel_bias, tq, tk):
    assert tk >= MAX_DISTANCE and tq % tk == 0
    table = rel_bias.astype(F32)
    rel_table = (table - table[N_BUCKETS - 1][None, :]) * LOG2E
    r = jnp.arange(tk, dtype=jnp.int32)[:, None]
    c = jnp.arange(tq, dtype=jnp.int32)[None, :]
    tiles = []
    for t in range(tq // tk + 1):
        dist = c - (r + (t - 1) * tk)
        onehot = (_t5_bucket(dist)[:, :, None] == jnp.arange(N_BUCKETS, dtype=jnp.int32)).astype(F32)
        b = jnp.einsum('rcn,nh->hrc', onehot, rel_table, precision=HIGHEST)
        tiles.append(jnp.where((dist >= 0)[None], b, MASK_VALUE))
    return jnp.stack(tiles, axis=1)


def _attention(z3, rel_bias, lam, subln_g, lam_init, tq=ATTN_TQ, tk=ATTN_TK, cb=ATTN_CB):
    bt, s_len, _ = z3.shape
    bias = _attn_bias_tiles(rel_bias, tq, tk)
    n_sp = bias.shape[1]
    kern = functools.partial(_attn_kernel, tq=tq, tk=tk, cb=cb, out_scale=1.0 - lam_init)
    n_blk = 2 * tq // cb
    return pl.pallas_call(
        kern,
        grid=(bt, N_HEADS, s_len // tq),
        in_specs=[
            pl.BlockSpec(memory_space=pltpu.SMEM),
            pl.BlockSpec((1, tq, 2 * DK), lambda b, h, i: (b, i, h)),
            pl.BlockSpec((1, s_len, 2 * DK), lambda b, h, i: (b, 0, N_HEADS + h)),
            pl.BlockSpec((1, s_len, DV), lambda b, h, i: (b, 0, 2 * N_HEADS + h)),
            pl.BlockSpec((1, n_sp, tk, tq), lambda b, h, i: (h, 0, 0, 0)),
            pl.BlockSpec((1, DV), lambda b, h, i: (0, 0)),
        ],
        out_specs=pl.BlockSpec((1, tq, DV), lambda b, h, i: (b, i, h)),
        out_shape=jax.ShapeDtypeStruct((bt, s_len, N_HEADS * DV), BF16),
        scratch_shapes=[
            pltpu.VMEM((2 * tq, 2 * DK), BF16),
            pltpu.VMEM((s_len // tk, DV + ONES_ROWS, tk), BF16),
        ] + [pltpu.VMEM((1, cb), F32)] * n_blk + [pltpu.VMEM((DV + ONES_ROWS, cb), F32)] * n_blk
        + [pltpu.VMEM((tk, cb), F32)] * n_blk,
        compiler_params=_params(("parallel", "parallel", "arbitrary")),
        name="diff_attention",
    )(lam.reshape(1, 1).astype(F32), z3, z3, z3, bias, subln_g.reshape(1, DV).astype(F32))


def _ssm_operators(lam_re, lam_im, log_dt, b_re, b_im, c_re, c_im, d_skip):
    L, H, P = SSM_CHUNK, SSM_GROUP, SSM_STATE
    lre = lam_re.astype(F32)
    lim = lam_im.astype(F32)
    dt = jnp.exp(log_dt.astype(F32))[:, None]
    mag = jnp.exp(lre * dt)
    ab_re = mag * jnp.cos(lim * dt)
    ab_im = mag * jnp.sin(lim * dt)
    den = lre * lre + lim * lim
    nr, ni = ab_re - 1.0, ab_im
    cr = ((nr * lre + ni * lim) / den)[..., None]
    ci = ((ni * lre - nr * lim) / den)[..., None]
    bre = b_re.astype(F32)
    bim = b_im.astype(F32)
    bb_re = cr * bre - ci * bim
    bb_im = cr * bim + ci * bre
    cre = c_re.astype(F32)
    cim = c_im.astype(F32)

    tau = jnp.arange(L + 1, dtype=F32)[:, None, None]
    pw_mag = jnp.exp(tau * (lre * dt)[None])
    pw_re = pw_mag * jnp.cos(tau * (lim * dt)[None])
    pw_im = pw_mag * jnp.sin(tau * (lim * dt)[None])

    ca_re = cre[None] * pw_re[:, :, None, :] - cim[None] * pw_im[:, :, None, :]
    ca_im = cre[None] * pw_im[:, :, None, :] + cim[None] * pw_re[:, :, None, :]
    bbt_re = bb_re.transpose(0, 2, 1)[None, :, None]
    bbt_im = bb_im.transpose(0, 2, 1)[None, :, None]
    k_tau = jnp.sum(ca_re[:L, :, :, None, :] * bbt_re - ca_im[:L, :, :, None, :] * bbt_im, axis=-1)
    ti = jnp.arange(L)
    t_src = jnp.tile(k_tau.transpose(0, 1, 3, 2).reshape(L, SSM_GROUPS * H, H), (1, 1, SSM_BUNDLE))

    rev_re = pw_re[L - 1 - ti]
    rev_im = pw_im[L - 1 - ti]
    w_re = rev_re[..., None] * bb_re[None] - rev_im[..., None] * bb_im[None]
    w_im = rev_re[..., None] * bb_im[None] + rev_im[..., None] * bb_re[None]
    w_src = jnp.concatenate([w_re, w_im], axis=2).transpose(0, 1, 3, 2)

    e_src = jnp.concatenate([ca_re[1:L + 1], -ca_im[1:L + 1]], axis=3)

    a1 = jnp.concatenate([pw_re[L], pw_re[L]], axis=-1)
    a2 = jnp.concatenate([-pw_im[L], pw_im[L]], axis=-1)
    d_tile = jnp.tile(d_skip.astype(F32).reshape(SSM_GROUPS // SSM_BUNDLE, 1, SSM_BUNDLE * H), (1, 1, L))
    return t_src.astype(BF16), w_src.astype(BF16), e_src.astype(BF16), a1, a2, d_tile


def _bundle_operator(dst, src_ref):
    n_l, n_g, h, x = src_ref.shape
    dst[...] = jnp.zeros(dst.shape, dst.dtype)
    for i in range(n_l):
        for g in range(n_g):
            r0 = (i * n_g + g) * h
            dst[r0:r0 + h, g * x:(g + 1) * x] = src_ref[i, g]


def _chunk_rows(u_ref, n_chunk):
    return jnp.concatenate([u_ref[pl.ds(t, n_chunk, stride=SSM_CHUNK), :] for t in range(SSM_CHUNK)], axis=1)


def _ssm_in_kernel(u_ref, w_ref, v_ref, w_scr):
    @pl.when(pl.program_id(1) == 0)
    def _():
        _bundle_operator(w_scr, w_ref)

    x = _chunk_rows(u_ref, v_ref.shape[0]).astype(BF16)
    v = jnp.dot(x, w_scr[...], preferred_element_type=F32)
    sw = v_ref.shape[2]
    for g in range(v_ref.shape[1]):
        v_ref[:, g, :] = v[:, g * sw:(g + 1) * sw]


def _ssm_scan_kernel(v_ref, a1_ref, a2_ref, o_ref, st):
    @pl.when(pl.program_id(0) == 0)
    def _():
        st[...] = jnp.zeros(st.shape, F32)

    a1 = a1_ref[...][None]
    a2 = a2_ref[...][None]
    n_chunk = v_ref.shape[1]
    half = v_ref.shape[3] // 2

    def body(c, s):
        o_ref[:, pl.ds(c, 1)] = s[:, None].astype(o_ref.dtype)
        v = v_ref[:, pl.ds(c, 1)][:, 0]
        return a1 * s + a2 * pltpu.roll(s, half, axis=2) + v

    st[...] = lax.fori_loop(0, n_chunk, body, st[...])


def _gelu_tanh(x):
    c = math.sqrt(2.0 / math.pi)
    return x * (0.5 * (1.0 + jnp.tanh(c * (x + 0.044715 * (x * x * x)))))


def _ssm_out_kernel(u_ref, t_ref, s_ref, e_ref, d_ref, y_ref, t_scr, et_scr):
    L, lw, h = SSM_CHUNK, t_ref.shape[1], SSM_GROUP

    @pl.when(pl.program_id(1) == 0)
    def _():
        _bundle_operator(et_scr, e_ref)
        same_group = (lax.broadcasted_iota(jnp.int32, (lw, lw), 0) // h
                      == lax.broadcasted_iota(jnp.int32, (lw, lw), 1) // h)
        t_scr[...] = jnp.zeros(t_scr.shape, t_scr.dtype)
        for tau in range(L):
            blk = jnp.where(same_group, t_ref[tau], jnp.zeros((lw, lw), t_ref.dtype))
            for i in range(L - tau):
                j = i + tau
                t_scr[i * lw:(i + 1) * lw, j * lw:(j + 1) * lw] = blk

    n_chunk = s_ref.shape[0]
    xf = _chunk_rows(u_ref, n_chunk)
    sp = jnp.concatenate([s_ref[:, g, :] for g in range(s_ref.shape[1])], axis=1).astype(BF16)
    y = jnp.dot(xf.astype(BF16), t_scr[...], preferred_element_type=F32)
    y = y + lax.dot_general(sp, et_scr[...], (((1,), (1,)), ((), ())), preferred_element_type=F32)
    y = _gelu_tanh(y + d_ref[0] * xf)
    for t in range(L):
        y_ref[pl.ds(t, n_chunk, stride=L), :] = y[:, t * lw:(t + 1) * lw]


def _ssm(u, ops, bt, scan_block=32, chunk_tile=256):
    t_src, w_src, e_src, a1, a2, d_tile = ops
    n, width = u.shape
    L, G, H, P2, nbg = SSM_CHUNK, SSM_GROUPS, SSM_GROUP, 2 * SSM_STATE, SSM_BUNDLE
    nb = G // nbg
    lw = nbg * H
    nc = n // L
    n_c = nc // bt
    ct = min(chunk_tile, nc)
    u_spec = pl.BlockSpec((ct * L, lw), lambda b, i: (i, b))
    src_spec = pl.BlockSpec((L, nbg, H, P2), lambda b, i: (0, b, 0, 0))
    state_spec = pl.BlockSpec((ct, nbg, P2), lambda b, i: (i, b, 0))

    v = pl.pallas_call(
        _ssm_in_kernel,
        grid=(nb, nc // ct),
        in_specs=[u_spec, src_spec],
        out_specs=state_spec,
        out_shape=jax.ShapeDtypeStruct((nc, G, P2), F32),
        scratch_shapes=[pltpu.VMEM((L * lw, nbg * P2), BF16)],
        compiler_params=_params(("parallel", "arbitrary")),
        name="ssm_chunk_state",
    )(u, w_src)

    cb = min(scan_block, n_c)
    s_prev = pl.pallas_call(
        _ssm_scan_kernel,
        grid=(n_c // cb,),
        in_specs=[pl.BlockSpec((bt, cb, G, P2), lambda c: (0, c, 0, 0)),
                  pl.BlockSpec((G, P2), lambda c: (0, 0)),
                  pl.BlockSpec((G, P2), lambda c: (0, 0))],
        out_specs=pl.BlockSpec((bt, cb, G, P2), lambda c: (0, c, 0, 0)),
        out_shape=jax.ShapeDtypeStruct((bt, n_c, G, P2), F32),
        scratch_shapes=[pltpu.VMEM((bt, G, P2), F32)],
        compiler_params=_params(("arbitrary",)),
        name="ssm_scan",
    )(v.reshape(bt, n_c, G, P2), a1, a2)

    return pl.pallas_call(
        _ssm_out_kernel,
        grid=(nb, nc // ct),
        in_specs=[u_spec,
                  pl.BlockSpec((L, lw, lw), lambda b, i: (0, b, 0)),
                  state_spec,
                  src_spec,
                  pl.BlockSpec((1, 1, L * lw), lambda b, i: (b, 0, 0))],
        out_specs=pl.BlockSpec((ct * L, lw), lambda b, i: (i, b)),
        out_shape=jax.ShapeDtypeStruct((n, width), F32),
        scratch_shapes=[pltpu.VMEM((L * lw, L * lw), BF16), pltpu.VMEM((L * lw, nbg * P2), BF16)],
        compiler_params=_params(("parallel", "arbitrary")),
        name="ssm_output",
    )(u, t_src, s_prev.reshape(nc, G, P2), e_src, d_tile)


def _postmix_kernel(a_ref, s_ref, x_ref, wg_ref, bg_ref, gs_ref, woa_ref, wos_ref, gf_ref,
                    wrh_ref, wrl_ref, br_ref, x1_ref, h_ref, lg_ref):
    sf = s_ref[...]
    gate = _sigmoid(jnp.dot(sf.astype(BF16), wg_ref[...], preferred_element_type=F32) + bg_ref[...])
    sn = _rms(sf * gate, gs_ref[...], 1e-6).astype(BF16)
    x1 = (x_ref[...]
          + jnp.dot(a_ref[...], woa_ref[...], preferred_element_type=F32)
          + jnp.dot(sn, wos_ref[...], preferred_element_type=F32))
    x1_ref[...] = x1
    h = _rms(x1, gf_ref[...], 1e-6)
    h_hi = h.astype(BF16)
    h_lo = (h - h_hi.astype(F32)).astype(BF16)
    hf = h_hi.astype(F32)
    for j in range(hf.shape[1] // LANES):
        h_ref[pl.ds(j, hf.shape[0], stride=hf.shape[1] // LANES), :] = hf[:, j * LANES:(j + 1) * LANES]
    lg_ref[...] = (jnp.dot(h_hi, wrh_ref[...], preferred_element_type=F32)
                   + jnp.dot(h_lo, wrh_ref[...], preferred_element_type=F32)
                   + jnp.dot(h_hi, wrl_ref[...], preferred_element_type=F32)
                   + br_ref[...])


def _postmix(a, s, x2d, w_glu, b_glu, g_s, w_o, g_ffn, w_r, b_r, tm=256):
    n, d = x2d.shape
    wa = a.shape[1]
    ws = s.shape[1]
    wr_hi = w_r.astype(BF16)
    wr_lo = (w_r - wr_hi.astype(F32)).astype(BF16)
    row = lambda i: (i, 0)
    fixed = lambda i: (0, 0)
    return pl.pallas_call(
        _postmix_kernel,
        grid=(n // tm,),
        in_specs=[
            pl.BlockSpec((tm, wa), row),
            pl.BlockSpec((tm, ws), row),
            pl.BlockSpec((tm, d), row),
            pl.BlockSpec((ws, ws), fixed),
            pl.BlockSpec((1, ws), fixed),
            pl.BlockSpec((1, ws), fixed),
            pl.BlockSpec((wa, d), fixed),
            pl.BlockSpec((ws, d), fixed),
            pl.BlockSpec((1, d), fixed),
            pl.BlockSpec((d, ROUTER_LANES), fixed),
            pl.BlockSpec((d, ROUTER_LANES), fixed),
            pl.BlockSpec((1, ROUTER_LANES), fixed),
        ],
        out_specs=[pl.BlockSpec((tm, d), row), pl.BlockSpec((tm * (d // LANES), LANES), row),
                   pl.BlockSpec((tm, ROUTER_LANES), row)],
        out_shape=[jax.ShapeDtypeStruct((n, d), F32), jax.ShapeDtypeStruct((n * (d // LANES), LANES), F32),
                   jax.ShapeDtypeStruct((n, ROUTER_LANES), F32)],
        compiler_params=_params(("parallel",)),
        name="postmix",
    )(a, s, x2d, w_glu.astype(BF16), b_glu.reshape(1, ws).astype(F32), g_s.reshape(1, ws).astype(F32),
      w_o[:wa].astype(BF16), w_o[wa:].astype(BF16), g_ffn.reshape(1, d).astype(F32),
      wr_hi, wr_lo, b_r.reshape(1, ROUTER_LANES).astype(F32))


def _route(logits, rows, slab_rows):
    n_tok = logits.shape[0]
    lg = logits[:, :N_EXPERT_GROUPS]
    le = logits[:, N_EXPERT_GROUPS:N_EXPERT_GROUPS + N_EXPERTS].reshape(
        n_tok, N_EXPERT_GROUPS, EXPERTS_PER_GROUP)
    pg = jax.nn.softmax(lg, axis=-1)
    gsel = jnp.argmax(lg, axis=-1).astype(jnp.int32)
    gate_g = jnp.max(pg, axis=-1, keepdims=True)
    sel = gsel[:, None] == jnp.arange(N_EXPERT_GROUPS, dtype=jnp.int32)[None, :]
    le_sel = jnp.sum(jnp.where(sel[:, :, None], le, 0.0), axis=1)
    pe = jax.nn.softmax(le_sel, axis=-1)
    top_p, top_i = lax.top_k(pe, TOP_K)
    w = gate_g * top_p / jnp.sum(top_p, axis=-1, keepdims=True)
    eid = gsel[:, None] * EXPERTS_PER_GROUP + top_i.astype(jnp.int32)

    n_assign = n_tok * TOP_K
    flat_e = eid.reshape(-1)
    counts = jnp.sum((flat_e[:, None] == jnp.arange(N_EXPERTS, dtype=jnp.int32)[None, :])
                     .astype(jnp.int32), axis=0)
    n_blk_e = (counts + rows - 1) // rows
    blk_end = jnp.cumsum(n_blk_e)
    blk_first = blk_end - n_blk_e
    start = jnp.cumsum(counts) - counts
    order = jnp.argsort(flat_e, stable=True).astype(jnp.int32)
    n_blk = (n_assign + rows - 1) // rows + N_EXPERTS
    blk = jnp.arange(n_blk, dtype=jnp.int32)
    blk_e = jnp.minimum(jnp.sum((blk[:, None] >= blk_end[None, :]).astype(jnp.int32), axis=1),
                        N_EXPERTS - 1)
    used = blk < blk_end[-1]
    in_e = (blk - blk_first[blk_e]) * rows
    blk_cnt = jnp.where(used, jnp.clip(counts[blk_e] - in_e, 0, rows), 0).astype(jnp.int32)
    blk_src = (start[blk_e] + in_e).astype(jnp.int32)
    last_e = blk_e[jnp.maximum(blk_end[-1] - 1, 0)]
    blk_e = jnp.where(used, blk_e, last_e)
    row_tok = order // TOP_K
    row_dst = (order % TOP_K) * n_tok + row_tok
    return w, row_tok * slab_rows, row_dst * slab_rows, blk_e, blk_cnt, blk_src


def _expert_kernel(blk_e_ref, blk_cnt_ref, blk_src_ref, tok_ref, dst_ref,
                   h_hbm, w1_ref, w3_ref, w2_ref, y_hbm, xbuf, ybuf, gsem, ssem):
    del blk_e_ref
    b = pl.program_id(0)
    nb = pl.num_programs(0)
    slot = b % 2
    cnt = blk_cnt_ref[b]
    spr = w1_ref.shape[1] // LANES
    rows = xbuf.shape[1] // spr

    def for_rows(n, body):
        n_main = n // ISSUE_UNROLL

        def main(i, c):
            for u in range(ISSUE_UNROLL):
                body(i * ISSUE_UNROLL + u)
            return c

        def tail(r, c):
            body(r)
            return c

        lax.fori_loop(0, n_main, main, 0)
        lax.fori_loop(n_main * ISSUE_UNROLL, n, tail, 0)

    def gather_copy(src, r, sl):
        return pltpu.make_async_copy(h_hbm.at[pl.ds(pl.multiple_of(tok_ref[src + r], spr), spr)],
                                     xbuf.at[sl, pl.ds(pl.multiple_of(r * spr, spr), spr)], gsem.at[sl])

    def scatter_copy(src, r, sl):
        return pltpu.make_async_copy(ybuf.at[sl, pl.ds(pl.multiple_of(r * spr, spr), spr)],
                                     y_hbm.at[pl.ds(pl.multiple_of(dst_ref[src + r], spr), spr)], ssem.at[sl])

    def start_gathers(blk, sl):
        src = blk_src_ref[blk]
        for_rows(blk_cnt_ref[blk], lambda r: gather_copy(src, r, sl).start())

    def wait_gathers(blk, sl):
        src = blk_src_ref[blk]
        for_rows(blk_cnt_ref[blk], lambda r: gather_copy(src, r, sl).wait())

    def start_scatters(blk, sl):
        src = blk_src_ref[blk]
        for_rows(blk_cnt_ref[blk], lambda r: scatter_copy(src, r, sl).start())

    def wait_scatters(blk, sl):
        src = blk_src_ref[blk]
        for_rows(blk_cnt_ref[blk], lambda r: scatter_copy(src, r, sl).wait())

    @pl.when(b == 0)
    def _():
        xbuf[...] = jnp.zeros(xbuf.shape, xbuf.dtype)
        start_gathers(0, 0)

    @pl.when(b + 1 < nb)
    def _():
        start_gathers(b + 1, 1 - slot)

    @pl.when(b >= 2)
    def _():
        wait_scatters(b - 2, slot)

    @pl.when(cnt > 0)
    def _():
        wait_gathers(b, slot)
        x = jnp.concatenate([xbuf[slot, pl.ds(j, rows, stride=spr), :] for j in range(spr)], axis=1).astype(BF16)
        h1 = jnp.dot(x, w1_ref[0].astype(BF16), preferred_element_type=F32)
        h3 = jnp.dot(x, w3_ref[0].astype(BF16), preferred_element_type=F32)
        act = (h1 * _sigmoid(h1) * h3).astype(BF16)
        y = jnp.dot(act, w2_ref[0].astype(BF16), preferred_element_type=F32)
        for j in range(spr):
            ybuf[slot, pl.ds(j, rows, stride=spr), :] = y[:, j * LANES:(j + 1) * LANES]
        start_scatters(b, slot)

    @pl.when(b == nb - 1)
    def _():
        @pl.when(b >= 1)
        def _():
            wait_scatters(b - 1, 1 - slot)
        wait_scatters(b, slot)


def _experts(h, w1, w3, w2, row_tok, row_dst, blk_e, blk_cnt, blk_src, rows):
    d, f = w1.shape[1], w1.shape[2]
    spr = d // LANES
    n_blk = blk_e.shape[0]
    wmap = lambda b, be, bc, bs, rt, rd: (be[b], 0, 0)
    grid_spec = pltpu.PrefetchScalarGridSpec(
        num_scalar_prefetch=5,
        grid=(n_blk,),
        in_specs=[
            pl.BlockSpec(memory_space=pl.ANY),
            pl.BlockSpec((1, d, f), wmap),
            pl.BlockSpec((1, d, f), wmap),
            pl.BlockSpec((1, f, d), wmap),
        ],
        out_specs=pl.BlockSpec(memory_space=pl.ANY),
        scratch_shapes=[
            pltpu.VMEM((2, rows * spr, LANES), F32),
            pltpu.VMEM((2, rows * spr, LANES), F32),
            pltpu.SemaphoreType.DMA((2,)),
            pltpu.SemaphoreType.DMA((2,)),
        ],
    )
    return pl.pallas_call(
        _expert_kernel,
        grid_spec=grid_spec,
        out_shape=jax.ShapeDtypeStruct((h.shape[0] * TOP_K, LANES), F32),
        compiler_params=_params(("arbitrary",), disable_bounds_checks=True),
        name="experts",
    )(blk_e, blk_cnt, blk_src, row_tok, row_dst, h, w1, w3, w2)


def _ple_kernel(x1_ref, y0_ref, y1_ref, w_ref, p_ref, gp_ref, wg_ref, wp_ref, gf_ref, o_ref, *, final):
    w = w_ref[...]
    tm, d = x1_ref.shape
    spr = d // LANES
    y0 = jnp.concatenate([y0_ref[pl.ds(j, tm, stride=spr), :] for j in range(spr)], axis=1)
    y1 = jnp.concatenate([y1_ref[pl.ds(j, tm, stride=spr), :] for j in range(spr)], axis=1)
    x2 = x1_ref[...] + w[:, 0:1] * y0 + w[:, 1:2] * y1
    hn = _rms(x2, gp_ref[...], 1e-6).astype(BF16)
    gate = _sigmoid(jnp.dot(hn, wg_ref[...], preferred_element_type=F32))
    pp = jnp.dot(p_ref[...].astype(BF16), wp_ref[...], preferred_element_type=F32)
    x3 = x2 + gate * pp
    o_ref[...] = _rms(x3, gf_ref[...], 1e-6) if final else x3


def _ple(x1, y2, w, p2d, g_ple, w_gate, w_proj, g_final, final, tm=256):
    n, d = x1.shape
    pd = p2d.shape[1]
    row = lambda i: (i, 0)
    fixed = lambda i: (0, 0)
    return pl.pallas_call(
        functools.partial(_ple_kernel, final=final),
        grid=(n // tm,),
        in_specs=[
            pl.BlockSpec((tm, d), row),
            pl.BlockSpec((tm * (d // LANES), LANES), row),
            pl.BlockSpec((tm * (d // LANES), LANES), lambda i: (n // tm + i, 0)),
            pl.BlockSpec((tm, TOP_K), row),
            pl.BlockSpec((tm, pd), row),
            pl.BlockSpec((1, d), fixed),
            pl.BlockSpec((d, d), fixed),
            pl.BlockSpec((pd, d), fixed),
            pl.BlockSpec((1, d), fixed),
        ],
        out_specs=pl.BlockSpec((tm, d), row),
        out_shape=jax.ShapeDtypeStruct((n, d), F32),
        compiler_params=_params(("parallel",)),
        name="ple_final",
    )(x1, y2, y2, w, p2d, g_ple.reshape(1, d).astype(F32), w_gate.astype(BF16), w_proj.astype(BF16),
      g_final.reshape(1, d).astype(F32))


def kernel(x, p, rel_bias, g_mix, w_in, lam_q1, lam_k1, lam_q2, lam_k2, subln_g, ssm_lam_re, ssm_lam_im, ssm_log_dt, ssm_b_re, ssm_b_im, ssm_c_re, ssm_c_im, ssm_d, w_glu, b_glu, ssm_norm_g, w_o, g_ffn, w_router_g, b_router_g, w_router_e, b_router_e, w1, w3, w2, g_ple, w_ple_gate, w_ple_proj, g_final):
    bt, s_len, d = x.shape
    n = bt * s_len
    depth = g_mix.shape[0]
    attn_w = N_HEADS * DV
    xc = x.reshape(n, d).astype(F32)
    for i in range(depth):
        lam_init = 0.8 - 0.6 * math.exp(-0.3 * i)
        col_scale = jnp.concatenate([jnp.full((attn_w,), LOG2E * DK ** -0.5, F32),
                                     jnp.ones((w_in.shape[2] - attn_w,), F32)])
        w_in_b = (w_in[i].astype(F32) * col_scale[None, :]).astype(BF16)
        z, u = _inproj(xc, g_mix[i].astype(F32), w_in_b)
        z3 = z.reshape(bt, s_len, z.shape[1])

        lam = (jnp.exp(jnp.sum(lam_q1[i].astype(F32) * lam_k1[i].astype(F32)))
               - jnp.exp(jnp.sum(lam_q2[i].astype(F32) * lam_k2[i].astype(F32))) + lam_init)
        a = _attention(z3, rel_bias, lam, subln_g[i], lam_init)

        ops = _ssm_operators(ssm_lam_re[i], ssm_lam_im[i], ssm_log_dt[i], ssm_b_re[i], ssm_b_im[i],
                             ssm_c_re[i], ssm_c_im[i], ssm_d[i])
        s = _ssm(u, ops, bt)

        w_r = jnp.zeros((d, ROUTER_LANES), F32)
        w_r = w_r.at[:, :N_EXPERT_GROUPS].set(w_router_g[i].astype(F32))
        w_r = w_r.at[:, N_EXPERT_GROUPS:N_EXPERT_GROUPS + N_EXPERTS].set(w_router_e[i].astype(F32))
        b_r = jnp.zeros((ROUTER_LANES,), F32)
        b_r = b_r.at[:N_EXPERT_GROUPS].set(b_router_g[i].astype(F32))
        b_r = b_r.at[N_EXPERT_GROUPS:N_EXPERT_GROUPS + N_EXPERTS].set(b_router_e[i].astype(F32))
        x1, h, logits = _postmix(a.reshape(n, attn_w), s, xc, w_glu[i], b_glu[i],
                                 ssm_norm_g[i], w_o[i], g_ffn[i], w_r, b_r)

        gate_w, row_tok, row_dst, blk_e, blk_cnt, blk_src = _route(logits, EXPERT_ROWS, d // LANES)
        y2 = _experts(h, w1[i], w3[i], w2[i],
                      row_tok, row_dst, blk_e, blk_cnt, blk_src, EXPERT_ROWS)
        xc = _ple(x1, y2, gate_w.astype(F32), p[i].reshape(n, -1).astype(F32),
                  g_ple[i], w_ple_gate[i], w_ple_proj[i], g_final, final=(i == depth - 1))
    return xc.reshape(bt, s_len, d)
```

```python
import functools
import math

import jax
import jax.numpy as jnp
from jax import lax
from jax.experimental import pallas as pl
from jax.experimental.pallas import tpu as pltpu

F32 = jnp.float32
BF16 = jnp.bfloat16
HIGHEST = lax.Precision.HIGHEST

N_HEADS = 8
DK = 64
DV = 128
N_BUCKETS = 32
MAX_DISTANCE = 128
SSM_GROUP = 16
SSM_GROUPS = 64
SSM_STATE = 64
N_EXPERT_GROUPS = 4
EXPERTS_PER_GROUP = 8
N_EXPERTS = 32
TOP_K = 2
MASK_VALUE = -1e30
LOG2E = math.log2(math.e)

SSM_CHUNK = 16
SSM_BUNDLE = 8
ATTN_TQ = 512
ONES_ROWS = 16
ATTN_CB = 256
ATTN_TK = 256
EXPERT_ROWS = 256
ISSUE_UNROLL = 8
LANES = 128
ROUTER_LANES = 128
VMEM_LIMIT = 56 << 20


def _params(semantics, **kw):
    return pltpu.CompilerParams(dimension_semantics=semantics, vmem_limit_bytes=VMEM_LIMIT, **kw)


def _rms(x, g, eps):
    return x * lax.rsqrt(jnp.mean(x * x, axis=-1, keepdims=True) + eps) * g


def _sigmoid(x):
    return 1.0 / (1.0 + jnp.exp(-x))


def _inproj_kernel(x_ref, g_ref, w_ref, z_ref, u_ref, h_scr):
    j = pl.program_id(1)
    last = pl.num_programs(1) - 1

    @pl.when(j == 0)
    def _():
        h_scr[...] = _rms(x_ref[...], g_ref[...], 1e-6).astype(BF16)

    acc = jnp.dot(h_scr[...], w_ref[...], preferred_element_type=F32)

    @pl.when(j < last)
    def _():
        z_ref[...] = acc.astype(z_ref.dtype)

    @pl.when(j == last)
    def _():
        u_ref[...] = acc


def _inproj(x2d, g, w, tm=512, tn=1024):
    n, d = x2d.shape
    pw = w.shape[1]
    n_col = pw // tn
    return pl.pallas_call(
        _inproj_kernel,
        grid=(n // tm, n_col),
        in_specs=[
            pl.BlockSpec((tm, d), lambda i, j: (i, 0)),
            pl.BlockSpec((1, d), lambda i, j: (0, 0)),
            pl.BlockSpec((d, tn), lambda i, j: (0, j)),
        ],
        out_specs=[pl.BlockSpec((tm, tn), lambda i, j: (i, jnp.minimum(j, n_col - 2))),
                   pl.BlockSpec((tm, tn), lambda i, j: (i, 0))],
        out_shape=[jax.ShapeDtypeStruct((n, pw - tn), BF16), jax.ShapeDtypeStruct((n, tn), F32)],
        scratch_shapes=[pltpu.VMEM((tm, d), BF16)],
        compiler_params=_params(("parallel", "arbitrary")),
        name="inproj",
    )(x2d, g.reshape(1, d), w)


def _attn_kernel(lam_ref, q_ref, k_ref, v_ref, bias_ref, g_ref, o_ref, qs, vt, *state,
                 tq, tk, cb, out_scale):
    n_blk = 2 * tq // cb
    m_scr, acc, s_scr = (state[i * n_blk:(i + 1) * n_blk] for i in range(3))
    qi = pl.program_id(2)
    n_sub = tq // tk
    n_kv = v_ref.shape[1] // tk

    @pl.when(qi == 0)
    def _():
        for c in range(n_kv):
            vt[c, 0:DV, :] = v_ref[0, c * tk:(c + 1) * tk, :].astype(F32).T.astype(BF16)
            vt[c, DV:DV + ONES_ROWS, :] = jnp.ones((ONES_ROWS, tk), BF16)

    q = q_ref[0]
    lane = lax.broadcasted_iota(jnp.int32, q.shape, 1)
    zero = jnp.zeros_like(q)
    qs[0:tq, :] = jnp.where(lane < DK, q, zero)
    qs[tq:2 * tq, :] = jnp.where(lane >= DK, q, zero)
    for c in range(n_blk):
        m_scr[c][...] = jnp.full(m_scr[c].shape, MASK_VALUE, F32)
        acc[c][...] = jnp.zeros(acc[c].shape, F32)

    all_blocks = tuple(range(n_blk))

    def visible(r):
        return tuple(c for c in all_blocks if (c * cb) % tq + cb > (r - 1) * tk)

    def scores(j, blocks):
        kj = k_ref[0, pl.ds(pl.multiple_of(j * tk, tk), tk), :]
        return [lax.dot_general(kj, qs[c * cb:(c + 1) * cb, :], (((1,), (1,)), ((), ())),
                                preferred_element_type=F32) for c in blocks]

    def consume(j, bias, blocks, next_blocks):
        nxt = scores(j + 1, next_blocks) if next_blocks else None
        vtj = vt[j]
        alphas, ps = [], []
        for c in blocks:
            s = s_scr[c][...]
            if bias is not None:
                qb = (c * cb) % tq
                s = s + bias[:, qb:qb + cb]
            m_old = m_scr[c][...]
            m_new = jnp.maximum(m_old, jnp.max(s, axis=0, keepdims=True))
            alpha = jnp.exp2(m_old - m_new)
            p = jnp.exp2(s - m_new)
            m_scr[c][...] = m_new
            alphas.append(alpha)
            ps.append(p.astype(BF16))
        pvs = [jnp.dot(vtj, p, preferred_element_type=F32) for p in ps]
        for c, alpha, pv in zip(blocks, alphas, pvs):
            acc[c][...] = alpha * acc[c][...] + pv
        if next_blocks:
            for c, sc in zip(next_blocks, nxt):
                s_scr[c][...] = sc

    first = qi * n_sub - 1
    for c, sc in zip(all_blocks, scores(0, all_blocks)):
        s_scr[c][...] = sc

    def far_pair(i, carry):
        consume(2 * i, None, all_blocks, all_blocks)
        consume(2 * i + 1, None, all_blocks, all_blocks)
        return carry

    lax.fori_loop(0, jnp.maximum(first, 0) // 2, far_pair, 0)

    @pl.when(qi >= 1)
    def _():
        consume(first - 1, None, all_blocks, all_blocks)
        consume(first, bias_ref[0, 0], all_blocks, all_blocks)

    for r in range(1, n_sub + 1):
        consume(first + r, bias_ref[0, r], visible(r), visible(r + 1) if r < n_sub else None)

    lam = lam_ref[0, 0]
    half = n_blk // 2
    for b in range(half):
        a1, a2 = acc[b], acc[half + b]
        ot = (a1[0:DV, :] / a1[DV:DV + 1, :] - lam * (a2[0:DV, :] / a2[DV:DV + 1, :]))
        ot = ot * lax.rsqrt(jnp.mean(ot * ot, axis=0, keepdims=True) + 1e-5)
        o_ref[0, b * cb:(b + 1) * cb, :] = (ot.T * (g_ref[...] * out_scale)).astype(o_ref.dtype)


def _t5_bucket(n):
    n = jnp.maximum(n, 0)
    max_exact = N_BUCKETS // 2
    nf = jnp.maximum(n, 1).astype(F32)
    large = max_exact + (jnp.log(nf / max_exact) / math.log(MAX_DISTANCE / max_exact)
                         * (N_BUCKETS - max_exact)).astype(jnp.int32)
    large = jnp.minimum(large, N_BUCKETS - 1)
    return jnp.where(n < max_exact, n, large)


def _attn_bias_tiles(rel_bias, tq, tk):
    assert tk >= MAX_DISTANCE and tq % (2 * tk) == 0
    table = rel_bias.astype(F32)
    rel_table = (table - table[N_BUCKETS - 1][None, :]) * LOG2E
    r = jnp.arange(tk, dtype=jnp.int32)[:, None]
    c = jnp.arange(tq, dtype=jnp.int32)[None, :]
    tiles = []
    for t in range(tq // tk + 1):
        dist = c - (r + (t - 1) * tk)
        onehot = (_t5_bucket(dist)[:, :, None] == jnp.arange(N_BUCKETS, dtype=jnp.int32)).astype(F32)
        b = jnp.einsum('rcn,nh->hrc', onehot, rel_table, precision=HIGHEST)
        tiles.append(jnp.where((dist >= 0)[None], b, MASK_VALUE))
    return jnp.stack(tiles, axis=1)


def _attention(z3, rel_bias, lam, subln_g, lam_init, tq=ATTN_TQ, tk=ATTN_TK, cb=ATTN_CB):
    bt, s_len, _ = z3.shape
    bias = _attn_bias_tiles(rel_bias, tq, tk)
    n_sp = bias.shape[1]
    kern = functools.partial(_attn_kernel, tq=tq, tk=tk, cb=cb, out_scale=1.0 - lam_init)
    n_blk = 2 * tq // cb
    return pl.pallas_call(
        kern,
        grid=(bt, N_HEADS, s_len // tq),
        in_specs=[
            pl.BlockSpec(memory_space=pltpu.SMEM),
            pl.BlockSpec((1, tq, 2 * DK), lambda b, h, i: (b, i, h)),
            pl.BlockSpec((1, s_len, 2 * DK), lambda b, h, i: (b, 0, N_HEADS + h)),
            pl.BlockSpec((1, s_len, DV), lambda b, h, i: (b, 0, 2 * N_HEADS + h)),
            pl.BlockSpec((1, n_sp, tk, tq), lambda b, h, i: (h, 0, 0, 0)),
            pl.BlockSpec((1, DV), lambda b, h, i: (0, 0)),
        ],
        out_specs=pl.BlockSpec((1, tq, DV), lambda b, h, i: (b, i, h)),
        out_shape=jax.ShapeDtypeStruct((bt, s_len, N_HEADS * DV), BF16),
        scratch_shapes=[
            pltpu.VMEM((2 * tq, 2 * DK), BF16),
            pltpu.VMEM((s_len // tk, DV + ONES_ROWS, tk), BF16),
        ] + [pltpu.VMEM((1, cb), F32)] * n_blk + [pltpu.VMEM((DV + ONES_ROWS, cb), F32)] * n_blk
        + [pltpu.VMEM((tk, cb), F32)] * n_blk,
        compiler_params=_params(("parallel", "parallel", "arbitrary")),
        name="diff_attention",
    )(lam.reshape(1, 1).astype(F32), z3, z3, z3, bias, subln_g.reshape(1, DV).astype(F32))


def _ssm_operators(lam_re, lam_im, log_dt, b_re, b_im, c_re, c_im, d_skip):
    L, H, P = SSM_CHUNK, SSM_GROUP, SSM_STATE
    lre = lam_re.astype(F32)
    lim = lam_im.astype(F32)
    dt = jnp.exp(log_dt.astype(F32))[:, None]
    mag = jnp.exp(lre * dt)
    ab_re = mag * jnp.cos(lim * dt)
    ab_im = mag * jnp.sin(lim * dt)
    den = lre * lre + lim * lim
    nr, ni = ab_re - 1.0, ab_im
    cr = ((nr * lre + ni * lim) / den)[..., None]
    ci = ((ni * lre - nr * lim) / den)[..., None]
    bre = b_re.astype(F32)
    bim = b_im.astype(F32)
    bb_re = cr * bre - ci * bim
    bb_im = cr * bim + ci * bre
    cre = c_re.astype(F32)
    cim = c_im.astype(F32)

    tau = jnp.arange(L + 1, dtype=F32)[:, None, None]
    pw_mag = jnp.exp(tau * (lre * dt)[None])
    pw_re = pw_mag * jnp.cos(tau * (lim * dt)[None])
    pw_im = pw_mag * jnp.sin(tau * (lim * dt)[None])

    ca_re = cre[None] * pw_re[:, :, None, :] - cim[None] * pw_im[:, :, None, :]
    ca_im = cre[None] * pw_im[:, :, None, :] + cim[None] * pw_re[:, :, None, :]
    bbt_re = bb_re.transpose(0, 2, 1)[None, :, None]
    bbt_im = bb_im.transpose(0, 2, 1)[None, :, None]
    k_tau = jnp.sum(ca_re[:L, :, :, None, :] * bbt_re - ca_im[:L, :, :, None, :] * bbt_im, axis=-1)
    ti = jnp.arange(L)
    t_src = jnp.tile(k_tau.transpose(0, 1, 3, 2).reshape(L, SSM_GROUPS * H, H), (1, 1, SSM_BUNDLE))

    rev_re = pw_re[L - 1 - ti]
    rev_im = pw_im[L - 1 - ti]
    w_re = rev_re[..., None] * bb_re[None] - rev_im[..., None] * bb_im[None]
    w_im = rev_re[..., None] * bb_im[None] + rev_im[..., None] * bb_re[None]
    w_src = jnp.concatenate([w_re, w_im], axis=2).transpose(0, 1, 3, 2)

    e_src = jnp.concatenate([ca_re[1:L + 1], -ca_im[1:L + 1]], axis=3)

    a1 = jnp.concatenate([pw_re[L], pw_re[L]], axis=-1)
    a2 = jnp.concatenate([-pw_im[L], pw_im[L]], axis=-1)
    d_tile = jnp.tile(d_skip.astype(F32).reshape(SSM_GROUPS // SSM_BUNDLE, 1, SSM_BUNDLE * H), (1, 1, L))
    return t_src.astype(BF16), w_src.astype(BF16), e_src.astype(BF16), a1, a2, d_tile


def _bundle_operator(dst, src_ref):
    n_l, n_g, h, x = src_ref.shape
    dst[...] = jnp.zeros(dst.shape, dst.dtype)
    for i in range(n_l):
        for g in range(n_g):
            r0 = (i * n_g + g) * h
            dst[r0:r0 + h, g * x:(g + 1) * x] = src_ref[i, g]


def _chunk_rows(u_ref, n_chunk):
    return jnp.concatenate([u_ref[pl.ds(t, n_chunk, stride=SSM_CHUNK), :] for t in range(SSM_CHUNK)], axis=1)


def _ssm_in_kernel(u_ref, w_ref, v_ref, w_scr):
    @pl.when(pl.program_id(1) == 0)
    def _():
        _bundle_operator(w_scr, w_ref)

    x = _chunk_rows(u_ref, v_ref.shape[0]).astype(BF16)
    v = jnp.dot(x, w_scr[...], preferred_element_type=F32)
    sw = v_ref.shape[2]
    for g in range(v_ref.shape[1]):
        v_ref[:, g, :] = v[:, g * sw:(g + 1) * sw]


def _ssm_scan_kernel(v_ref, a1_ref, a2_ref, o_ref, st):
    @pl.when(pl.program_id(0) == 0)
    def _():
        st[...] = jnp.zeros(st.shape, F32)

    a1 = a1_ref[...][None]
    a2 = a2_ref[...][None]
    n_chunk = v_ref.shape[1]
    half = v_ref.shape[3] // 2

    def body(c, s):
        o_ref[:, pl.ds(c, 1)] = s[:, None].astype(o_ref.dtype)
        v = v_ref[:, pl.ds(c, 1)][:, 0]
        return a1 * s + a2 * pltpu.roll(s, half, axis=2) + v

    st[...] = lax.fori_loop(0, n_chunk, body, st[...])


def _gelu_tanh(x):
    c = math.sqrt(2.0 / math.pi)
    return x * (0.5 * (1.0 + jnp.tanh(c * (x + 0.044715 * (x * x * x)))))


def _ssm_out_kernel(u_ref, t_ref, s_ref, e_ref, d_ref, y_ref, t_scr, et_scr):
    L, lw, h = SSM_CHUNK, t_ref.shape[1], SSM_GROUP

    @pl.when(pl.program_id(1) == 0)
    def _():
        _bundle_operator(et_scr, e_ref)
        same_group = (lax.broadcasted_iota(jnp.int32, (lw, lw), 0) // h
                      == lax.broadcasted_iota(jnp.int32, (lw, lw), 1) // h)
        t_scr[...] = jnp.zeros(t_scr.shape, t_scr.dtype)
        for tau in range(L):
            blk = jnp.where(same_group, t_ref[tau], jnp.zeros((lw, lw), t_ref.dtype))
            for i in range(L - tau):
                j = i + tau
                t_scr[i * lw:(i + 1) * lw, j * lw:(j + 1) * lw] = blk

    n_chunk = s_ref.shape[0]
    xf = _chunk_rows(u_ref, n_chunk)
    sp = jnp.concatenate([s_ref[:, g, :] for g in range(s_ref.shape[1])], axis=1).astype(BF16)
    y = jnp.dot(xf.astype(BF16), t_scr[...], preferred_element_type=F32)
    y = y + lax.dot_general(sp, et_scr[...], (((1,), (1,)), ((), ())), preferred_element_type=F32)
    y = _gelu_tanh(y + d_ref[0] * xf)
    for t in range(L):
        y_ref[pl.ds(t, n_chunk, stride=L), :] = y[:, t * lw:(t + 1) * lw]


def _ssm(u, ops, bt, scan_block=32, chunk_tile=256):
    t_src, w_src, e_src, a1, a2, d_tile = ops
    n, width = u.shape
    L, G, H, P2, nbg = SSM_CHUNK, SSM_GROUPS, SSM_GROUP, 2 * SSM_STATE, SSM_BUNDLE
    nb = G // nbg
    lw = nbg * H
    nc = n // L
    n_c = nc // bt
    ct = min(chunk_tile, nc)
    u_spec = pl.BlockSpec((ct * L, lw), lambda b, i: (i, b))
    src_spec = pl.BlockSpec((L, nbg, H, P2), lambda b, i: (0, b, 0, 0))
    state_spec = pl.BlockSpec((ct, nbg, P2), lambda b, i: (i, b, 0))

    v = pl.pallas_call(
        _ssm_in_kernel,
        grid=(nb, nc // ct),
        in_specs=[u_spec, src_spec],
        out_specs=state_spec,
        out_shape=jax.ShapeDtypeStruct((nc, G, P2), F32),
        scratch_shapes=[pltpu.VMEM((L * lw, nbg * P2), BF16)],
        compiler_params=_params(("parallel", "arbitrary")),
        name="ssm_chunk_state",
    )(u, w_src)

    cb = min(scan_block, n_c)
    s_prev = pl.pallas_call(
        _ssm_scan_kernel,
        grid=(n_c // cb,),
        in_specs=[pl.BlockSpec((bt, cb, G, P2), lambda c: (0, c, 0, 0)),
                  pl.BlockSpec((G, P2), lambda c: (0, 0)),
                  pl.BlockSpec((G, P2), lambda c: (0, 0))],
        out_specs=pl.BlockSpec((bt, cb, G, P2), lambda c: (0, c, 0, 0)),
        out_shape=jax.ShapeDtypeStruct((bt, n_c, G, P2), F32),
        scratch_shapes=[pltpu.VMEM((bt, G, P2), F32)],
        compiler_params=_params(("arbitrary",)),
        name="ssm_scan",
    )(v.reshape(bt, n_c, G, P2), a1, a2)

    return pl.pallas_call(
        _ssm_out_kernel,
        grid=(nb, nc // ct),
        in_specs=[u_spec,
                  pl.BlockSpec((L, lw, lw), lambda b, i: (0, b, 0)),
                  state_spec,
                  src_spec,
                  pl.BlockSpec((1, 1, L * lw), lambda b, i: (b, 0, 0))],
        out_specs=pl.BlockSpec((ct * L, lw), lambda b, i: (i, b)),
        out_shape=jax.ShapeDtypeStruct((n, width), F32),
        scratch_shapes=[pltpu.VMEM((L * lw, L * lw), BF16), pltpu.VMEM((L * lw, nbg * P2), BF16)],
        compiler_params=_params(("parallel", "arbitrary")),
        name="ssm_output",
    )(u, t_src, s_prev.reshape(nc, G, P2), e_src, d_tile)


def _postmix_kernel(a_ref, s_ref, x_ref, wg_ref, bg_ref, gs_ref, woa_ref, wos_ref, gf_ref,
                    wrh_ref, wrl_ref, br_ref, x1_ref, h_ref, lg_ref):
    sf = s_ref[...]
    gate = _sigmoid(jnp.dot(sf.astype(BF16), wg_ref[...], preferred_element_type=F32) + bg_ref[...])
    sn = _rms(sf * gate, gs_ref[...], 1e-6).astype(BF16)
    x1 = (x_ref[...]
          + jnp.dot(a_ref[...], woa_ref[...], preferred_element_type=F32)
          + jnp.dot(sn, wos_ref[...], preferred_element_type=F32))
    x1_ref[...] = x1
    h = _rms(x1, gf_ref[...], 1e-6)
    h_hi = h.astype(BF16)
    h_lo = (h - h_hi.astype(F32)).astype(BF16)
    hf = h_hi.astype(F32)
    for j in range(hf.shape[1] // LANES):
        h_ref[pl.ds(j, hf.shape[0], stride=hf.shape[1] // LANES), :] = hf[:, j * LANES:(j + 1) * LANES]
    lg_ref[...] = (jnp.dot(h_hi, wrh_ref[...], preferred_element_type=F32)
                   + jnp.dot(h_lo, wrh_ref[...], preferred_element_type=F32)
                   + jnp.dot(h_hi, wrl_ref[...], preferred_element_type=F32)
                   + br_ref[...])


def _postmix(a, s, x2d, w_glu, b_glu, g_s, w_o, g_ffn, w_r, b_r, tm=256):
    n, d = x2d.shape
    wa = a.shape[1]
    ws = s.shape[1]
    wr_hi = w_r.astype(BF16)
    wr_lo = (w_r - wr_hi.astype(F32)).astype(BF16)
    row = lambda i: (i, 0)
    fixed = lambda i: (0, 0)
    return pl.pallas_call(
        _postmix_kernel,
        grid=(n // tm,),
        in_specs=[
            pl.BlockSpec((tm, wa), row),
            pl.BlockSpec((tm, ws), row),
            pl.BlockSpec((tm, d), row),
            pl.BlockSpec((ws, ws), fixed),
            pl.BlockSpec((1, ws), fixed),
            pl.BlockSpec((1, ws), fixed),
            pl.BlockSpec((wa, d), fixed),
            pl.BlockSpec((ws, d), fixed),
            pl.BlockSpec((1, d), fixed),
            pl.BlockSpec((d, ROUTER_LANES), fixed),
            pl.BlockSpec((d, ROUTER_LANES), fixed),
            pl.BlockSpec((1, ROUTER_LANES), fixed),
        ],
        out_specs=[pl.BlockSpec((tm, d), row), pl.BlockSpec((tm * (d // LANES), LANES), row),
                   pl.BlockSpec((tm, ROUTER_LANES), row)],
        out_shape=[jax.ShapeDtypeStruct((n, d), F32), jax.ShapeDtypeStruct((n * (d // LANES), LANES), F32),
                   jax.ShapeDtypeStruct((n, ROUTER_LANES), F32)],
        compiler_params=_params(("parallel",)),
        name="postmix",
    )(a, s, x2d, w_glu.astype(BF16), b_glu.reshape(1, ws).astype(F32), g_s.reshape(1, ws).astype(F32),
      w_o[:wa].astype(BF16), w_o[wa:].astype(BF16), g_ffn.reshape(1, d).astype(F32),
      wr_hi, wr_lo, b_r.reshape(1, ROUTER_LANES).astype(F32))


def _route(logits, rows, slab_rows):
    n_tok = logits.shape[0]
    lg = logits[:, :N_EXPERT_GROUPS]
    le = logits[:, N_EXPERT_GROUPS:N_EXPERT_GROUPS + N_EXPERTS].reshape(
        n_tok, N_EXPERT_GROUPS, EXPERTS_PER_GROUP)
    pg = jax.nn.softmax(lg, axis=-1)
    gsel = jnp.argmax(lg, axis=-1).astype(jnp.int32)
    gate_g = jnp.max(pg, axis=-1, keepdims=True)
    sel = gsel[:, None] == jnp.arange(N_EXPERT_GROUPS, dtype=jnp.int32)[None, :]
    le_sel = jnp.sum(jnp.where(sel[:, :, None], le, 0.0), axis=1)
    pe = jax.nn.softmax(le_sel, axis=-1)
    top_p, top_i = lax.top_k(pe, TOP_K)
    w = gate_g * top_p / jnp.sum(top_p, axis=-1, keepdims=True)
    eid = gsel[:, None] * EXPERTS_PER_GROUP + top_i.astype(jnp.int32)

    n_assign = n_tok * TOP_K
    flat_e = eid.reshape(-1)
    counts = jnp.sum((flat_e[None, :] == jnp.arange(N_EXPERTS, dtype=jnp.int32)[:, None])
                     .astype(jnp.int32), axis=1)
    n_blk_e = (counts + rows - 1) // rows
    blk_end = jnp.cumsum(n_blk_e)
    blk_first = blk_end - n_blk_e
    start = jnp.cumsum(counts) - counts
    order = jnp.argsort(flat_e, stable=True).astype(jnp.int32)
    n_blk = (n_assign + rows - 1) // rows + N_EXPERTS
    blk = jnp.arange(n_blk, dtype=jnp.int32)
    blk_e = jnp.minimum(jnp.sum((blk[:, None] >= blk_end[None, :]).astype(jnp.int32), axis=1),
                        N_EXPERTS - 1)
    used = blk < blk_end[-1]
    in_e = (blk - blk_first[blk_e]) * rows
    blk_cnt = jnp.where(used, jnp.clip(counts[blk_e] - in_e, 0, rows), 0).astype(jnp.int32)
    blk_src = (start[blk_e] + in_e).astype(jnp.int32)
    last_e = blk_e[jnp.maximum(blk_end[-1] - 1, 0)]
    blk_e = jnp.where(used, blk_e, last_e)
    row_tok = order // TOP_K
    row_dst = (order % TOP_K) * n_tok + row_tok
    return w, row_tok * slab_rows, row_dst * slab_rows, blk_e, blk_cnt, blk_src


def _expert_kernel(blk_e_ref, blk_cnt_ref, blk_src_ref, tok_ref, dst_ref,
                   h_hbm, w1_ref, w3_ref, w2_ref, y_hbm, xbuf, ybuf, gsem, ssem):
    del blk_e_ref
    b = pl.program_id(0)
    nb = pl.num_programs(0)
    slot = b % 2
    cnt = blk_cnt_ref[b]
    spr = w1_ref.shape[1] // LANES
    rows = xbuf.shape[1] // spr

    def for_rows(n, body):
        n_main = n // ISSUE_UNROLL

        def main(i, c):
            for u in range(ISSUE_UNROLL):
                body(i * ISSUE_UNROLL + u)
            return c

        def tail(r, c):
            body(r)
            return c

        lax.fori_loop(0, n_main, main, 0)
        lax.fori_loop(n_main * ISSUE_UNROLL, n, tail, 0)

    def gather_copy(src, r, sl):
        return pltpu.make_async_copy(h_hbm.at[pl.ds(pl.multiple_of(tok_ref[src + r], spr), spr)],
                                     xbuf.at[sl, pl.ds(pl.multiple_of(r * spr, spr), spr)], gsem.at[sl])

    def scatter_copy(src, r, sl):
        return pltpu.make_async_copy(ybuf.at[sl, pl.ds(pl.multiple_of(r * spr, spr), spr)],
                                     y_hbm.at[pl.ds(pl.multiple_of(dst_ref[src + r], spr), spr)], ssem.at[sl])

    def start_gathers(blk, sl):
        src = blk_src_ref[blk]
        for_rows(blk_cnt_ref[blk], lambda r: gather_copy(src, r, sl).start())

    def wait_gathers(blk, sl):
        src = blk_src_ref[blk]
        for_rows(blk_cnt_ref[blk], lambda r: gather_copy(src, r, sl).wait())

    def start_scatters(blk, sl):
        src = blk_src_ref[blk]
        for_rows(blk_cnt_ref[blk], lambda r: scatter_copy(src, r, sl).start())

    def wait_scatters(blk, sl):
        src = blk_src_ref[blk]
        for_rows(blk_cnt_ref[blk], lambda r: scatter_copy(src, r, sl).wait())

    @pl.when(b == 0)
    def _():
        xbuf[...] = jnp.zeros(xbuf.shape, xbuf.dtype)
        start_gathers(0, 0)

    @pl.when(b + 1 < nb)
    def _():
        start_gathers(b + 1, 1 - slot)

    @pl.when(b >= 2)
    def _():
        wait_scatters(b - 2, slot)

    @pl.when(cnt > 0)
    def _():
        wait_gathers(b, slot)
        x = jnp.concatenate([xbuf[slot, pl.ds(j, rows, stride=spr), :] for j in range(spr)], axis=1).astype(BF16)
        h1 = jnp.dot(x, w1_ref[0].astype(BF16), preferred_element_type=F32)
        h3 = jnp.dot(x, w3_ref[0].astype(BF16), preferred_element_type=F32)
        act = (h1 * _sigmoid(h1) * h3).astype(BF16)
        y = jnp.dot(act, w2_ref[0].astype(BF16), preferred_element_type=F32)
        for j in range(spr):
            ybuf[slot, pl.ds(j, rows, stride=spr), :] = y[:, j * LANES:(j + 1) * LANES]
        start_scatters(b, slot)

    @pl.when(b == nb - 1)
    def _():
        @pl.when(b >= 1)
        def _():
            wait_scatters(b - 1, 1 - slot)
        wait_scatters(b, slot)


def _experts(h, w1, w3, w2, row_tok, row_dst, blk_e, blk_cnt, blk_src, rows):
    d, f = w1.shape[1], w1.shape[2]
    spr = d // LANES
    n_blk = blk_e.shape[0]
    wmap = lambda b, be, bc, bs, rt, rd: (be[b], 0, 0)
    grid_spec = pltpu.PrefetchScalarGridSpec(
        num_scalar_prefetch=5,
        grid=(n_blk,),
        in_specs=[
            pl.BlockSpec(memory_space=pl.ANY),
            pl.BlockSpec((1, d, f), wmap),
            pl.BlockSpec((1, d, f), wmap),
            pl.BlockSpec((1, f, d), wmap),
        ],
        out_specs=pl.BlockSpec(memory_space=pl.ANY),
        scratch_shapes=[
            pltpu.VMEM((2, rows * spr, LANES), F32),
            pltpu.VMEM((2, rows * spr, LANES), F32),
            pltpu.SemaphoreType.DMA((2,)),
            pltpu.SemaphoreType.DMA((2,)),
        ],
    )
    return pl.pallas_call(
        _expert_kernel,
        grid_spec=grid_spec,
        out_shape=jax.ShapeDtypeStruct((h.shape[0] * TOP_K, LANES), F32),
        compiler_params=_params(("arbitrary",), disable_bounds_checks=True),
        name="experts",
    )(blk_e, blk_cnt, blk_src, row_tok, row_dst, h, w1, w3, w2)


def _ple_kernel(x1_ref, y0_ref, y1_ref, w_ref, p_ref, gp_ref, wg_ref, wp_ref, gf_ref, o_ref, *, final):
    w = w_ref[...]
    tm, d = x1_ref.shape
    spr = d // LANES
    y0 = jnp.concatenate([y0_ref[pl.ds(j, tm, stride=spr), :] for j in range(spr)], axis=1)
    y1 = jnp.concatenate([y1_ref[pl.ds(j, tm, stride=spr), :] for j in range(spr)], axis=1)
    x2 = x1_ref[...] + w[:, 0:1] * y0 + w[:, 1:2] * y1
    hn = _rms(x2, gp_ref[...], 1e-6).astype(BF16)
    gate = _sigmoid(jnp.dot(hn, wg_ref[...], preferred_element_type=F32))
    pp = jnp.dot(p_ref[...].astype(BF16), wp_ref[...], preferred_element_type=F32)
    x3 = x2 + gate * pp
    o_ref[...] = _rms(x3, gf_ref[...], 1e-6) if final else x3


def _ple(x1, y2, w, p2d, g_ple, w_gate, w_proj, g_final, final, tm=256):
    n, d = x1.shape
    pd = p2d.shape[1]
    row = lambda i: (i, 0)
    fixed = lambda i: (0, 0)
    return pl.pallas_call(
        functools.partial(_ple_kernel, final=final),
        grid=(n // tm,),
        in_specs=[
            pl.BlockSpec((tm, d), row),
            pl.BlockSpec((tm * (d // LANES), LANES), row),
            pl.BlockSpec((tm * (d // LANES), LANES), lambda i: (n // tm + i, 0)),
            pl.BlockSpec((tm, TOP_K), row),
            pl.BlockSpec((tm, pd), row),
            pl.BlockSpec((1, d), fixed),
            pl.BlockSpec((d, d), fixed),
            pl.BlockSpec((pd, d), fixed),
            pl.BlockSpec((1, d), fixed),
        ],
        out_specs=pl.BlockSpec((tm, d), row),
        out_shape=jax.ShapeDtypeStruct((n, d), F32),
        compiler_params=_params(("parallel",)),
        name="ple_final",
    )(x1, y2, y2, w, p2d, g_ple.reshape(1, d).astype(F32), w_gate.astype(BF16), w_proj.astype(BF16),
      g_final.reshape(1, d).astype(F32))


def kernel(x, p, rel_bias, g_mix, w_in, lam_q1, lam_k1, lam_q2, lam_k2, subln_g, ssm_lam_re, ssm_lam_im, ssm_log_dt, ssm_b_re, ssm_b_im, ssm_c_re, ssm_c_im, ssm_d, w_glu, b_glu, ssm_norm_g, w_o, g_ffn, w_router_g, b_router_g, w_router_e, b_router_e, w1, w3, w2, g_ple, w_ple_gate, w_ple_proj, g_final):
    bt, s_len, d = x.shape
    n = bt * s_len
    depth = g_mix.shape[0]
    attn_w = N_HEADS * DV
    xc = x.reshape(n, d).astype(F32)
    for i in range(depth):
        lam_init = 0.8 - 0.6 * math.exp(-0.3 * i)
        col_scale = jnp.concatenate([jnp.full((attn_w,), LOG2E * DK ** -0.5, F32),
                                     jnp.ones((w_in.shape[2] - attn_w,), F32)])
        w_in_b = (w_in[i].astype(F32) * col_scale[None, :]).astype(BF16)
        z, u = _inproj(xc, g_mix[i].astype(F32), w_in_b)
        z3 = z.reshape(bt, s_len, z.shape[1])

        lam = (jnp.exp(jnp.sum(lam_q1[i].astype(F32) * lam_k1[i].astype(F32)))
               - jnp.exp(jnp.sum(lam_q2[i].astype(F32) * lam_k2[i].astype(F32))) + lam_init)
        a = _attention(z3, rel_bias, lam, subln_g[i], lam_init)

        ops = _ssm_operators(ssm_lam_re[i], ssm_lam_im[i], ssm_log_dt[i], ssm_b_re[i], ssm_b_im[i],
                             ssm_c_re[i], ssm_c_im[i], ssm_d[i])
        s = _ssm(u, ops, bt)

        w_r = jnp.zeros((d, ROUTER_LANES), F32)
        w_r = w_r.at[:, :N_EXPERT_GROUPS].set(w_router_g[i].astype(F32))
        w_r = w_r.at[:, N_EXPERT_GROUPS:N_EXPERT_GROUPS + N_EXPERTS].set(w_router_e[i].astype(F32))
        b_r = jnp.zeros((ROUTER_LANES,), F32)
        b_r = b_r.at[:N_EXPERT_GROUPS].set(b_router_g[i].astype(F32))
        b_r = b_r.at[N_EXPERT_GROUPS:N_EXPERT_GROUPS + N_EXPERTS].set(b_router_e[i].astype(F32))
        x1, h, logits = _postmix(a.reshape(n, attn_w), s, xc, w_glu[i], b_glu[i],
                                 ssm_norm_g[i], w_o[i], g_ffn[i], w_r, b_r)

        gate_w, row_tok, row_dst, blk_e, blk_cnt, blk_src = _route(logits, EXPERT_ROWS, d // LANES)
        y2 = _experts(h, w1[i], w3[i], w2[i],
                      row_tok, row_dst, blk_e, blk_cnt, blk_src, EXPERT_ROWS)
        xc = _ple(x1, y2, gate_w.astype(F32), p[i].reshape(n, -1).astype(F32),
                  g_ple[i], w_ple_gate[i], w_ple_proj[i], g_final, final=(i == depth - 1))
    return xc.reshape(bt, s_len, d)
```

```python
import functools
import math

import jax
import jax.numpy as jnp
from jax import lax
from jax.experimental import pallas as pl
from jax.experimental.pallas import tpu as pltpu

F32 = jnp.float32
BF16 = jnp.bfloat16
HIGHEST = lax.Precision.HIGHEST

N_HEADS = 8
DK = 64
DV = 128
N_BUCKETS = 32
MAX_DISTANCE = 128
SSM_GROUP = 16
SSM_GROUPS = 64
SSM_STATE = 64
N_EXPERT_GROUPS = 4
EXPERTS_PER_GROUP = 8
N_EXPERTS = 32
TOP_K = 2
MASK_VALUE = -1e30
LOG2E = math.log2(math.e)

SSM_CHUNK = 16
SSM_BUNDLE = 8
ATTN_TQ = 512
ONES_ROWS = 16
ATTN_CB = 256
ATTN_TK = 256
EXPERT_ROWS = 256
MXU_WIDTH = 256
LANES = 128
ROUTER_LANES = 128
VMEM_LIMIT = 56 << 20


def _params(semantics, **kw):
    return pltpu.CompilerParams(dimension_semantics=semantics, vmem_limit_bytes=VMEM_LIMIT, **kw)


def _rms(x, g, eps):
    return x * lax.rsqrt(jnp.mean(x * x, axis=-1, keepdims=True) + eps) * g


def _sigmoid(x):
    return 1.0 / (1.0 + jnp.exp(-x))


def _inproj_kernel(x_ref, g_ref, w_ref, z_ref, u_ref, h_scr):
    j = pl.program_id(1)
    last = pl.num_programs(1) - 1

    @pl.when(j == 0)
    def _():
        h_scr[...] = _rms(x_ref[...], g_ref[...], 1e-6).astype(BF16)

    acc = jnp.dot(h_scr[...], w_ref[...], preferred_element_type=F32)

    @pl.when(j < last)
    def _():
        z_ref[...] = acc.astype(z_ref.dtype)

    @pl.when(j == last)
    def _():
        u_ref[...] = acc


def _inproj(x2d, g, w, tm=512, tn=1024):
    n, d = x2d.shape
    pw = w.shape[1]
    n_col = pw // tn
    return pl.pallas_call(
        _inproj_kernel,
        grid=(n // tm, n_col),
        in_specs=[
            pl.BlockSpec((tm, d), lambda i, j: (i, 0)),
            pl.BlockSpec((1, d), lambda i, j: (0, 0)),
            pl.BlockSpec((d, tn), lambda i, j: (0, j)),
        ],
        out_specs=[pl.BlockSpec((tm, tn), lambda i, j: (i, jnp.minimum(j, n_col - 2))),
                   pl.BlockSpec((tm, tn), lambda i, j: (i, 0))],
        out_shape=[jax.ShapeDtypeStruct((n, pw - tn), BF16), jax.ShapeDtypeStruct((n, tn), F32)],
        scratch_shapes=[pltpu.VMEM((tm, d), BF16)],
        compiler_params=_params(("parallel", "arbitrary")),
        name="inproj",
    )(x2d, g.reshape(1, d), w)


def _attn_kernel(lam_ref, q_ref, k_ref, v_ref, bias_ref, g_ref, o_ref, qs, vt, *state,
                 tq, tk, cb, out_scale):
    n_blk = 2 * tq // cb
    m_scr, acc, s_scr = (state[i * n_blk:(i + 1) * n_blk] for i in range(3))
    qi = pl.program_id(2)
    n_sub = tq // tk
    n_kv = v_ref.shape[1] // tk

    @pl.when(qi == 0)
    def _():
        for c in range(n_kv):
            vt[c, 0:DV, :] = v_ref[0, c * tk:(c + 1) * tk, :].astype(F32).T.astype(BF16)
            vt[c, DV:DV + ONES_ROWS, :] = jnp.ones((ONES_ROWS, tk), BF16)

    q = q_ref[0]
    lane = lax.broadcasted_iota(jnp.int32, q.shape, 1)
    zero = jnp.zeros_like(q)
    qs[0:tq, :] = jnp.where(lane < DK, q, zero)
    qs[tq:2 * tq, :] = jnp.where(lane >= DK, q, zero)
    for c in range(n_blk):
        m_scr[c][...] = jnp.full(m_scr[c].shape, MASK_VALUE, F32)
        acc[c][...] = jnp.zeros(acc[c].shape, F32)

    all_blocks = tuple(range(n_blk))

    def visible(r):
        return tuple(c for c in all_blocks if (c * cb) % tq + cb > (r - 1) * tk)

    def scores(j, blocks):
        kj = k_ref[0, pl.ds(pl.multiple_of(j * tk, tk), tk), :]
        return [lax.dot_general(kj, qs[c * cb:(c + 1) * cb, :], (((1,), (1,)), ((), ())),
                                preferred_element_type=F32) for c in blocks]

    def consume(j, bias, blocks, next_blocks):
        nxt = scores(j + 1, next_blocks) if next_blocks else None
        vtj = vt[j]
        alphas, ps = [], []
        for c in blocks:
            s = s_scr[c][...]
            if bias is not None:
                qb = (c * cb) % tq
                s = s + bias[:, qb:qb + cb]
            m_old = m_scr[c][...]
            m_new = jnp.maximum(m_old, jnp.max(s, axis=0, keepdims=True))
            alpha = jnp.exp2(m_old - m_new)
            p = jnp.exp2(s - m_new)
            m_scr[c][...] = m_new
            alphas.append(alpha)
            ps.append(p.astype(BF16))
        pvs = [jnp.dot(vtj, p, preferred_element_type=F32) for p in ps]
        for c, alpha, pv in zip(blocks, alphas, pvs):
            acc[c][...] = alpha * acc[c][...] + pv
        if next_blocks:
            for c, sc in zip(next_blocks, nxt):
                s_scr[c][...] = sc

    first = qi * n_sub - 1
    for c, sc in zip(all_blocks, scores(0, all_blocks)):
        s_scr[c][...] = sc

    def far_pair(i, carry):
        consume(2 * i, None, all_blocks, all_blocks)
        consume(2 * i + 1, None, all_blocks, all_blocks)
        return carry

    lax.fori_loop(0, jnp.maximum(first, 0) // 2, far_pair, 0)

    @pl.when(qi >= 1)
    def _():
        consume(first - 1, None, all_blocks, all_blocks)
        consume(first, bias_ref[0, 0], all_blocks, all_blocks)

    for r in range(1, n_sub + 1):
        consume(first + r, bias_ref[0, r], visible(r), visible(r + 1) if r < n_sub else None)

    lam = lam_ref[0, 0]
    half = n_blk // 2
    for b in range(half):
        a1, a2 = acc[b], acc[half + b]
        ot = (a1[0:DV, :] / a1[DV:DV + 1, :] - lam * (a2[0:DV, :] / a2[DV:DV + 1, :]))
        ot = ot * lax.rsqrt(jnp.mean(ot * ot, axis=0, keepdims=True) + 1e-5)
        o_ref[0, b * cb:(b + 1) * cb, :] = (ot.T * (g_ref[...] * out_scale)).astype(o_ref.dtype)


def _t5_bucket(n):
    n = jnp.maximum(n, 0)
    max_exact = N_BUCKETS // 2
    nf = jnp.maximum(n, 1).astype(F32)
    large = max_exact + (jnp.log(nf / max_exact) / math.log(MAX_DISTANCE / max_exact)
                         * (N_BUCKETS - max_exact)).astype(jnp.int32)
    large = jnp.minimum(large, N_BUCKETS - 1)
    return jnp.where(n < max_exact, n, large)


def _attn_bias_tiles(rel_bias, tq, tk):
    assert tk >= MAX_DISTANCE and tq % (2 * tk) == 0
    table = rel_bias.astype(F32)
    rel_table = (table - table[N_BUCKETS - 1][None, :]) * LOG2E
    r = jnp.arange(tk, dtype=jnp.int32)[:, None]
    c = jnp.arange(tq, dtype=jnp.int32)[None, :]
    tiles = []
    for t in range(tq // tk + 1):
        dist = c - (r + (t - 1) * tk)
        onehot = (_t5_bucket(dist)[:, :, None] == jnp.arange(N_BUCKETS, dtype=jnp.int32)).astype(F32)
        b = jnp.einsum('rcn,nh->hrc', onehot, rel_table, precision=HIGHEST)
        tiles.append(jnp.where((dist >= 0)[None], b, MASK_VALUE))
    return jnp.stack(tiles, axis=1)


def _attention(z3, rel_bias, lam, subln_g, lam_init, tq=ATTN_TQ, tk=ATTN_TK, cb=ATTN_CB):
    bt, s_len, _ = z3.shape
    bias = _attn_bias_tiles(rel_bias, tq, tk)
    n_sp = bias.shape[1]
    kern = functools.partial(_attn_kernel, tq=tq, tk=tk, cb=cb, out_scale=1.0 - lam_init)
    n_blk = 2 * tq // cb
    return pl.pallas_call(
        kern,
        grid=(bt, N_HEADS, s_len // tq),
        in_specs=[
            pl.BlockSpec(memory_space=pltpu.SMEM),
            pl.BlockSpec((1, tq, 2 * DK), lambda b, h, i: (b, i, h)),
            pl.BlockSpec((1, s_len, 2 * DK), lambda b, h, i: (b, 0, N_HEADS + h)),
            pl.BlockSpec((1, s_len, DV), lambda b, h, i: (b, 0, 2 * N_HEADS + h)),
            pl.BlockSpec((1, n_sp, tk, tq), lambda b, h, i: (h, 0, 0, 0)),
            pl.BlockSpec((1, DV), lambda b, h, i: (0, 0)),
        ],
        out_specs=pl.BlockSpec((1, tq, DV), lambda b, h, i: (b, i, h)),
        out_shape=jax.ShapeDtypeStruct((bt, s_len, N_HEADS * DV), BF16),
        scratch_shapes=[
            pltpu.VMEM((2 * tq, 2 * DK), BF16),
            pltpu.VMEM((s_len // tk, DV + ONES_ROWS, tk), BF16),
        ] + [pltpu.VMEM((1, cb), F32)] * n_blk + [pltpu.VMEM((DV + ONES_ROWS, cb), F32)] * n_blk
        + [pltpu.VMEM((tk, cb), F32)] * n_blk,
        compiler_params=_params(("parallel", "parallel", "arbitrary")),
        name="diff_attention",
    )(lam.reshape(1, 1).astype(F32), z3, z3, z3, bias, subln_g.reshape(1, DV).astype(F32))


def _ssm_operators(lam_re, lam_im, log_dt, b_re, b_im, c_re, c_im, d_skip):
    L, H, P = SSM_CHUNK, SSM_GROUP, SSM_STATE
    lre = lam_re.astype(F32)
    lim = lam_im.astype(F32)
    dt = jnp.exp(log_dt.astype(F32))[:, None]
    mag = jnp.exp(lre * dt)
    ab_re = mag * jnp.cos(lim * dt)
    ab_im = mag * jnp.sin(lim * dt)
    den = lre * lre + lim * lim
    nr, ni = ab_re - 1.0, ab_im
    cr = ((nr * lre + ni * lim) / den)[..., None]
    ci = ((ni * lre - nr * lim) / den)[..., None]
    bre = b_re.astype(F32)
    bim = b_im.astype(F32)
    bb_re = cr * bre - ci * bim
    bb_im = cr * bim + ci * bre
    cre = c_re.astype(F32)
    cim = c_im.astype(F32)

    tau = jnp.arange(L + 1, dtype=F32)[:, None, None]
    pw_mag = jnp.exp(tau * (lre * dt)[None])
    pw_re = pw_mag * jnp.cos(tau * (lim * dt)[None])
    pw_im = pw_mag * jnp.sin(tau * (lim * dt)[None])

    ca_re = cre[None] * pw_re[:, :, None, :] - cim[None] * pw_im[:, :, None, :]
    ca_im = cre[None] * pw_im[:, :, None, :] + cim[None] * pw_re[:, :, None, :]
    bbt_re = bb_re.transpose(0, 2, 1)[None, :, None]
    bbt_im = bb_im.transpose(0, 2, 1)[None, :, None]
    k_tau = jnp.sum(ca_re[:L, :, :, None, :] * bbt_re - ca_im[:L, :, :, None, :] * bbt_im, axis=-1)
    ti = jnp.arange(L)
    t_src = jnp.tile(k_tau.transpose(0, 1, 3, 2).reshape(L, SSM_GROUPS * H, H), (1, 1, SSM_BUNDLE))

    rev_re = pw_re[L - 1 - ti]
    rev_im = pw_im[L - 1 - ti]
    w_re = rev_re[..., None] * bb_re[None] - rev_im[..., None] * bb_im[None]
    w_im = rev_re[..., None] * bb_im[None] + rev_im[..., None] * bb_re[None]
    w_src = jnp.concatenate([w_re, w_im], axis=2).transpose(0, 1, 3, 2)

    e_src = jnp.concatenate([ca_re[1:L + 1], -ca_im[1:L + 1]], axis=3)

    a1 = jnp.concatenate([pw_re[L], pw_re[L]], axis=-1)
    a2 = jnp.concatenate([-pw_im[L], pw_im[L]], axis=-1)
    d_tile = jnp.tile(d_skip.astype(F32).reshape(SSM_GROUPS // SSM_BUNDLE, 1, SSM_BUNDLE * H), (1, 1, L))
    return t_src.astype(BF16), w_src.astype(BF16), e_src.astype(BF16), a1, a2, d_tile


def _bundle_operator(dst, src_ref):
    n_l, n_g, h, x = src_ref.shape
    dst[...] = jnp.zeros(dst.shape, dst.dtype)
    for i in range(n_l):
        for g in range(n_g):
            r0 = (i * n_g + g) * h
            dst[r0:r0 + h, g * x:(g + 1) * x] = src_ref[i, g]


def _chunk_rows(u_ref, n_chunk):
    return jnp.concatenate([u_ref[pl.ds(t, n_chunk, stride=SSM_CHUNK), :] for t in range(SSM_CHUNK)], axis=1)


def _ssm_in_kernel(u_ref, w_ref, v_ref, w_scr):
    @pl.when(pl.program_id(1) == 0)
    def _():
        _bundle_operator(w_scr, w_ref)

    x = _chunk_rows(u_ref, v_ref.shape[0]).astype(BF16)
    v = jnp.dot(x, w_scr[...], preferred_element_type=F32)
    sw = v_ref.shape[2]
    for g in range(v_ref.shape[1]):
        v_ref[:, g, :] = v[:, g * sw:(g + 1) * sw]


def _ssm_scan_kernel(v_ref, a1_ref, a2_ref, o_ref, st):
    @pl.when(pl.program_id(0) == 0)
    def _():
        st[...] = jnp.zeros(st.shape, F32)

    a1 = a1_ref[...][None]
    a2 = a2_ref[...][None]
    n_chunk = v_ref.shape[1]
    half = v_ref.shape[3] // 2

    def body(c, s):
        o_ref[:, pl.ds(c, 1)] = s[:, None].astype(o_ref.dtype)
        v = v_ref[:, pl.ds(c, 1)][:, 0]
        return a1 * s + a2 * pltpu.roll(s, half, axis=2) + v

    st[...] = lax.fori_loop(0, n_chunk, body, st[...])


def _gelu_tanh(x):
    c = math.sqrt(2.0 / math.pi)
    return x * (0.5 * (1.0 + jnp.tanh(c * (x + 0.044715 * (x * x * x)))))


def _ssm_out_kernel(u_ref, t_ref, s_ref, e_ref, d_ref, y_ref, t_scr, et_scr):
    L, lw, h = SSM_CHUNK, t_ref.shape[1], SSM_GROUP

    @pl.when(pl.program_id(1) == 0)
    def _():
        _bundle_operator(et_scr, e_ref)
        same_group = (lax.broadcasted_iota(jnp.int32, (lw, lw), 0) // h
                      == lax.broadcasted_iota(jnp.int32, (lw, lw), 1) // h)
        t_scr[...] = jnp.zeros(t_scr.shape, t_scr.dtype)
        for tau in range(L):
            blk = jnp.where(same_group, t_ref[tau], jnp.zeros((lw, lw), t_ref.dtype))
            for i in range(L - tau):
                j = i + tau
                t_scr[i * lw:(i + 1) * lw, j * lw:(j + 1) * lw] = blk

    n_chunk = s_ref.shape[0]
    xf = _chunk_rows(u_ref, n_chunk)
    sp = jnp.concatenate([s_ref[:, g, :] for g in range(s_ref.shape[1])], axis=1).astype(BF16)
    y = jnp.dot(xf.astype(BF16), t_scr[...], preferred_element_type=F32)
    y = y + lax.dot_general(sp, et_scr[...], (((1,), (1,)), ((), ())), preferred_element_type=F32)
    y = _gelu_tanh(y + d_ref[0] * xf)
    for t in range(L):
        y_ref[pl.ds(t, n_chunk, stride=L), :] = y[:, t * lw:(t + 1) * lw]


def _ssm(u, ops, bt, scan_block=32, chunk_tile=256):
    t_src, w_src, e_src, a1, a2, d_tile = ops
    n, width = u.shape
    L, G, H, P2, nbg = SSM_CHUNK, SSM_GROUPS, SSM_GROUP, 2 * SSM_STATE, SSM_BUNDLE
    nb = G // nbg
    lw = nbg * H
    nc = n // L
    n_c = nc // bt
    ct = min(chunk_tile, nc)
    u_spec = pl.BlockSpec((ct * L, lw), lambda b, i: (i, b))
    src_spec = pl.BlockSpec((L, nbg, H, P2), lambda b, i: (0, b, 0, 0))
    state_spec = pl.BlockSpec((ct, nbg, P2), lambda b, i: (i, b, 0))

    v = pl.pallas_call(
        _ssm_in_kernel,
        grid=(nb, nc // ct),
        in_specs=[u_spec, src_spec],
        out_specs=state_spec,
        out_shape=jax.ShapeDtypeStruct((nc, G, P2), F32),
        scratch_shapes=[pltpu.VMEM((L * lw, nbg * P2), BF16)],
        compiler_params=_params(("parallel", "arbitrary")),
        name="ssm_chunk_state",
    )(u, w_src)

    cb = min(scan_block, n_c)
    s_prev = pl.pallas_call(
        _ssm_scan_kernel,
        grid=(n_c // cb,),
        in_specs=[pl.BlockSpec((bt, cb, G, P2), lambda c: (0, c, 0, 0)),
                  pl.BlockSpec((G, P2), lambda c: (0, 0)),
                  pl.BlockSpec((G, P2), lambda c: (0, 0))],
        out_specs=pl.BlockSpec((bt, cb, G, P2), lambda c: (0, c, 0, 0)),
        out_shape=jax.ShapeDtypeStruct((bt, n_c, G, P2), F32),
        scratch_shapes=[pltpu.VMEM((bt, G, P2), F32)],
        compiler_params=_params(("arbitrary",)),
        name="ssm_scan",
    )(v.reshape(bt, n_c, G, P2), a1, a2)

    return pl.pallas_call(
        _ssm_out_kernel,
        grid=(nb, nc // ct),
        in_specs=[u_spec,
                  pl.BlockSpec((L, lw, lw), lambda b, i: (0, b, 0)),
                  state_spec,
                  src_spec,
                  pl.BlockSpec((1, 1, L * lw), lambda b, i: (b, 0, 0))],
        out_specs=pl.BlockSpec((ct * L, lw), lambda b, i: (i, b)),
        out_shape=jax.ShapeDtypeStruct((n, width), F32),
        scratch_shapes=[pltpu.VMEM((L * lw, L * lw), BF16), pltpu.VMEM((L * lw, nbg * P2), BF16)],
        compiler_params=_params(("parallel", "arbitrary")),
        name="ssm_output",
    )(u, t_src, s_prev.reshape(nc, G, P2), e_src, d_tile)


def _postmix_kernel(a_ref, s_ref, x_ref, wg_ref, bg_ref, gs_ref, woa_ref, wos_ref, gf_ref,
                    wrh_ref, wrl_ref, br_ref, x1_ref, h_ref, lg_ref):
    sf = s_ref[...]
    gate = _sigmoid(jnp.dot(sf.astype(BF16), wg_ref[...], preferred_element_type=F32) + bg_ref[...])
    sn = _rms(sf * gate, gs_ref[...], 1e-6).astype(BF16)
    x1 = (x_ref[...]
          + jnp.dot(a_ref[...], woa_ref[...], preferred_element_type=F32)
          + jnp.dot(sn, wos_ref[...], preferred_element_type=F32))
    x1_ref[...] = x1
    h = _rms(x1, gf_ref[...], 1e-6)
    h_hi = h.astype(BF16)
    h_lo = (h - h_hi.astype(F32)).astype(BF16)
    hf = h_hi.astype(F32)
    for j in range(hf.shape[1] // LANES):
        h_ref[pl.ds(j, hf.shape[0], stride=hf.shape[1] // LANES), :] = hf[:, j * LANES:(j + 1) * LANES]
    lg_ref[...] = (jnp.dot(h_hi, wrh_ref[...], preferred_element_type=F32)
                   + jnp.dot(h_lo, wrh_ref[...], preferred_element_type=F32)
                   + jnp.dot(h_hi, wrl_ref[...], preferred_element_type=F32)
                   + br_ref[...])


def _postmix(a, s, x2d, w_glu, b_glu, g_s, w_o, g_ffn, w_r, b_r, tm=256):
    n, d = x2d.shape
    wa = a.shape[1]
    ws = s.shape[1]
    wr_hi = w_r.astype(BF16)
    wr_lo = (w_r - wr_hi.astype(F32)).astype(BF16)
    row = lambda i: (i, 0)
    fixed = lambda i: (0, 0)
    return pl.pallas_call(
        _postmix_kernel,
        grid=(n // tm,),
        in_specs=[
            pl.BlockSpec((tm, wa), row),
            pl.BlockSpec((tm, ws), row),
            pl.BlockSpec((tm, d), row),
            pl.BlockSpec((ws, ws), fixed),
            pl.BlockSpec((1, ws), fixed),
            pl.BlockSpec((1, ws), fixed),
            pl.BlockSpec((wa, d), fixed),
            pl.BlockSpec((ws, d), fixed),
            pl.BlockSpec((1, d), fixed),
            pl.BlockSpec((d, ROUTER_LANES), fixed),
            pl.BlockSpec((d, ROUTER_LANES), fixed),
            pl.BlockSpec((1, ROUTER_LANES), fixed),
        ],
        out_specs=[pl.BlockSpec((tm, d), row), pl.BlockSpec((tm * (d // LANES), LANES), row),
                   pl.BlockSpec((tm, ROUTER_LANES), row)],
        out_shape=[jax.ShapeDtypeStruct((n, d), F32), jax.ShapeDtypeStruct((n * (d // LANES), LANES), F32),
                   jax.ShapeDtypeStruct((n, ROUTER_LANES), F32)],
        compiler_params=_params(("parallel",)),
        name="postmix",
    )(a, s, x2d, w_glu.astype(BF16), b_glu.reshape(1, ws).astype(F32), g_s.reshape(1, ws).astype(F32),
      w_o[:wa].astype(BF16), w_o[wa:].astype(BF16), g_ffn.reshape(1, d).astype(F32),
      wr_hi, wr_lo, b_r.reshape(1, ROUTER_LANES).astype(F32))


def _route(logits, rows, slab_rows):
    n_tok = logits.shape[0]
    lg = logits[:, :N_EXPERT_GROUPS]
    le = logits[:, N_EXPERT_GROUPS:N_EXPERT_GROUPS + N_EXPERTS].reshape(
        n_tok, N_EXPERT_GROUPS, EXPERTS_PER_GROUP)
    pg = jax.nn.softmax(lg, axis=-1)
    gsel = jnp.argmax(lg, axis=-1).astype(jnp.int32)
    gate_g = jnp.max(pg, axis=-1, keepdims=True)
    sel = gsel[:, None] == jnp.arange(N_EXPERT_GROUPS, dtype=jnp.int32)[None, :]
    le_sel = jnp.sum(jnp.where(sel[:, :, None], le, 0.0), axis=1)
    pe = jax.nn.softmax(le_sel, axis=-1)
    top_p, top_i = lax.top_k(pe, TOP_K)
    w = gate_g * top_p / jnp.sum(top_p, axis=-1, keepdims=True)
    eid = gsel[:, None] * EXPERTS_PER_GROUP + top_i.astype(jnp.int32)

    n_assign = n_tok * TOP_K
    flat_e = eid.reshape(-1)
    counts = jnp.sum((flat_e[None, :] == jnp.arange(N_EXPERTS, dtype=jnp.int32)[:, None])
                     .astype(jnp.int32), axis=1)
    n_blk_e = (counts + rows - 1) // rows
    blk_end = jnp.cumsum(n_blk_e)
    blk_first = blk_end - n_blk_e
    start = jnp.cumsum(counts) - counts
    order = jnp.argsort(flat_e, stable=True).astype(jnp.int32)
    n_blk = (n_assign + rows - 1) // rows + N_EXPERTS
    blk = jnp.arange(n_blk, dtype=jnp.int32)
    blk_e = jnp.minimum(jnp.sum((blk[:, None] >= blk_end[None, :]).astype(jnp.int32), axis=1),
                        N_EXPERTS - 1)
    used = blk < blk_end[-1]
    experts = jnp.arange(N_EXPERTS, dtype=jnp.int32)
    is_e = blk_e[:, None] == experts[None, :]

    def per_block(table):
        return jnp.sum(jnp.where(is_e, table[None, :], 0), axis=1)

    in_e = (blk - per_block(blk_first)) * rows
    blk_cnt = jnp.where(used, jnp.clip(per_block(counts) - in_e, 0, rows), 0).astype(jnp.int32)
    blk_src = jnp.where(used, per_block(start) + in_e, 0).astype(jnp.int32)
    last_e = jnp.max(jnp.where(counts > 0, experts, 0))
    blk_e = jnp.where(used, blk_e, last_e)
    row_tok = order // TOP_K
    row_dst = (order % TOP_K) * n_tok + row_tok
    pad = jnp.zeros((rows,), jnp.int32)
    return (w, jnp.concatenate([row_tok * slab_rows, pad]), jnp.concatenate([row_dst * slab_rows, pad]),
            blk_e, blk_cnt, blk_src)


def _expert_kernel(blk_e_ref, blk_cnt_ref, blk_src_ref, tok_ref, dst_ref,
                   h_hbm, w1_ref, w3_ref, w2_ref, y_hbm, xbuf, ybuf, gsem, ssem):
    del blk_e_ref
    b = pl.program_id(0)
    slot = b % 2
    spr = w1_ref.shape[1] // LANES
    rows = xbuf.shape[1] // spr
    dump = y_hbm.shape[0] - rows * spr
    cnt = blk_cnt_ref[b]
    prev = jnp.maximum(b - 1, 0)
    prev_cnt = jnp.where(b >= 1, blk_cnt_ref[prev], 0)

    def gather_copy(src, r, sl):
        return pltpu.make_async_copy(h_hbm.at[pl.ds(pl.multiple_of(tok_ref[src + r], spr), spr)],
                                     xbuf.at[sl, pl.ds(r * spr, spr)], gsem.at[sl])

    def scatter_copy(src, n_valid, r, sl):
        dst = jnp.where(r < n_valid, dst_ref[src + r], dump + r * spr)
        return pltpu.make_async_copy(ybuf.at[sl, pl.ds(r * spr, spr)],
                                     y_hbm.at[pl.ds(pl.multiple_of(dst, spr), spr)], ssem.at[sl])

    def start_gathers(blk, sl, lo=0, hi=None):
        src = blk_src_ref[blk]
        for r in range(lo, rows if hi is None else hi):
            gather_copy(src, r, sl).start()

    def wait_gathers(sl):
        for r in range(rows):
            gather_copy(0, r, sl).wait()

    def start_scatters(blk, n_valid, sl, lo=0, hi=None):
        src = blk_src_ref[blk]
        for r in range(lo, rows if hi is None else hi):
            scatter_copy(src, n_valid, r, sl).start()

    def wait_scatters(sl):
        for r in range(rows):
            scatter_copy(0, 0, r, sl).wait()

    @pl.when(b == 0)
    def _():
        ybuf[...] = jnp.zeros(ybuf.shape, ybuf.dtype)
        start_gathers(0, 0)

    @pl.when(prev_cnt > 0)
    def _():
        wait_scatters(slot)

    @pl.when(cnt > 0)
    def _():
        wait_gathers(slot)
        x = jnp.concatenate([xbuf[slot, pl.ds(j, rows, stride=spr), :] for j in range(spr)], axis=1).astype(BF16)
        f = w1_ref.shape[2]
        n_f = f // MXU_WIDTH
        n_part = 3 * n_f
        bounds = [rows * i // n_part for i in range(n_part + 1)]
        part = iter(range(n_part))

        def issue_part():
            i = next(part)
            start_gathers(b + 1, 1 - slot, bounds[i], bounds[i + 1])
            start_scatters(prev, prev_cnt, 1 - slot, bounds[i], bounds[i + 1])

        acts = []
        for c in range(n_f):
            cols = slice(c * MXU_WIDTH, (c + 1) * MXU_WIDTH)
            issue_part()
            h1 = jnp.dot(x, w1_ref[0, :, cols].astype(BF16), preferred_element_type=F32)
            issue_part()
            h3 = jnp.dot(x, w3_ref[0, :, cols].astype(BF16), preferred_element_type=F32)
            acts.append((h1 * _sigmoid(h1) * h3).astype(BF16))
        y = None
        for c in range(n_f):
            issue_part()
            yc = jnp.dot(acts[c], w2_ref[0, c * MXU_WIDTH:(c + 1) * MXU_WIDTH, :].astype(BF16),
                         preferred_element_type=F32)
            y = yc if y is None else y + yc
        for j in range(spr):
            ybuf[slot, pl.ds(j, rows, stride=spr), :] = y[:, j * LANES:(j + 1) * LANES]

    @pl.when((cnt == 0) & (prev_cnt > 0))
    def _():
        wait_gathers(slot)
        start_scatters(prev, prev_cnt, 1 - slot)
        wait_scatters(1 - slot)


def _experts(h, w1, w3, w2, row_tok, row_dst, blk_e, blk_cnt, blk_src, rows):
    d, f = w1.shape[1], w1.shape[2]
    spr = d // LANES
    n_blk = blk_e.shape[0]
    wmap = lambda b, be, bc, bs, rt, rd: (be[b], 0, 0)
    grid_spec = pltpu.PrefetchScalarGridSpec(
        num_scalar_prefetch=5,
        grid=(n_blk,),
        in_specs=[
            pl.BlockSpec(memory_space=pl.ANY),
            pl.BlockSpec((1, d, f), wmap),
            pl.BlockSpec((1, d, f), wmap),
            pl.BlockSpec((1, f, d), wmap),
        ],
        out_specs=pl.BlockSpec(memory_space=pl.ANY),
        scratch_shapes=[
            pltpu.VMEM((2, rows * spr, LANES), F32),
            pltpu.VMEM((2, rows * spr, LANES), F32),
            pltpu.SemaphoreType.DMA((2,)),
            pltpu.SemaphoreType.DMA((2,)),
        ],
    )
    return pl.pallas_call(
        _expert_kernel,
        grid_spec=grid_spec,
        out_shape=jax.ShapeDtypeStruct((h.shape[0] * TOP_K + rows * spr, LANES), F32),
        compiler_params=_params(("arbitrary",), disable_bounds_checks=True),
        name="experts",
    )(blk_e, blk_cnt, blk_src, row_tok, row_dst, h, w1, w3, w2)


def _ple_kernel(x1_ref, y0_ref, y1_ref, w_ref, p_ref, gp_ref, wg_ref, wp_ref, gf_ref, o_ref, *, final):
    w = w_ref[...]
    tm, d = x1_ref.shape
    spr = d // LANES
    y0 = jnp.concatenate([y0_ref[pl.ds(j, tm, stride=spr), :] for j in range(spr)], axis=1)
    y1 = jnp.concatenate([y1_ref[pl.ds(j, tm, stride=spr), :] for j in range(spr)], axis=1)
    x2 = x1_ref[...] + w[:, 0:1] * y0 + w[:, 1:2] * y1
    hn = _rms(x2, gp_ref[...], 1e-6).astype(BF16)
    gate = _sigmoid(jnp.dot(hn, wg_ref[...], preferred_element_type=F32))
    pp = jnp.dot(p_ref[...].astype(BF16), wp_ref[...], preferred_element_type=F32)
    x3 = x2 + gate * pp
    o_ref[...] = _rms(x3, gf_ref[...], 1e-6) if final else x3


def _ple(x1, y2, w, p2d, g_ple, w_gate, w_proj, g_final, final, tm=256):
    n, d = x1.shape
    pd = p2d.shape[1]
    row = lambda i: (i, 0)
    fixed = lambda i: (0, 0)
    return pl.pallas_call(
        functools.partial(_ple_kernel, final=final),
        grid=(n // tm,),
        in_specs=[
            pl.BlockSpec((tm, d), row),
            pl.BlockSpec((tm * (d // LANES), LANES), row),
            pl.BlockSpec((tm * (d // LANES), LANES), lambda i: (n // tm + i, 0)),
            pl.BlockSpec((tm, TOP_K), row),
            pl.BlockSpec((tm, pd), row),
            pl.BlockSpec((1, d), fixed),
            pl.BlockSpec((d, d), fixed),
            pl.BlockSpec((pd, d), fixed),
            pl.BlockSpec((1, d), fixed),
        ],
        out_specs=pl.BlockSpec((tm, d), row),
        out_shape=jax.ShapeDtypeStruct((n, d), F32),
        compiler_params=_params(("parallel",)),
        name="ple_final",
    )(x1, y2, y2, w, p2d, g_ple.reshape(1, d).astype(F32), w_gate.astype(BF16), w_proj.astype(BF16),
      g_final.reshape(1, d).astype(F32))


def kernel(x, p, rel_bias, g_mix, w_in, lam_q1, lam_k1, lam_q2, lam_k2, subln_g, ssm_lam_re, ssm_lam_im, ssm_log_dt, ssm_b_re, ssm_b_im, ssm_c_re, ssm_c_im, ssm_d, w_glu, b_glu, ssm_norm_g, w_o, g_ffn, w_router_g, b_router_g, w_router_e, b_router_e, w1, w3, w2, g_ple, w_ple_gate, w_ple_proj, g_final):
    bt, s_len, d = x.shape
    n = bt * s_len
    depth = g_mix.shape[0]
    attn_w = N_HEADS * DV
    xc = x.reshape(n, d).astype(F32)
    for i in range(depth):
        lam_init = 0.8 - 0.6 * math.exp(-0.3 * i)
        col_scale = jnp.concatenate([jnp.full((attn_w,), LOG2E * DK ** -0.5, F32),
                                     jnp.ones((w_in.shape[2] - attn_w,), F32)])
        w_in_b = (w_in[i].astype(F32) * col_scale[None, :]).astype(BF16)
        z, u = _inproj(xc, g_mix[i].astype(F32), w_in_b)
        z3 = z.reshape(bt, s_len, z.shape[1])

        lam = (jnp.exp(jnp.sum(lam_q1[i].astype(F32) * lam_k1[i].astype(F32)))
               - jnp.exp(jnp.sum(lam_q2[i].astype(F32) * lam_k2[i].astype(F32))) + lam_init)
        a = _attention(z3, rel_bias, lam, subln_g[i], lam_init)

        ops = _ssm_operators(ssm_lam_re[i], ssm_lam_im[i], ssm_log_dt[i], ssm_b_re[i], ssm_b_im[i],
                             ssm_c_re[i], ssm_c_im[i], ssm_d[i])
        s = _ssm(u, ops, bt)

        w_r = jnp.zeros((d, ROUTER_LANES), F32)
        w_r = w_r.at[:, :N_EXPERT_GROUPS].set(w_router_g[i].astype(F32))
        w_r = w_r.at[:, N_EXPERT_GROUPS:N_EXPERT_GROUPS + N_EXPERTS].set(w_router_e[i].astype(F32))
        b_r = jnp.zeros((ROUTER_LANES,), F32)
        b_r = b_r.at[:N_EXPERT_GROUPS].set(b_router_g[i].astype(F32))
        b_r = b_r.at[N_EXPERT_GROUPS:N_EXPERT_GROUPS + N_EXPERTS].set(b_router_e[i].astype(F32))
        x1, h, logits = _postmix(a.reshape(n, attn_w), s, xc, w_glu[i], b_glu[i],
                                 ssm_norm_g[i], w_o[i], g_ffn[i], w_r, b_r)

        gate_w, row_tok, row_dst, blk_e, blk_cnt, blk_src = _route(logits, EXPERT_ROWS, d // LANES)
        y2 = _experts(h, w1[i], w3[i], w2[i],
                      row_tok, row_dst, blk_e, blk_cnt, blk_src, EXPERT_ROWS)
        xc = _ple(x1, y2, gate_w.astype(F32), p[i].reshape(n, -1).astype(F32),
                  g_ple[i], w_ple_gate[i], w_ple_proj[i], g_final, final=(i == depth - 1))
    return xc.reshape(bt, s_len, d)
```

```python
import functools
import math

import jax
import jax.numpy as jnp
from jax import lax
from jax.experimental import pallas as pl
from jax.experimental.pallas import tpu as pltpu

F32 = jnp.float32
BF16 = jnp.bfloat16
HIGHEST = lax.Precision.HIGHEST

N_HEADS = 8
DK = 64
DV = 128
N_BUCKETS = 32
MAX_DISTANCE = 128
SSM_GROUP = 16
SSM_GROUPS = 64
SSM_STATE = 64
N_EXPERT_GROUPS = 4
EXPERTS_PER_GROUP = 8
N_EXPERTS = 32
TOP_K = 2
MASK_VALUE = -1e30
LOG2E = math.log2(math.e)

SSM_CHUNK = 16
SSM_BUNDLE = 8
ATTN_TQ = 512
ONES_ROWS = 16
ATTN_CB = 256
ATTN_TK = 256
EXPERT_ROWS = 256
MXU_WIDTH = 256
LANES = 128
SLAB_COLS = 2 * LANES
ROUTER_LANES = 128
VMEM_LIMIT = 56 << 20


def _params(semantics, **kw):
    return pltpu.CompilerParams(dimension_semantics=semantics, vmem_limit_bytes=VMEM_LIMIT, **kw)


def _rms(x, g, eps):
    return x * lax.rsqrt(jnp.mean(x * x, axis=-1, keepdims=True) + eps) * g


def _sigmoid(x):
    return 1.0 / (1.0 + jnp.exp(-x))


def _store_row_slabs(ref, row0, x):
    n, d = x.shape
    spr = d // (2 * LANES)
    bits = lax.bitcast_convert_type(x.astype(BF16).astype(F32), jnp.uint32)
    for j in range(spr):
        lo = lax.shift_right_logical(bits[:, j * LANES:(j + 1) * LANES], jnp.uint32(16))
        hi = bits[:, d // 2 + j * LANES:d // 2 + (j + 1) * LANES] & jnp.uint32(0xFFFF0000)
        ref[pl.ds(row0 * spr + j, n, stride=spr), :] = lo | hi


def _load_row_slabs(ref, row0, n, spr):
    words = [ref[pl.ds(row0 * spr + j, n, stride=spr), :] for j in range(spr)]
    lo = [lax.bitcast_convert_type(lax.shift_left(w, jnp.uint32(16)), F32) for w in words]
    hi = [lax.bitcast_convert_type(w & jnp.uint32(0xFFFF0000), F32) for w in words]
    return jnp.concatenate(lo + hi, axis=1)


def _inproj_kernel(x_ref, g_ref, w_ref, z_ref, u_ref, h_scr):
    j = pl.program_id(1)
    last = pl.num_programs(1) - 1

    @pl.when(j == 0)
    def _():
        h_scr[...] = _rms(x_ref[...], g_ref[...], 1e-6).astype(BF16)

    acc = jnp.dot(h_scr[...], w_ref[...], preferred_element_type=F32)

    @pl.when(j < last)
    def _():
        z_ref[...] = acc.astype(z_ref.dtype)

    @pl.when(j == last)
    def _():
        u_ref[...] = acc


def _inproj(x2d, g, w, tm=512, tn=1024):
    n, d = x2d.shape
    pw = w.shape[1]
    n_col = pw // tn
    return pl.pallas_call(
        _inproj_kernel,
        grid=(n // tm, n_col),
        in_specs=[
            pl.BlockSpec((tm, d), lambda i, j: (i, 0)),
            pl.BlockSpec((1, d), lambda i, j: (0, 0)),
            pl.BlockSpec((d, tn), lambda i, j: (0, j)),
        ],
        out_specs=[pl.BlockSpec((tm, tn), lambda i, j: (i, jnp.minimum(j, n_col - 2))),
                   pl.BlockSpec((tm, tn), lambda i, j: (i, 0))],
        out_shape=[jax.ShapeDtypeStruct((n, pw - tn), BF16), jax.ShapeDtypeStruct((n, tn), F32)],
        scratch_shapes=[pltpu.VMEM((tm, d), BF16)],
        compiler_params=_params(("parallel", "arbitrary")),
        name="inproj",
    )(x2d, g.reshape(1, d), w)


def _attn_kernel(lam_ref, q_ref, k_ref, v_ref, bias_ref, g_ref, o_ref, qs, vt, *state,
                 tq, tk, cb, out_scale):
    n_blk = 2 * tq // cb
    m_scr, acc, s_scr = (state[i * n_blk:(i + 1) * n_blk] for i in range(3))
    qi = pl.program_id(2)
    n_sub = tq // tk
    n_kv = v_ref.shape[1] // tk

    @pl.when(qi == 0)
    def _():
        for c in range(n_kv):
            vt[c, 0:DV, :] = v_ref[0, c * tk:(c + 1) * tk, :].astype(F32).T.astype(BF16)
            vt[c, DV:DV + ONES_ROWS, :] = jnp.ones((ONES_ROWS, tk), BF16)

    q = q_ref[0]
    lane = lax.broadcasted_iota(jnp.int32, q.shape, 1)
    zero = jnp.zeros_like(q)
    qs[0:tq, :] = jnp.where(lane < DK, q, zero)
    qs[tq:2 * tq, :] = jnp.where(lane >= DK, q, zero)
    for c in range(n_blk):
        m_scr[c][...] = jnp.full(m_scr[c].shape, MASK_VALUE, F32)
        acc[c][...] = jnp.zeros(acc[c].shape, F32)

    all_blocks = tuple(range(n_blk))

    def visible(r):
        return tuple(c for c in all_blocks if (c * cb) % tq + cb > (r - 1) * tk)

    def scores(j, blocks):
        kj = k_ref[0, pl.ds(pl.multiple_of(j * tk, tk), tk), :]
        return [lax.dot_general(kj, qs[c * cb:(c + 1) * cb, :], (((1,), (1,)), ((), ())),
                                preferred_element_type=F32) for c in blocks]

    def consume(j, bias, blocks, next_blocks):
        nxt = scores(j + 1, next_blocks) if next_blocks else None
        vtj = vt[j]
        alphas, ps = [], []
        for c in blocks:
            s = s_scr[c][...]
            if bias is not None:
                qb = (c * cb) % tq
                s = s + bias[:, qb:qb + cb]
            m_old = m_scr[c][...]
            m_new = jnp.maximum(m_old, jnp.max(s, axis=0, keepdims=True))
            alpha = jnp.exp2(m_old - m_new)
            p = jnp.exp2(s - m_new)
            m_scr[c][...] = m_new
            alphas.append(alpha)
            ps.append(p.astype(BF16))
        pvs = [jnp.dot(vtj, p, preferred_element_type=F32) for p in ps]
        for c, alpha, pv in zip(blocks, alphas, pvs):
            acc[c][...] = alpha * acc[c][...] + pv
        if next_blocks:
            for c, sc in zip(next_blocks, nxt):
                s_scr[c][...] = sc

    first = qi * n_sub - 1
    for c, sc in zip(all_blocks, scores(0, all_blocks)):
        s_scr[c][...] = sc

    def far_pair(i, carry):
        consume(2 * i, None, all_blocks, all_blocks)
        consume(2 * i + 1, None, all_blocks, all_blocks)
        return carry

    lax.fori_loop(0, jnp.maximum(first, 0) // 2, far_pair, 0)

    @pl.when(qi >= 1)
    def _():
        consume(first - 1, None, all_blocks, all_blocks)
        consume(first, bias_ref[0, 0], all_blocks, all_blocks)

    for r in range(1, n_sub + 1):
        consume(first + r, bias_ref[0, r], visible(r), visible(r + 1) if r < n_sub else None)

    lam = lam_ref[0, 0]
    half = n_blk // 2
    for b in range(half):
        a1, a2 = acc[b], acc[half + b]
        ot = (a1[0:DV, :] / a1[DV:DV + 1, :] - lam * (a2[0:DV, :] / a2[DV:DV + 1, :]))
        ot = ot * lax.rsqrt(jnp.mean(ot * ot, axis=0, keepdims=True) + 1e-5)
        o_ref[0, b * cb:(b + 1) * cb, :] = (ot.T * (g_ref[...] * out_scale)).astype(o_ref.dtype)


def _t5_bucket(n):
    n = jnp.maximum(n, 0)
    max_exact = N_BUCKETS // 2
    nf = jnp.maximum(n, 1).astype(F32)
    large = max_exact + (jnp.log(nf / max_exact) / math.log(MAX_DISTANCE / max_exact)
                         * (N_BUCKETS - max_exact)).astype(jnp.int32)
    large = jnp.minimum(large, N_BUCKETS - 1)
    return jnp.where(n < max_exact, n, large)


def _attn_bias_tiles(rel_bias, tq, tk):
    assert tk >= MAX_DISTANCE and tq % (2 * tk) == 0
    table = rel_bias.astype(F32)
    rel_table = (table - table[N_BUCKETS - 1][None, :]) * LOG2E
    r = jnp.arange(tk, dtype=jnp.int32)[:, None]
    c = jnp.arange(tq, dtype=jnp.int32)[None, :]
    tiles = []
    for t in range(tq // tk + 1):
        dist = c - (r + (t - 1) * tk)
        onehot = (_t5_bucket(dist)[:, :, None] == jnp.arange(N_BUCKETS, dtype=jnp.int32)).astype(F32)
        b = jnp.einsum('rcn,nh->hrc', onehot, rel_table, precision=HIGHEST)
        tiles.append(jnp.where((dist >= 0)[None], b, MASK_VALUE))
    return jnp.stack(tiles, axis=1)


def _attention(z3, rel_bias, lam, subln_g, lam_init, tq=ATTN_TQ, tk=ATTN_TK, cb=ATTN_CB):
    bt, s_len, _ = z3.shape
    bias = _attn_bias_tiles(rel_bias, tq, tk)
    n_sp = bias.shape[1]
    kern = functools.partial(_attn_kernel, tq=tq, tk=tk, cb=cb, out_scale=1.0 - lam_init)
    n_blk = 2 * tq // cb
    return pl.pallas_call(
        kern,
        grid=(bt, N_HEADS, s_len // tq),
        in_specs=[
            pl.BlockSpec(memory_space=pltpu.SMEM),
            pl.BlockSpec((1, tq, 2 * DK), lambda b, h, i: (b, i, h)),
            pl.BlockSpec((1, s_len, 2 * DK), lambda b, h, i: (b, 0, N_HEADS + h)),
            pl.BlockSpec((1, s_len, DV), lambda b, h, i: (b, 0, 2 * N_HEADS + h)),
            pl.BlockSpec((1, n_sp, tk, tq), lambda b, h, i: (h, 0, 0, 0)),
            pl.BlockSpec((1, DV), lambda b, h, i: (0, 0)),
        ],
        out_specs=pl.BlockSpec((1, tq, DV), lambda b, h, i: (b, i, h)),
        out_shape=jax.ShapeDtypeStruct((bt, s_len, N_HEADS * DV), BF16),
        scratch_shapes=[
            pltpu.VMEM((2 * tq, 2 * DK), BF16),
            pltpu.VMEM((s_len // tk, DV + ONES_ROWS, tk), BF16),
        ] + [pltpu.VMEM((1, cb), F32)] * n_blk + [pltpu.VMEM((DV + ONES_ROWS, cb), F32)] * n_blk
        + [pltpu.VMEM((tk, cb), F32)] * n_blk,
        compiler_params=_params(("parallel", "parallel", "arbitrary")),
        name="diff_attention",
    )(lam.reshape(1, 1).astype(F32), z3, z3, z3, bias, subln_g.reshape(1, DV).astype(F32))


def _ssm_operators(lam_re, lam_im, log_dt, b_re, b_im, c_re, c_im, d_skip):
    L, H, P = SSM_CHUNK, SSM_GROUP, SSM_STATE
    lre = lam_re.astype(F32)
    lim = lam_im.astype(F32)
    dt = jnp.exp(log_dt.astype(F32))[:, None]
    mag = jnp.exp(lre * dt)
    ab_re = mag * jnp.cos(lim * dt)
    ab_im = mag * jnp.sin(lim * dt)
    den = lre * lre + lim * lim
    nr, ni = ab_re - 1.0, ab_im
    cr = ((nr * lre + ni * lim) / den)[..., None]
    ci = ((ni * lre - nr * lim) / den)[..., None]
    bre = b_re.astype(F32)
    bim = b_im.astype(F32)
    bb_re = cr * bre - ci * bim
    bb_im = cr * bim + ci * bre
    cre = c_re.astype(F32)
    cim = c_im.astype(F32)

    tau = jnp.arange(L + 1, dtype=F32)[:, None, None]
    pw_mag = jnp.exp(tau * (lre * dt)[None])
    pw_re = pw_mag * jnp.cos(tau * (lim * dt)[None])
    pw_im = pw_mag * jnp.sin(tau * (lim * dt)[None])

    ca_re = cre[None] * pw_re[:, :, None, :] - cim[None] * pw_im[:, :, None, :]
    ca_im = cre[None] * pw_im[:, :, None, :] + cim[None] * pw_re[:, :, None, :]
    bbt_re = bb_re.transpose(0, 2, 1)[None, :, None]
    bbt_im = bb_im.transpose(0, 2, 1)[None, :, None]
    k_tau = jnp.sum(ca_re[:L, :, :, None, :] * bbt_re - ca_im[:L, :, :, None, :] * bbt_im, axis=-1)
    ti = jnp.arange(L)
    t_src = jnp.tile(k_tau.transpose(0, 1, 3, 2).reshape(L, SSM_GROUPS * H, H), (1, 1, SSM_BUNDLE))

    rev_re = pw_re[L - 1 - ti]
    rev_im = pw_im[L - 1 - ti]
    w_re = rev_re[..., None] * bb_re[None] - rev_im[..., None] * bb_im[None]
    w_im = rev_re[..., None] * bb_im[None] + rev_im[..., None] * bb_re[None]
    w_src = jnp.concatenate([w_re, w_im], axis=2).transpose(0, 1, 3, 2)

    e_src = jnp.concatenate([ca_re[1:L + 1], -ca_im[1:L + 1]], axis=3)

    a1 = jnp.concatenate([pw_re[L], pw_re[L]], axis=-1)
    a2 = jnp.concatenate([-pw_im[L], pw_im[L]], axis=-1)
    d_tile = jnp.tile(d_skip.astype(F32).reshape(SSM_GROUPS // SSM_BUNDLE, 1, SSM_BUNDLE * H), (1, 1, L))
    return t_src.astype(BF16), w_src.astype(BF16), e_src.astype(BF16), a1, a2, d_tile


def _bundle_operator(dst, src_ref):
    n_l, n_g, h, x = src_ref.shape
    dst[...] = jnp.zeros(dst.shape, dst.dtype)
    for i in range(n_l):
        for g in range(n_g):
            r0 = (i * n_g + g) * h
            dst[r0:r0 + h, g * x:(g + 1) * x] = src_ref[i, g]


def _chunk_rows(u_ref, n_chunk):
    return jnp.concatenate([u_ref[pl.ds(t, n_chunk, stride=SSM_CHUNK), :] for t in range(SSM_CHUNK)], axis=1)


def _ssm_in_kernel(u_ref, w_ref, v_ref, w_scr):
    @pl.when(pl.program_id(1) == 0)
    def _():
        _bundle_operator(w_scr, w_ref)

    x = _chunk_rows(u_ref, v_ref.shape[0]).astype(BF16)
    v = jnp.dot(x, w_scr[...], preferred_element_type=F32)
    sw = v_ref.shape[2]
    for g in range(v_ref.shape[1]):
        v_ref[:, g, :] = v[:, g * sw:(g + 1) * sw]


def _ssm_scan_kernel(v_ref, a1_ref, a2_ref, o_ref, st):
    @pl.when(pl.program_id(0) == 0)
    def _():
        st[...] = jnp.zeros(st.shape, F32)

    a1 = a1_ref[...][None]
    a2 = a2_ref[...][None]
    n_chunk = v_ref.shape[1]
    half = v_ref.shape[3] // 2

    def body(c, s):
        o_ref[:, pl.ds(c, 1)] = s[:, None].astype(o_ref.dtype)
        v = v_ref[:, pl.ds(c, 1)][:, 0]
        return a1 * s + a2 * pltpu.roll(s, half, axis=2) + v

    st[...] = lax.fori_loop(0, n_chunk, body, st[...])


def _gelu_tanh(x):
    c = math.sqrt(2.0 / math.pi)
    return x * (0.5 * (1.0 + jnp.tanh(c * (x + 0.044715 * (x * x * x)))))


def _ssm_out_kernel(u_ref, t_ref, s_ref, e_ref, d_ref, y_ref, t_scr, et_scr):
    L, lw, h = SSM_CHUNK, t_ref.shape[1], SSM_GROUP

    @pl.when(pl.program_id(1) == 0)
    def _():
        _bundle_operator(et_scr, e_ref)
        same_group = (lax.broadcasted_iota(jnp.int32, (lw, lw), 0) // h
                      == lax.broadcasted_iota(jnp.int32, (lw, lw), 1) // h)
        t_scr[...] = jnp.zeros(t_scr.shape, t_scr.dtype)
        for tau in range(L):
            blk = jnp.where(same_group, t_ref[tau], jnp.zeros((lw, lw), t_ref.dtype))
            for i in range(L - tau):
                j = i + tau
                t_scr[i * lw:(i + 1) * lw, j * lw:(j + 1) * lw] = blk

    n_chunk = s_ref.shape[0]
    xf = _chunk_rows(u_ref, n_chunk)
    sp = jnp.concatenate([s_ref[:, g, :] for g in range(s_ref.shape[1])], axis=1).astype(BF16)
    y = jnp.dot(xf.astype(BF16), t_scr[...], preferred_element_type=F32)
    y = y + lax.dot_general(sp, et_scr[...], (((1,), (1,)), ((), ())), preferred_element_type=F32)
    y = _gelu_tanh(y + d_ref[0] * xf)
    for t in range(L):
        y_ref[pl.ds(t, n_chunk, stride=L), :] = y[:, t * lw:(t + 1) * lw]


def _ssm(u, ops, bt, scan_block=32, chunk_tile=256):
    t_src, w_src, e_src, a1, a2, d_tile = ops
    n, width = u.shape
    L, G, H, P2, nbg = SSM_CHUNK, SSM_GROUPS, SSM_GROUP, 2 * SSM_STATE, SSM_BUNDLE
    nb = G // nbg
    lw = nbg * H
    nc = n // L
    n_c = nc // bt
    ct = min(chunk_tile, nc)
    u_spec = pl.BlockSpec((ct * L, lw), lambda b, i: (i, b))
    src_spec = pl.BlockSpec((L, nbg, H, P2), lambda b, i: (0, b, 0, 0))
    state_spec = pl.BlockSpec((ct, nbg, P2), lambda b, i: (i, b, 0))

    v = pl.pallas_call(
        _ssm_in_kernel,
        grid=(nb, nc // ct),
        in_specs=[u_spec, src_spec],
        out_specs=state_spec,
        out_shape=jax.ShapeDtypeStruct((nc, G, P2), F32),
        scratch_shapes=[pltpu.VMEM((L * lw, nbg * P2), BF16)],
        compiler_params=_params(("parallel", "arbitrary")),
        name="ssm_chunk_state",
    )(u, w_src)

    cb = min(scan_block, n_c)
    s_prev = pl.pallas_call(
        _ssm_scan_kernel,
        grid=(n_c // cb,),
        in_specs=[pl.BlockSpec((bt, cb, G, P2), lambda c: (0, c, 0, 0)),
                  pl.BlockSpec((G, P2), lambda c: (0, 0)),
                  pl.BlockSpec((G, P2), lambda c: (0, 0))],
        out_specs=pl.BlockSpec((bt, cb, G, P2), lambda c: (0, c, 0, 0)),
        out_shape=jax.ShapeDtypeStruct((bt, n_c, G, P2), F32),
        scratch_shapes=[pltpu.VMEM((bt, G, P2), F32)],
        compiler_params=_params(("arbitrary",)),
        name="ssm_scan",
    )(v.reshape(bt, n_c, G, P2), a1, a2)

    return pl.pallas_call(
        _ssm_out_kernel,
        grid=(nb, nc // ct),
        in_specs=[u_spec,
                  pl.BlockSpec((L, lw, lw), lambda b, i: (0, b, 0)),
                  state_spec,
                  src_spec,
                  pl.BlockSpec((1, 1, L * lw), lambda b, i: (b, 0, 0))],
        out_specs=pl.BlockSpec((ct * L, lw), lambda b, i: (i, b)),
        out_shape=jax.ShapeDtypeStruct((n, width), F32),
        scratch_shapes=[pltpu.VMEM((L * lw, L * lw), BF16), pltpu.VMEM((L * lw, nbg * P2), BF16)],
        compiler_params=_params(("parallel", "arbitrary")),
        name="ssm_output",
    )(u, t_src, s_prev.reshape(nc, G, P2), e_src, d_tile)


def _postmix_kernel(a_ref, s_ref, x_ref, wg_ref, bg_ref, gs_ref, woa_ref, wos_ref, gf_ref,
                    wrh_ref, wrl_ref, br_ref, x1_ref, h_ref, lg_ref):
    sf = s_ref[...]
    gate = _sigmoid(jnp.dot(sf.astype(BF16), wg_ref[...], preferred_element_type=F32) + bg_ref[...])
    sn = _rms(sf * gate, gs_ref[...], 1e-6).astype(BF16)
    x1 = (x_ref[...]
          + jnp.dot(a_ref[...], woa_ref[...], preferred_element_type=F32)
          + jnp.dot(sn, wos_ref[...], preferred_element_type=F32))
    x1_ref[...] = x1
    h = _rms(x1, gf_ref[...], 1e-6)
    h_hi = h.astype(BF16)
    h_lo = (h - h_hi.astype(F32)).astype(BF16)
    _store_row_slabs(h_ref, 0, h_hi)
    lg_ref[...] = (jnp.dot(h_hi, wrh_ref[...], preferred_element_type=F32)
                   + jnp.dot(h_lo, wrh_ref[...], preferred_element_type=F32)
                   + jnp.dot(h_hi, wrl_ref[...], preferred_element_type=F32)
                   + br_ref[...])


def _postmix(a, s, x2d, w_glu, b_glu, g_s, w_o, g_ffn, w_r, b_r, tm=256):
    n, d = x2d.shape
    wa = a.shape[1]
    ws = s.shape[1]
    wr_hi = w_r.astype(BF16)
    wr_lo = (w_r - wr_hi.astype(F32)).astype(BF16)
    row = lambda i: (i, 0)
    fixed = lambda i: (0, 0)
    return pl.pallas_call(
        _postmix_kernel,
        grid=(n // tm,),
        in_specs=[
            pl.BlockSpec((tm, wa), row),
            pl.BlockSpec((tm, ws), row),
            pl.BlockSpec((tm, d), row),
            pl.BlockSpec((ws, ws), fixed),
            pl.BlockSpec((1, ws), fixed),
            pl.BlockSpec((1, ws), fixed),
            pl.BlockSpec((wa, d), fixed),
            pl.BlockSpec((ws, d), fixed),
            pl.BlockSpec((1, d), fixed),
            pl.BlockSpec((d, ROUTER_LANES), fixed),
            pl.BlockSpec((d, ROUTER_LANES), fixed),
            pl.BlockSpec((1, ROUTER_LANES), fixed),
        ],
        out_specs=[pl.BlockSpec((tm, d), row), pl.BlockSpec((tm * (d // SLAB_COLS), LANES), row),
                   pl.BlockSpec((tm, ROUTER_LANES), row)],
        out_shape=[jax.ShapeDtypeStruct((n, d), F32), jax.ShapeDtypeStruct((n * (d // SLAB_COLS), LANES), jnp.uint32),
                   jax.ShapeDtypeStruct((n, ROUTER_LANES), F32)],
        compiler_params=_params(("parallel",)),
        name="postmix",
    )(a, s, x2d, w_glu.astype(BF16), b_glu.reshape(1, ws).astype(F32), g_s.reshape(1, ws).astype(F32),
      w_o[:wa].astype(BF16), w_o[wa:].astype(BF16), g_ffn.reshape(1, d).astype(F32),
      wr_hi, wr_lo, b_r.reshape(1, ROUTER_LANES).astype(F32))


def _route(logits, rows, slab_rows):
    n_tok = logits.shape[0]
    lg = logits[:, :N_EXPERT_GROUPS]
    le = logits[:, N_EXPERT_GROUPS:N_EXPERT_GROUPS + N_EXPERTS].reshape(
        n_tok, N_EXPERT_GROUPS, EXPERTS_PER_GROUP)
    pg = jax.nn.softmax(lg, axis=-1)
    gsel = jnp.argmax(lg, axis=-1).astype(jnp.int32)
    gate_g = jnp.max(pg, axis=-1, keepdims=True)
    sel = gsel[:, None] == jnp.arange(N_EXPERT_GROUPS, dtype=jnp.int32)[None, :]
    le_sel = jnp.sum(jnp.where(sel[:, :, None], le, 0.0), axis=1)
    pe = jax.nn.softmax(le_sel, axis=-1)
    top_p, top_i = lax.top_k(pe, TOP_K)
    w = gate_g * top_p / jnp.sum(top_p, axis=-1, keepdims=True)
    eid = gsel[:, None] * EXPERTS_PER_GROUP + top_i.astype(jnp.int32)

    n_assign = n_tok * TOP_K
    flat_e = eid.reshape(-1)
    counts = jnp.sum((flat_e[None, :] == jnp.arange(N_EXPERTS, dtype=jnp.int32)[:, None])
                     .astype(jnp.int32), axis=1)
    n_blk_e = (counts + rows - 1) // rows
    blk_end = jnp.cumsum(n_blk_e)
    blk_first = blk_end - n_blk_e
    start = jnp.cumsum(counts) - counts
    order = jnp.argsort(flat_e, stable=True).astype(jnp.int32)
    n_blk = (n_assign + rows - 1) // rows + N_EXPERTS
    blk = jnp.arange(n_blk, dtype=jnp.int32)
    blk_e = jnp.minimum(jnp.sum((blk[:, None] >= blk_end[None, :]).astype(jnp.int32), axis=1),
                        N_EXPERTS - 1)
    used = blk < blk_end[-1]
    experts = jnp.arange(N_EXPERTS, dtype=jnp.int32)
    is_e = blk_e[:, None] == experts[None, :]

    def per_block(table):
        return jnp.sum(jnp.where(is_e, table[None, :], 0), axis=1)

    in_e = (blk - per_block(blk_first)) * rows
    blk_cnt = jnp.where(used, jnp.clip(per_block(counts) - in_e, 0, rows), 0).astype(jnp.int32)
    blk_src = jnp.where(used, per_block(start) + in_e, 0).astype(jnp.int32)
    last_e = jnp.max(jnp.where(counts > 0, experts, 0))
    blk_e = jnp.where(used, blk_e, last_e)
    row_tok = order // TOP_K
    row_dst = (order % TOP_K) * n_tok + row_tok
    pad = jnp.zeros((rows,), jnp.int32)
    return (w, jnp.concatenate([row_tok * slab_rows, pad]), jnp.concatenate([row_dst * slab_rows, pad]),
            blk_e, blk_cnt, blk_src)


def _expert_kernel(blk_e_ref, blk_cnt_ref, blk_src_ref, tok_ref, dst_ref,
                   h_hbm, w1_ref, w3_ref, w2_ref, y_hbm, xbuf, ybuf, gsem, ssem):
    del blk_e_ref
    b = pl.program_id(0)
    slot = b % 2
    spr = w1_ref.shape[1] // SLAB_COLS
    rows = xbuf.shape[1] // spr
    dump = y_hbm.shape[0] - rows * spr
    cnt = blk_cnt_ref[b]
    prev = jnp.maximum(b - 1, 0)
    prev_cnt = jnp.where(b >= 1, blk_cnt_ref[prev], 0)

    def gather_copy(src, r, sl):
        return pltpu.make_async_copy(h_hbm.at[pl.ds(pl.multiple_of(tok_ref[src + r], spr), spr)],
                                     xbuf.at[sl, pl.ds(r * spr, spr)], gsem.at[sl])

    def scatter_copy(src, n_valid, r, sl):
        dst = jnp.where(r < n_valid, dst_ref[src + r], dump + r * spr)
        return pltpu.make_async_copy(ybuf.at[sl, pl.ds(r * spr, spr)],
                                     y_hbm.at[pl.ds(pl.multiple_of(dst, spr), spr)], ssem.at[sl])

    def start_gathers(blk, sl, lo=0, hi=None):
        src = blk_src_ref[blk]
        for r in range(lo, rows if hi is None else hi):
            gather_copy(src, r, sl).start()

    def wait_gathers(sl):
        for r in range(rows):
            gather_copy(0, r, sl).wait()

    def start_scatters(blk, n_valid, sl, lo=0, hi=None):
        src = blk_src_ref[blk]
        for r in range(lo, rows if hi is None else hi):
            scatter_copy(src, n_valid, r, sl).start()

    def wait_scatters(sl):
        for r in range(rows):
            scatter_copy(0, 0, r, sl).wait()

    @pl.when(b == 0)
    def _():
        ybuf[...] = jnp.zeros(ybuf.shape, ybuf.dtype)
        start_gathers(0, 0)

    @pl.when(prev_cnt > 0)
    def _():
        wait_scatters(slot)

    @pl.when(cnt > 0)
    def _():
        wait_gathers(slot)
        x = _load_row_slabs(xbuf.at[slot], 0, rows, spr).astype(BF16)
        f = w1_ref.shape[2]
        n_f = f // MXU_WIDTH
        n_part = 3 * n_f
        bounds = [rows * i // n_part for i in range(n_part + 1)]
        part = iter(range(n_part))

        def issue_part():
            i = next(part)
            start_gathers(b + 1, 1 - slot, bounds[i], bounds[i + 1])
            start_scatters(prev, prev_cnt, 1 - slot, bounds[i], bounds[i + 1])

        acts = []
        for c in range(n_f):
            cols = slice(c * MXU_WIDTH, (c + 1) * MXU_WIDTH)
            issue_part()
            h1 = jnp.dot(x, w1_ref[0, :, cols].astype(BF16), preferred_element_type=F32)
            issue_part()
            h3 = jnp.dot(x, w3_ref[0, :, cols].astype(BF16), preferred_element_type=F32)
            acts.append((h1 * _sigmoid(h1) * h3).astype(BF16))
        y = None
        for c in range(n_f):
            issue_part()
            yc = jnp.dot(acts[c], w2_ref[0, c * MXU_WIDTH:(c + 1) * MXU_WIDTH, :].astype(BF16),
                         preferred_element_type=F32)
            y = yc if y is None else y + yc
        _store_row_slabs(ybuf.at[slot], 0, y)

    @pl.when((cnt == 0) & (prev_cnt > 0))
    def _():
        wait_gathers(slot)
        start_scatters(prev, prev_cnt, 1 - slot)
        wait_scatters(1 - slot)


def _experts(h, w1, w3, w2, row_tok, row_dst, blk_e, blk_cnt, blk_src, rows):
    d, f = w1.shape[1], w1.shape[2]
    spr = d // SLAB_COLS
    n_blk = blk_e.shape[0]
    wmap = lambda b, be, bc, bs, rt, rd: (be[b], 0, 0)
    grid_spec = pltpu.PrefetchScalarGridSpec(
        num_scalar_prefetch=5,
        grid=(n_blk,),
        in_specs=[
            pl.BlockSpec(memory_space=pl.ANY),
            pl.BlockSpec((1, d, f), wmap),
            pl.BlockSpec((1, d, f), wmap),
            pl.BlockSpec((1, f, d), wmap),
        ],
        out_specs=pl.BlockSpec(memory_space=pl.ANY),
        scratch_shapes=[
            pltpu.VMEM((2, rows * spr, LANES), jnp.uint32),
            pltpu.VMEM((2, rows * spr, LANES), jnp.uint32),
            pltpu.SemaphoreType.DMA((2,)),
            pltpu.SemaphoreType.DMA((2,)),
        ],
    )
    return pl.pallas_call(
        _expert_kernel,
        grid_spec=grid_spec,
        out_shape=jax.ShapeDtypeStruct((h.shape[0] * TOP_K + rows * spr, LANES), jnp.uint32),
        compiler_params=_params(("arbitrary",), disable_bounds_checks=True),
        name="experts",
    )(blk_e, blk_cnt, blk_src, row_tok, row_dst, h, w1, w3, w2)


def _ple_kernel(x1_ref, y0_ref, y1_ref, w_ref, p_ref, gp_ref, wg_ref, wp_ref, gf_ref, o_ref, *, final):
    w = w_ref[...]
    tm, d = x1_ref.shape
    spr = d // SLAB_COLS
    y0 = _load_row_slabs(y0_ref, 0, tm, spr)
    y1 = _load_row_slabs(y1_ref, 0, tm, spr)
    x2 = x1_ref[...] + w[:, 0:1] * y0 + w[:, 1:2] * y1
    hn = _rms(x2, gp_ref[...], 1e-6).astype(BF16)
    gate = _sigmoid(jnp.dot(hn, wg_ref[...], preferred_element_type=F32))
    pp = jnp.dot(p_ref[...].astype(BF16), wp_ref[...], preferred_element_type=F32)
    x3 = x2 + gate * pp
    o_ref[...] = _rms(x3, gf_ref[...], 1e-6) if final else x3


def _ple(x1, y2, w, p2d, g_ple, w_gate, w_proj, g_final, final, tm=256):
    n, d = x1.shape
    pd = p2d.shape[1]
    row = lambda i: (i, 0)
    fixed = lambda i: (0, 0)
    return pl.pallas_call(
        functools.partial(_ple_kernel, final=final),
        grid=(n // tm,),
        in_specs=[
            pl.BlockSpec((tm, d), row),
            pl.BlockSpec((tm * (d // SLAB_COLS), LANES), row),
            pl.BlockSpec((tm * (d // SLAB_COLS), LANES), lambda i: (n // tm + i, 0)),
            pl.BlockSpec((tm, TOP_K), row),
            pl.BlockSpec((tm, pd), row),
            pl.BlockSpec((1, d), fixed),
            pl.BlockSpec((d, d), fixed),
            pl.BlockSpec((pd, d), fixed),
            pl.BlockSpec((1, d), fixed),
        ],
        out_specs=pl.BlockSpec((tm, d), row),
        out_shape=jax.ShapeDtypeStruct((n, d), F32),
        compiler_params=_params(("parallel",)),
        name="ple_final",
    )(x1, y2, y2, w, p2d, g_ple.reshape(1, d).astype(F32), w_gate.astype(BF16), w_proj.astype(BF16),
      g_final.reshape(1, d).astype(F32))


def kernel(x, p, rel_bias, g_mix, w_in, lam_q1, lam_k1, lam_q2, lam_k2, subln_g, ssm_lam_re, ssm_lam_im, ssm_log_dt, ssm_b_re, ssm_b_im, ssm_c_re, ssm_c_im, ssm_d, w_glu, b_glu, ssm_norm_g, w_o, g_ffn, w_router_g, b_router_g, w_router_e, b_router_e, w1, w3, w2, g_ple, w_ple_gate, w_ple_proj, g_final):
    bt, s_len, d = x.shape
    n = bt * s_len
    depth = g_mix.shape[0]
    attn_w = N_HEADS * DV
    xc = x.reshape(n, d).astype(F32)
    for i in range(depth):
        lam_init = 0.8 - 0.6 * math.exp(-0.3 * i)
        col_scale = jnp.concatenate([jnp.full((attn_w,), LOG2E * DK ** -0.5, F32),
                                     jnp.ones((w_in.shape[2] - attn_w,), F32)])
        w_in_b = (w_in[i].astype(F32) * col_scale[None, :]).astype(BF16)
        z, u = _inproj(xc, g_mix[i].astype(F32), w_in_b)
        z3 = z.reshape(bt, s_len, z.shape[1])

        lam = (jnp.exp(jnp.sum(lam_q1[i].astype(F32) * lam_k1[i].astype(F32)))
               - jnp.exp(jnp.sum(lam_q2[i].astype(F32) * lam_k2[i].astype(F32))) + lam_init)
        a = _attention(z3, rel_bias, lam, subln_g[i], lam_init)

        ops = _ssm_operators(ssm_lam_re[i], ssm_lam_im[i], ssm_log_dt[i], ssm_b_re[i], ssm_b_im[i],
                             ssm_c_re[i], ssm_c_im[i], ssm_d[i])
        s = _ssm(u, ops, bt)

        w_r = jnp.zeros((d, ROUTER_LANES), F32)
        w_r = w_r.at[:, :N_EXPERT_GROUPS].set(w_router_g[i].astype(F32))
        w_r = w_r.at[:, N_EXPERT_GROUPS:N_EXPERT_GROUPS + N_EXPERTS].set(w_router_e[i].astype(F32))
        b_r = jnp.zeros((ROUTER_LANES,), F32)
        b_r = b_r.at[:N_EXPERT_GROUPS].set(b_router_g[i].astype(F32))
        b_r = b_r.at[N_EXPERT_GROUPS:N_EXPERT_GROUPS + N_EXPERTS].set(b_router_e[i].astype(F32))
        x1, h, logits = _postmix(a.reshape(n, attn_w), s, xc, w_glu[i], b_glu[i],
                                 ssm_norm_g[i], w_o[i], g_ffn[i], w_r, b_r)

        gate_w, row_tok, row_dst, blk_e, blk_cnt, blk_src = _route(logits, EXPERT_ROWS, d // SLAB_COLS)
        y2 = _experts(h, w1[i], w3[i], w2[i],
                      row_tok, row_dst, blk_e, blk_cnt, blk_src, EXPERT_ROWS)
        xc = _ple(x1, y2, gate_w.astype(F32), p[i].reshape(n, -1).astype(F32),
                  g_ple[i], w_ple_gate[i], w_ple_proj[i], g_final, final=(i == depth - 1))
    return xc.reshape(bt, s_len, d)
```

```python
import functools
import math

import jax
import jax.numpy as jnp
from jax import lax
from jax.experimental import pallas as pl
from jax.experimental.pallas import tpu as pltpu

F32 = jnp.float32
BF16 = jnp.bfloat16
HIGHEST = lax.Precision.HIGHEST

N_HEADS = 8
DK = 64
DV = 128
N_BUCKETS = 32
MAX_DISTANCE = 128
SSM_GROUP = 16
SSM_GROUPS = 64
SSM_STATE = 64
N_EXPERT_GROUPS = 4
EXPERTS_PER_GROUP = 8
N_EXPERTS = 32
TOP_K = 2
MASK_VALUE = -1e30
LOG2E = math.log2(math.e)

SSM_CHUNK = 16
SSM_BUNDLE = 8
ATTN_TQ = 512
ONES_ROWS = 16
ATTN_CB = 256
ATTN_TK = 256
EXPERT_ROWS = 256
N_SLOTS = 3
MXU_WIDTH = 256
LANES = 128
SLAB_COLS = 2 * LANES
ROUTER_LANES = 128
VMEM_LIMIT = 56 << 20


def _params(semantics, **kw):
    return pltpu.CompilerParams(dimension_semantics=semantics, vmem_limit_bytes=VMEM_LIMIT, **kw)


def _rms(x, g, eps):
    return x * lax.rsqrt(jnp.mean(x * x, axis=-1, keepdims=True) + eps) * g


def _sigmoid(x):
    return 1.0 / (1.0 + jnp.exp(-x))


def _store_row_slabs(ref, row0, x):
    n, d = x.shape
    spr = d // (2 * LANES)
    bits = lax.bitcast_convert_type(x.astype(BF16).astype(F32), jnp.uint32)
    for j in range(spr):
        lo = lax.shift_right_logical(bits[:, j * LANES:(j + 1) * LANES], jnp.uint32(16))
        hi = bits[:, d // 2 + j * LANES:d // 2 + (j + 1) * LANES] & jnp.uint32(0xFFFF0000)
        ref[pl.ds(row0 * spr + j, n, stride=spr), :] = lo | hi


def _load_row_slabs(ref, row0, n, spr):
    words = [ref[pl.ds(row0 * spr + j, n, stride=spr), :] for j in range(spr)]
    lo = [lax.bitcast_convert_type(lax.shift_left(w, jnp.uint32(16)), F32) for w in words]
    hi = [lax.bitcast_convert_type(w & jnp.uint32(0xFFFF0000), F32) for w in words]
    return jnp.concatenate(lo + hi, axis=1)


def _inproj_kernel(x_ref, g_ref, w_ref, z_ref, u_ref, h_scr):
    j = pl.program_id(1)
    last = pl.num_programs(1) - 1

    @pl.when(j == 0)
    def _():
        h_scr[...] = _rms(x_ref[...], g_ref[...], 1e-6).astype(BF16)

    acc = jnp.dot(h_scr[...], w_ref[...], preferred_element_type=F32)

    @pl.when(j < last)
    def _():
        z_ref[...] = acc.astype(z_ref.dtype)

    @pl.when(j == last)
    def _():
        u_ref[...] = acc


def _inproj(x2d, g, w, tm=512, tn=1024):
    n, d = x2d.shape
    pw = w.shape[1]
    n_col = pw // tn
    return pl.pallas_call(
        _inproj_kernel,
        grid=(n // tm, n_col),
        in_specs=[
            pl.BlockSpec((tm, d), lambda i, j: (i, 0)),
            pl.BlockSpec((1, d), lambda i, j: (0, 0)),
            pl.BlockSpec((d, tn), lambda i, j: (0, j)),
        ],
        out_specs=[pl.BlockSpec((tm, tn), lambda i, j: (i, jnp.minimum(j, n_col - 2))),
                   pl.BlockSpec((tm, tn), lambda i, j: (i, 0))],
        out_shape=[jax.ShapeDtypeStruct((n, pw - tn), BF16), jax.ShapeDtypeStruct((n, tn), F32)],
        scratch_shapes=[pltpu.VMEM((tm, d), BF16)],
        compiler_params=_params(("parallel", "arbitrary")),
        name="inproj",
    )(x2d, g.reshape(1, d), w)


def _attn_kernel(lam_ref, q_ref, k_ref, v_ref, bias_ref, g_ref, o_ref, qs, vt, *state,
                 tq, tk, cb, out_scale):
    n_blk = 2 * tq // cb
    m_scr, acc, s_scr = (state[i * n_blk:(i + 1) * n_blk] for i in range(3))
    qi = pl.program_id(2)
    n_sub = tq // tk
    n_kv = v_ref.shape[1] // tk

    @pl.when(qi == 0)
    def _():
        for c in range(n_kv):
            vt[c, 0:DV, :] = v_ref[0, c * tk:(c + 1) * tk, :].astype(F32).T.astype(BF16)
            vt[c, DV:DV + ONES_ROWS, :] = jnp.ones((ONES_ROWS, tk), BF16)

    q = q_ref[0]
    lane = lax.broadcasted_iota(jnp.int32, q.shape, 1)
    zero = jnp.zeros_like(q)
    qs[0:tq, :] = jnp.where(lane < DK, q, zero)
    qs[tq:2 * tq, :] = jnp.where(lane >= DK, q, zero)
    for c in range(n_blk):
        m_scr[c][...] = jnp.full(m_scr[c].shape, MASK_VALUE, F32)
        acc[c][...] = jnp.zeros(acc[c].shape, F32)

    all_blocks = tuple(range(n_blk))

    def visible(r):
        return tuple(c for c in all_blocks if (c * cb) % tq + cb > (r - 1) * tk)

    def scores(j, blocks):
        kj = k_ref[0, pl.ds(pl.multiple_of(j * tk, tk), tk), :]
        return [lax.dot_general(kj, qs[c * cb:(c + 1) * cb, :], (((1,), (1,)), ((), ())),
                                preferred_element_type=F32) for c in blocks]

    def consume(j, bias, blocks, next_blocks):
        nxt = scores(j + 1, next_blocks) if next_blocks else None
        vtj = vt[j]
        alphas, ps = [], []
        for c in blocks:
            s = s_scr[c][...]
            if bias is not None:
                qb = (c * cb) % tq
                s = s + bias[:, qb:qb + cb]
            m_old = m_scr[c][...]
            m_new = jnp.maximum(m_old, jnp.max(s, axis=0, keepdims=True))
            alpha = jnp.exp2(m_old - m_new)
            p = jnp.exp2(s - m_new)
            m_scr[c][...] = m_new
            alphas.append(alpha)
            ps.append(p.astype(BF16))
        pvs = [jnp.dot(vtj, p, preferred_element_type=F32) for p in ps]
        for c, alpha, pv in zip(blocks, alphas, pvs):
            acc[c][...] = alpha * acc[c][...] + pv
        if next_blocks:
            for c, sc in zip(next_blocks, nxt):
                s_scr[c][...] = sc

    first = qi * n_sub - 1
    for c, sc in zip(all_blocks, scores(0, all_blocks)):
        s_scr[c][...] = sc

    def far_pair(i, carry):
        consume(2 * i, None, all_blocks, all_blocks)
        consume(2 * i + 1, None, all_blocks, all_blocks)
        return carry

    lax.fori_loop(0, jnp.maximum(first, 0) // 2, far_pair, 0)

    @pl.when(qi >= 1)
    def _():
        consume(first - 1, None, all_blocks, all_blocks)
        consume(first, bias_ref[0, 0], all_blocks, all_blocks)

    for r in range(1, n_sub + 1):
        consume(first + r, bias_ref[0, r], visible(r), visible(r + 1) if r < n_sub else None)

    lam = lam_ref[0, 0]
    half = n_blk // 2
    for b in range(half):
        a1, a2 = acc[b], acc[half + b]
        ot = (a1[0:DV, :] / a1[DV:DV + 1, :] - lam * (a2[0:DV, :] / a2[DV:DV + 1, :]))
        ot = ot * lax.rsqrt(jnp.mean(ot * ot, axis=0, keepdims=True) + 1e-5)
        o_ref[0, b * cb:(b + 1) * cb, :] = (ot.T * (g_ref[...] * out_scale)).astype(o_ref.dtype)


def _t5_bucket(n):
    n = jnp.maximum(n, 0)
    max_exact = N_BUCKETS // 2
    nf = jnp.maximum(n, 1).astype(F32)
    large = max_exact + (jnp.log(nf / max_exact) / math.log(MAX_DISTANCE / max_exact)
                         * (N_BUCKETS - max_exact)).astype(jnp.int32)
    large = jnp.minimum(large, N_BUCKETS - 1)
    return jnp.where(n < max_exact, n, large)


def _attn_bias_tiles(rel_bias, tq, tk):
    assert tk >= MAX_DISTANCE and tq % (2 * tk) == 0
    table = rel_bias.astype(F32)
    rel_table = (table - table[N_BUCKETS - 1][None, :]) * LOG2E
    r = jnp.arange(tk, dtype=jnp.int32)[:, None]
    c = jnp.arange(tq, dtype=jnp.int32)[None, :]
    tiles = []
    for t in range(tq // tk + 1):
        dist = c - (r + (t - 1) * tk)
        onehot = (_t5_bucket(dist)[:, :, None] == jnp.arange(N_BUCKETS, dtype=jnp.int32)).astype(F32)
        b = jnp.einsum('rcn,nh->hrc', onehot, rel_table, precision=HIGHEST)
        tiles.append(jnp.where((dist >= 0)[None], b, MASK_VALUE))
    return jnp.stack(tiles, axis=1)


def _attention(z3, rel_bias, lam, subln_g, lam_init, tq=ATTN_TQ, tk=ATTN_TK, cb=ATTN_CB):
    bt, s_len, _ = z3.shape
    bias = _attn_bias_tiles(rel_bias, tq, tk)
    n_sp = bias.shape[1]
    kern = functools.partial(_attn_kernel, tq=tq, tk=tk, cb=cb, out_scale=1.0 - lam_init)
    n_blk = 2 * tq // cb
    return pl.pallas_call(
        kern,
        grid=(bt, N_HEADS, s_len // tq),
        in_specs=[
            pl.BlockSpec(memory_space=pltpu.SMEM),
            pl.BlockSpec((1, tq, 2 * DK), lambda b, h, i: (b, i, h)),
            pl.BlockSpec((1, s_len, 2 * DK), lambda b, h, i: (b, 0, N_HEADS + h)),
            pl.BlockSpec((1, s_len, DV), lambda b, h, i: (b, 0, 2 * N_HEADS + h)),
            pl.BlockSpec((1, n_sp, tk, tq), lambda b, h, i: (h, 0, 0, 0)),
            pl.BlockSpec((1, DV), lambda b, h, i: (0, 0)),
        ],
        out_specs=pl.BlockSpec((1, tq, DV), lambda b, h, i: (b, i, h)),
        out_shape=jax.ShapeDtypeStruct((bt, s_len, N_HEADS * DV), BF16),
        scratch_shapes=[
            pltpu.VMEM((2 * tq, 2 * DK), BF16),
            pltpu.VMEM((s_len // tk, DV + ONES_ROWS, tk), BF16),
        ] + [pltpu.VMEM((1, cb), F32)] * n_blk + [pltpu.VMEM((DV + ONES_ROWS, cb), F32)] * n_blk
        + [pltpu.VMEM((tk, cb), F32)] * n_blk,
        compiler_params=_params(("parallel", "parallel", "arbitrary")),
        name="diff_attention",
    )(lam.reshape(1, 1).astype(F32), z3, z3, z3, bias, subln_g.reshape(1, DV).astype(F32))


def _ssm_operators(lam_re, lam_im, log_dt, b_re, b_im, c_re, c_im, d_skip):
    L, H, P = SSM_CHUNK, SSM_GROUP, SSM_STATE
    lre = lam_re.astype(F32)
    lim = lam_im.astype(F32)
    dt = jnp.exp(log_dt.astype(F32))[:, None]
    mag = jnp.exp(lre * dt)
    ab_re = mag * jnp.cos(lim * dt)
    ab_im = mag * jnp.sin(lim * dt)
    den = lre * lre + lim * lim
    nr, ni = ab_re - 1.0, ab_im
    cr = ((nr * lre + ni * lim) / den)[..., None]
    ci = ((ni * lre - nr * lim) / den)[..., None]
    bre = b_re.astype(F32)
    bim = b_im.astype(F32)
    bb_re = cr * bre - ci * bim
    bb_im = cr * bim + ci * bre
    cre = c_re.astype(F32)
    cim = c_im.astype(F32)

    tau = jnp.arange(L + 1, dtype=F32)[:, None, None]
    pw_mag = jnp.exp(tau * (lre * dt)[None])
    pw_re = pw_mag * jnp.cos(tau * (lim * dt)[None])
    pw_im = pw_mag * jnp.sin(tau * (lim * dt)[None])

    ca_re = cre[None] * pw_re[:, :, None, :] - cim[None] * pw_im[:, :, None, :]
    ca_im = cre[None] * pw_im[:, :, None, :] + cim[None] * pw_re[:, :, None, :]
    bbt_re = bb_re.transpose(0, 2, 1)[None, :, None]
    bbt_im = bb_im.transpose(0, 2, 1)[None, :, None]
    k_tau = jnp.sum(ca_re[:L, :, :, None, :] * bbt_re - ca_im[:L, :, :, None, :] * bbt_im, axis=-1)
    ti = jnp.arange(L)
    t_src = jnp.tile(k_tau.transpose(0, 1, 3, 2).reshape(L, SSM_GROUPS * H, H), (1, 1, SSM_BUNDLE))

    rev_re = pw_re[L - 1 - ti]
    rev_im = pw_im[L - 1 - ti]
    w_re = rev_re[..., None] * bb_re[None] - rev_im[..., None] * bb_im[None]
    w_im = rev_re[..., None] * bb_im[None] + rev_im[..., None] * bb_re[None]
    w_src = jnp.concatenate([w_re, w_im], axis=2).transpose(0, 1, 3, 2)

    e_src = jnp.concatenate([ca_re[1:L + 1], -ca_im[1:L + 1]], axis=3)

    a1 = jnp.concatenate([pw_re[L], pw_re[L]], axis=-1)
    a2 = jnp.concatenate([-pw_im[L], pw_im[L]], axis=-1)
    d_tile = jnp.tile(d_skip.astype(F32).reshape(SSM_GROUPS // SSM_BUNDLE, 1, SSM_BUNDLE * H), (1, 1, L))
    return t_src.astype(BF16), w_src.astype(BF16), e_src.astype(BF16), a1, a2, d_tile


def _bundle_operator(dst, src_ref):
    n_l, n_g, h, x = src_ref.shape
    dst[...] = jnp.zeros(dst.shape, dst.dtype)
    for i in range(n_l):
        for g in range(n_g):
            r0 = (i * n_g + g) * h
            dst[r0:r0 + h, g * x:(g + 1) * x] = src_ref[i, g]


def _chunk_rows(u_ref, n_chunk):
    return jnp.concatenate([u_ref[pl.ds(t, n_chunk, stride=SSM_CHUNK), :] for t in range(SSM_CHUNK)], axis=1)


def _ssm_in_kernel(u_ref, w_ref, v_ref, w_scr):
    @pl.when(pl.program_id(1) == 0)
    def _():
        _bundle_operator(w_scr, w_ref)

    x = _chunk_rows(u_ref, v_ref.shape[0]).astype(BF16)
    v = jnp.dot(x, w_scr[...], preferred_element_type=F32)
    sw = v_ref.shape[2]
    for g in range(v_ref.shape[1]):
        v_ref[:, g, :] = v[:, g * sw:(g + 1) * sw]


def _ssm_scan_kernel(v_ref, a1_ref, a2_ref, o_ref, st):
    @pl.when(pl.program_id(0) == 0)
    def _():
        st[...] = jnp.zeros(st.shape, F32)

    a1 = a1_ref[...][None]
    a2 = a2_ref[...][None]
    n_chunk = v_ref.shape[1]
    half = v_ref.shape[3] // 2

    def body(c, s):
        o_ref[:, pl.ds(c, 1)] = s[:, None].astype(o_ref.dtype)
        v = v_ref[:, pl.ds(c, 1)][:, 0]
        return a1 * s + a2 * pltpu.roll(s, half, axis=2) + v

    st[...] = lax.fori_loop(0, n_chunk, body, st[...])


def _gelu_tanh(x):
    c = math.sqrt(2.0 / math.pi)
    return x * (0.5 * (1.0 + jnp.tanh(c * (x + 0.044715 * (x * x * x)))))


def _ssm_out_kernel(u_ref, t_ref, s_ref, e_ref, d_ref, y_ref, t_scr, et_scr):
    L, lw, h = SSM_CHUNK, t_ref.shape[1], SSM_GROUP

    @pl.when(pl.program_id(1) == 0)
    def _():
        _bundle_operator(et_scr, e_ref)
        same_group = (lax.broadcasted_iota(jnp.int32, (lw, lw), 0) // h
                      == lax.broadcasted_iota(jnp.int32, (lw, lw), 1) // h)
        t_scr[...] = jnp.zeros(t_scr.shape, t_scr.dtype)
        for tau in range(L):
            blk = jnp.where(same_group, t_ref[tau], jnp.zeros((lw, lw), t_ref.dtype))
            for i in range(L - tau):
                j = i + tau
                t_scr[i * lw:(i + 1) * lw, j * lw:(j + 1) * lw] = blk

    n_chunk = s_ref.shape[0]
    xf = _chunk_rows(u_ref, n_chunk)
    sp = jnp.concatenate([s_ref[:, g, :] for g in range(s_ref.shape[1])], axis=1).astype(BF16)
    y = jnp.dot(xf.astype(BF16), t_scr[...], preferred_element_type=F32)
    y = y + lax.dot_general(sp, et_scr[...], (((1,), (1,)), ((), ())), preferred_element_type=F32)
    y = _gelu_tanh(y + d_ref[0] * xf)
    for t in range(L):
        y_ref[pl.ds(t, n_chunk, stride=L), :] = y[:, t * lw:(t + 1) * lw]


def _ssm(u, ops, bt, scan_block=32, chunk_tile=256):
    t_src, w_src, e_src, a1, a2, d_tile = ops
    n, width = u.shape
    L, G, H, P2, nbg = SSM_CHUNK, SSM_GROUPS, SSM_GROUP, 2 * SSM_STATE, SSM_BUNDLE
    nb = G // nbg
    lw = nbg * H
    nc = n // L
    n_c = nc // bt
    ct = min(chunk_tile, nc)
    u_spec = pl.BlockSpec((ct * L, lw), lambda b, i: (i, b))
    src_spec = pl.BlockSpec((L, nbg, H, P2), lambda b, i: (0, b, 0, 0))
    state_spec = pl.BlockSpec((ct, nbg, P2), lambda b, i: (i, b, 0))

    v = pl.pallas_call(
        _ssm_in_kernel,
        grid=(nb, nc // ct),
        in_specs=[u_spec, src_spec],
        out_specs=state_spec,
        out_shape=jax.ShapeDtypeStruct((nc, G, P2), F32),
        scratch_shapes=[pltpu.VMEM((L * lw, nbg * P2), BF16)],
        compiler_params=_params(("parallel", "arbitrary")),
        name="ssm_chunk_state",
    )(u, w_src)

    cb = min(scan_block, n_c)
    s_prev = pl.pallas_call(
        _ssm_scan_kernel,
        grid=(n_c // cb,),
        in_specs=[pl.BlockSpec((bt, cb, G, P2), lambda c: (0, c, 0, 0)),
                  pl.BlockSpec((G, P2), lambda c: (0, 0)),
                  pl.BlockSpec((G, P2), lambda c: (0, 0))],
        out_specs=pl.BlockSpec((bt, cb, G, P2), lambda c: (0, c, 0, 0)),
        out_shape=jax.ShapeDtypeStruct((bt, n_c, G, P2), F32),
        scratch_shapes=[pltpu.VMEM((bt, G, P2), F32)],
        compiler_params=_params(("arbitrary",)),
        name="ssm_scan",
    )(v.reshape(bt, n_c, G, P2), a1, a2)

    return pl.pallas_call(
        _ssm_out_kernel,
        grid=(nb, nc // ct),
        in_specs=[u_spec,
                  pl.BlockSpec((L, lw, lw), lambda b, i: (0, b, 0)),
                  state_spec,
                  src_spec,
                  pl.BlockSpec((1, 1, L * lw), lambda b, i: (b, 0, 0))],
        out_specs=pl.BlockSpec((ct * L, lw), lambda b, i: (i, b)),
        out_shape=jax.ShapeDtypeStruct((n, width), F32),
        scratch_shapes=[pltpu.VMEM((L * lw, L * lw), BF16), pltpu.VMEM((L * lw, nbg * P2), BF16)],
        compiler_params=_params(("parallel", "arbitrary")),
        name="ssm_output",
    )(u, t_src, s_prev.reshape(nc, G, P2), e_src, d_tile)


def _postmix_kernel(a_ref, s_ref, x_ref, wg_ref, bg_ref, gs_ref, woa_ref, wos_ref, gf_ref,
                    wrh_ref, wrl_ref, br_ref, x1_ref, h_ref, lg_ref):
    sf = s_ref[...]
    gate = _sigmoid(jnp.dot(sf.astype(BF16), wg_ref[...], preferred_element_type=F32) + bg_ref[...])
    sn = _rms(sf * gate, gs_ref[...], 1e-6).astype(BF16)
    x1 = (x_ref[...]
          + jnp.dot(a_ref[...], woa_ref[...], preferred_element_type=F32)
          + jnp.dot(sn, wos_ref[...], preferred_element_type=F32))
    x1_ref[...] = x1
    h = _rms(x1, gf_ref[...], 1e-6)
    h_hi = h.astype(BF16)
    h_lo = (h - h_hi.astype(F32)).astype(BF16)
    _store_row_slabs(h_ref, 0, h_hi)
    lg_ref[...] = (jnp.dot(h_hi, wrh_ref[...], preferred_element_type=F32)
                   + jnp.dot(h_lo, wrh_ref[...], preferred_element_type=F32)
                   + jnp.dot(h_hi, wrl_ref[...], preferred_element_type=F32)
                   + br_ref[...])


def _postmix(a, s, x2d, w_glu, b_glu, g_s, w_o, g_ffn, w_r, b_r, tm=256):
    n, d = x2d.shape
    wa = a.shape[1]
    ws = s.shape[1]
    wr_hi = w_r.astype(BF16)
    wr_lo = (w_r - wr_hi.astype(F32)).astype(BF16)
    row = lambda i: (i, 0)
    fixed = lambda i: (0, 0)
    return pl.pallas_call(
        _postmix_kernel,
        grid=(n // tm,),
        in_specs=[
            pl.BlockSpec((tm, wa), row),
            pl.BlockSpec((tm, ws), row),
            pl.BlockSpec((tm, d), row),
            pl.BlockSpec((ws, ws), fixed),
            pl.BlockSpec((1, ws), fixed),
            pl.BlockSpec((1, ws), fixed),
            pl.BlockSpec((wa, d), fixed),
            pl.BlockSpec((ws, d), fixed),
            pl.BlockSpec((1, d), fixed),
            pl.BlockSpec((d, ROUTER_LANES), fixed),
            pl.BlockSpec((d, ROUTER_LANES), fixed),
            pl.BlockSpec((1, ROUTER_LANES), fixed),
        ],
        out_specs=[pl.BlockSpec((tm, d), row), pl.BlockSpec((tm * (d // SLAB_COLS), LANES), row),
                   pl.BlockSpec((tm, ROUTER_LANES), row)],
        out_shape=[jax.ShapeDtypeStruct((n, d), F32), jax.ShapeDtypeStruct((n * (d // SLAB_COLS), LANES), jnp.uint32),
                   jax.ShapeDtypeStruct((n, ROUTER_LANES), F32)],
        compiler_params=_params(("parallel",)),
        name="postmix",
    )(a, s, x2d, w_glu.astype(BF16), b_glu.reshape(1, ws).astype(F32), g_s.reshape(1, ws).astype(F32),
      w_o[:wa].astype(BF16), w_o[wa:].astype(BF16), g_ffn.reshape(1, d).astype(F32),
      wr_hi, wr_lo, b_r.reshape(1, ROUTER_LANES).astype(F32))


def _route(logits, rows, slab_rows):
    n_tok = logits.shape[0]
    lg = logits[:, :N_EXPERT_GROUPS]
    le = logits[:, N_EXPERT_GROUPS:N_EXPERT_GROUPS + N_EXPERTS].reshape(
        n_tok, N_EXPERT_GROUPS, EXPERTS_PER_GROUP)
    pg = jax.nn.softmax(lg, axis=-1)
    gsel = jnp.argmax(lg, axis=-1).astype(jnp.int32)
    gate_g = jnp.max(pg, axis=-1, keepdims=True)
    sel = gsel[:, None] == jnp.arange(N_EXPERT_GROUPS, dtype=jnp.int32)[None, :]
    le_sel = jnp.sum(jnp.where(sel[:, :, None], le, 0.0), axis=1)
    pe = jax.nn.softmax(le_sel, axis=-1)
    top_p, top_i = lax.top_k(pe, TOP_K)
    w = gate_g * top_p / jnp.sum(top_p, axis=-1, keepdims=True)
    eid = gsel[:, None] * EXPERTS_PER_GROUP + top_i.astype(jnp.int32)

    n_assign = n_tok * TOP_K
    flat_e = eid.reshape(-1)
    counts = jnp.sum((flat_e[None, :] == jnp.arange(N_EXPERTS, dtype=jnp.int32)[:, None])
                     .astype(jnp.int32), axis=1)
    n_blk_e = (counts + rows - 1) // rows
    blk_end = jnp.cumsum(n_blk_e)
    blk_first = blk_end - n_blk_e
    start = jnp.cumsum(counts) - counts
    order = jnp.argsort(flat_e, stable=True).astype(jnp.int32)
    n_blk = (n_assign + rows - 1) // rows + N_EXPERTS
    blk = jnp.arange(n_blk, dtype=jnp.int32)
    blk_e = jnp.minimum(jnp.sum((blk[:, None] >= blk_end[None, :]).astype(jnp.int32), axis=1),
                        N_EXPERTS - 1)
    used = blk < blk_end[-1]
    experts = jnp.arange(N_EXPERTS, dtype=jnp.int32)
    is_e = blk_e[:, None] == experts[None, :]

    def per_block(table):
        return jnp.sum(jnp.where(is_e, table[None, :], 0), axis=1)

    in_e = (blk - per_block(blk_first)) * rows
    blk_cnt = jnp.where(used, jnp.clip(per_block(counts) - in_e, 0, rows), 0).astype(jnp.int32)
    blk_src = jnp.where(used, per_block(start) + in_e, 0).astype(jnp.int32)
    last_e = jnp.max(jnp.where(counts > 0, experts, 0))
    blk_e = jnp.where(used, blk_e, last_e)
    row_tok = order // TOP_K
    row_dst = (order % TOP_K) * n_tok + row_tok
    pad = jnp.zeros((rows,), jnp.int32)
    return (w, jnp.concatenate([row_tok * slab_rows, pad]), jnp.concatenate([row_dst * slab_rows, pad]),
            blk_e, blk_cnt, blk_src)


def _expert_kernel(blk_e_ref, blk_cnt_ref, blk_src_ref, tok_ref, dst_ref,
                   h_hbm, w1_ref, w3_ref, w2_ref, y_hbm, xbuf, ybuf, gsem, ssem):
    del blk_e_ref
    b = pl.program_id(0)
    nb = pl.num_programs(0)
    slot = b % N_SLOTS
    slot_m1 = (b + N_SLOTS - 1) % N_SLOTS
    slot_p1 = (b + 1) % N_SLOTS
    spr = w1_ref.shape[1] // SLAB_COLS
    rows = xbuf.shape[1] // spr
    dump = y_hbm.shape[0] - N_SLOTS * rows * spr
    cnt = blk_cnt_ref[b]
    prev = jnp.maximum(b - 1, 0)
    prev_cnt = jnp.where(b >= 1, blk_cnt_ref[prev], 0)

    def gather_copy(src, r, sl):
        return pltpu.make_async_copy(h_hbm.at[pl.ds(pl.multiple_of(tok_ref[src + r], spr), spr)],
                                     xbuf.at[sl, pl.ds(r * spr, spr)], gsem.at[sl])

    def scatter_copy(src, n_valid, r, sl):
        dst = jnp.where(r < n_valid, dst_ref[src + r], dump + (sl * rows + r) * spr)
        return pltpu.make_async_copy(ybuf.at[sl, pl.ds(r * spr, spr)],
                                     y_hbm.at[pl.ds(pl.multiple_of(dst, spr), spr)], ssem.at[sl])

    def start_gathers(blk, sl, lo=0, hi=None):
        src = blk_src_ref[blk]
        for r in range(lo, rows if hi is None else hi):
            gather_copy(src, r, sl).start()

    def wait_gathers(sl):
        for r in range(rows):
            gather_copy(0, r, sl).wait()

    def start_scatters(blk, n_valid, sl, lo=0, hi=None):
        src = blk_src_ref[blk]
        for r in range(lo, rows if hi is None else hi):
            scatter_copy(src, n_valid, r, sl).start()

    def wait_scatters(sl):
        for r in range(rows):
            scatter_copy(0, 0, r, sl).wait()

    @pl.when(b == 0)
    def _():
        ybuf[...] = jnp.zeros(ybuf.shape, ybuf.dtype)
        start_gathers(0, 0)
        start_gathers(1, 1)
        for sl in range(N_SLOTS - 1):
            start_scatters(0, 0, sl)
        for sl in range(N_SLOTS - 1):
            wait_scatters(sl)

    @pl.when(cnt > 0)
    def _():
        @pl.when(b >= 2)
        def _():
            wait_scatters(slot)

        wait_gathers(slot)
        x = _load_row_slabs(xbuf.at[slot], 0, rows, spr).astype(BF16)
        f = w1_ref.shape[2]
        n_f = f // MXU_WIDTH
        n_part = 3 * n_f
        bounds = [rows * i // n_part for i in range(n_part + 1)]
        part = iter(range(n_part))

        def issue_part():
            i = next(part)
            start_gathers(jnp.minimum(b + 2, nb - 1), slot_m1, bounds[i], bounds[i + 1])
            start_scatters(prev, prev_cnt, slot_m1, bounds[i], bounds[i + 1])

        acts = []
        for c in range(n_f):
            cols = slice(c * MXU_WIDTH, (c + 1) * MXU_WIDTH)
            issue_part()
            h1 = jnp.dot(x, w1_ref[0, :, cols].astype(BF16), preferred_element_type=F32)
            issue_part()
            h3 = jnp.dot(x, w3_ref[0, :, cols].astype(BF16), preferred_element_type=F32)
            acts.append((h1 * _sigmoid(h1) * h3).astype(BF16))
        y = None
        for c in range(n_f):
            issue_part()
            yc = jnp.dot(acts[c], w2_ref[0, c * MXU_WIDTH:(c + 1) * MXU_WIDTH, :].astype(BF16),
                         preferred_element_type=F32)
            y = yc if y is None else y + yc
        _store_row_slabs(ybuf.at[slot], 0, y)

    @pl.when((cnt == 0) & (prev_cnt > 0))
    def _():
        wait_gathers(slot)
        wait_gathers(slot_p1)

        @pl.when(b >= 2)
        def _():
            wait_scatters(slot)

        wait_scatters(slot_p1)
        start_scatters(prev, prev_cnt, slot_m1)
        wait_scatters(slot_m1)


def _experts(h, w1, w3, w2, row_tok, row_dst, blk_e, blk_cnt, blk_src, rows):
    d, f = w1.shape[1], w1.shape[2]
    spr = d // SLAB_COLS
    n_blk = blk_e.shape[0]
    wmap = lambda b, be, bc, bs, rt, rd: (be[b], 0, 0)
    grid_spec = pltpu.PrefetchScalarGridSpec(
        num_scalar_prefetch=5,
        grid=(n_blk,),
        in_specs=[
            pl.BlockSpec(memory_space=pl.ANY),
            pl.BlockSpec((1, d, f), wmap),
            pl.BlockSpec((1, d, f), wmap),
            pl.BlockSpec((1, f, d), wmap),
        ],
        out_specs=pl.BlockSpec(memory_space=pl.ANY),
        scratch_shapes=[
            pltpu.VMEM((N_SLOTS, rows * spr, LANES), jnp.uint32),
            pltpu.VMEM((N_SLOTS, rows * spr, LANES), jnp.uint32),
            pltpu.SemaphoreType.DMA((N_SLOTS,)),
            pltpu.SemaphoreType.DMA((N_SLOTS,)),
        ],
    )
    return pl.pallas_call(
        _expert_kernel,
        grid_spec=grid_spec,
        out_shape=jax.ShapeDtypeStruct((h.shape[0] * TOP_K + N_SLOTS * rows * spr, LANES), jnp.uint32),
        compiler_params=_params(("arbitrary",), disable_bounds_checks=True),
        name="experts",
    )(blk_e, blk_cnt, blk_src, row_tok, row_dst, h, w1, w3, w2)


def _ple_kernel(x1_ref, y0_ref, y1_ref, w_ref, p_ref, gp_ref, wg_ref, wp_ref, gf_ref, o_ref, *, final):
    w = w_ref[...]
    tm, d = x1_ref.shape
    spr = d // SLAB_COLS
    y0 = _load_row_slabs(y0_ref, 0, tm, spr)
    y1 = _load_row_slabs(y1_ref, 0, tm, spr)
    x2 = x1_ref[...] + w[:, 0:1] * y0 + w[:, 1:2] * y1
    hn = _rms(x2, gp_ref[...], 1e-6).astype(BF16)
    gate = _sigmoid(jnp.dot(hn, wg_ref[...], preferred_element_type=F32))
    pp = jnp.dot(p_ref[...].astype(BF16), wp_ref[...], preferred_element_type=F32)
    x3 = x2 + gate * pp
    o_ref[...] = _rms(x3, gf_ref[...], 1e-6) if final else x3


def _ple(x1, y2, w, p2d, g_ple, w_gate, w_proj, g_final, final, tm=256):
    n, d = x1.shape
    pd = p2d.shape[1]
    row = lambda i: (i, 0)
    fixed = lambda i: (0, 0)
    return pl.pallas_call(
        functools.partial(_ple_kernel, final=final),
        grid=(n // tm,),
        in_specs=[
            pl.BlockSpec((tm, d), row),
            pl.BlockSpec((tm * (d // SLAB_COLS), LANES), row),
            pl.BlockSpec((tm * (d // SLAB_COLS), LANES), lambda i: (n // tm + i, 0)),
            pl.BlockSpec((tm, TOP_K), row),
            pl.BlockSpec((tm, pd), row),
            pl.BlockSpec((1, d), fixed),
            pl.BlockSpec((d, d), fixed),
            pl.BlockSpec((pd, d), fixed),
            pl.BlockSpec((1, d), fixed),
        ],
        out_specs=pl.BlockSpec((tm, d), row),
        out_shape=jax.ShapeDtypeStruct((n, d), F32),
        compiler_params=_params(("parallel",)),
        name="ple_final",
    )(x1, y2, y2, w, p2d, g_ple.reshape(1, d).astype(F32), w_gate.astype(BF16), w_proj.astype(BF16),
      g_final.reshape(1, d).astype(F32))


def kernel(x, p, rel_bias, g_mix, w_in, lam_q1, lam_k1, lam_q2, lam_k2, subln_g, ssm_lam_re, ssm_lam_im, ssm_log_dt, ssm_b_re, ssm_b_im, ssm_c_re, ssm_c_im, ssm_d, w_glu, b_glu, ssm_norm_g, w_o, g_ffn, w_router_g, b_router_g, w_router_e, b_router_e, w1, w3, w2, g_ple, w_ple_gate, w_ple_proj, g_final):
    bt, s_len, d = x.shape
    n = bt * s_len
    depth = g_mix.shape[0]
    attn_w = N_HEADS * DV
    xc = x.reshape(n, d).astype(F32)
    for i in range(depth):
        lam_init = 0.8 - 0.6 * math.exp(-0.3 * i)
        col_scale = jnp.concatenate([jnp.full((attn_w,), LOG2E * DK ** -0.5, F32),
                                     jnp.ones((w_in.shape[2] - attn_w,), F32)])
        w_in_b = (w_in[i].astype(F32) * col_scale[None, :]).astype(BF16)
        z, u = _inproj(xc, g_mix[i].astype(F32), w_in_b)
        z3 = z.reshape(bt, s_len, z.shape[1])

        lam = (jnp.exp(jnp.sum(lam_q1[i].astype(F32) * lam_k1[i].astype(F32)))
               - jnp.exp(jnp.sum(lam_q2[i].astype(F32) * lam_k2[i].astype(F32))) + lam_init)
        a = _attention(z3, rel_bias, lam, subln_g[i], lam_init)

        ops = _ssm_operators(ssm_lam_re[i], ssm_lam_im[i], ssm_log_dt[i], ssm_b_re[i], ssm_b_im[i],
                             ssm_c_re[i], ssm_c_im[i], ssm_d[i])
        s = _ssm(u, ops, bt)

        w_r = jnp.zeros((d, ROUTER_LANES), F32)
        w_r = w_r.at[:, :N_EXPERT_GROUPS].set(w_router_g[i].astype(F32))
        w_r = w_r.at[:, N_EXPERT_GROUPS:N_EXPERT_GROUPS + N_EXPERTS].set(w_router_e[i].astype(F32))
        b_r = jnp.zeros((ROUTER_LANES,), F32)
        b_r = b_r.at[:N_EXPERT_GROUPS].set(b_router_g[i].astype(F32))
        b_r = b_r.at[N_EXPERT_GROUPS:N_EXPERT_GROUPS + N_EXPERTS].set(b_router_e[i].astype(F32))
        x1, h, logits = _postmix(a.reshape(n, attn_w), s, xc, w_glu[i], b_glu[i],
                                 ssm_norm_g[i], w_o[i], g_ffn[i], w_r, b_r)

        gate_w, row_tok, row_dst, blk_e, blk_cnt, blk_src = _route(logits, EXPERT_ROWS, d // SLAB_COLS)
        y2 = _experts(h, w1[i], w3[i], w2[i],
                      row_tok, row_dst, blk_e, blk_cnt, blk_src, EXPERT_ROWS)
        xc = _ple(x1, y2, gate_w.astype(F32), p[i].reshape(n, -1).astype(F32),
                  g_ple[i], w_ple_gate[i], w_ple_proj[i], g_final, final=(i == depth - 1))
    return xc.reshape(bt, s_len, d)
```

```python
import functools
import math

import jax
import jax.numpy as jnp
from jax import lax
from jax.experimental import pallas as pl
from jax.experimental.pallas import tpu as pltpu

F32 = jnp.float32
BF16 = jnp.bfloat16
HIGHEST = lax.Precision.HIGHEST

N_HEADS = 8
DK = 64
DV = 128
N_BUCKETS = 32
MAX_DISTANCE = 128
SSM_GROUP = 16
SSM_GROUPS = 64
SSM_STATE = 64
N_EXPERT_GROUPS = 4
EXPERTS_PER_GROUP = 8
N_EXPERTS = 32
TOP_K = 2
MASK_VALUE = -1e30
LOG2E = math.log2(math.e)

SSM_CHUNK = 16
SSM_BUNDLE = 8
ATTN_TQ = 1024
ONES_ROWS = 16
ATTN_CB = 256
ATTN_TK = 256
EXPERT_ROWS = 256
N_SLOTS = 3
MXU_WIDTH = 256
LANES = 128
SLAB_COLS = 2 * LANES
ROUTER_LANES = 128
VMEM_LIMIT = 56 << 20


def _params(semantics, **kw):
    return pltpu.CompilerParams(dimension_semantics=semantics, vmem_limit_bytes=VMEM_LIMIT, **kw)


def _rms(x, g, eps):
    return x * lax.rsqrt(jnp.mean(x * x, axis=-1, keepdims=True) + eps) * g


def _sigmoid(x):
    return 1.0 / (1.0 + jnp.exp(-x))


def _store_row_slabs(ref, row0, x):
    n, d = x.shape
    spr = d // (2 * LANES)
    bits = lax.bitcast_convert_type(x.astype(BF16).astype(F32), jnp.uint32)
    for j in range(spr):
        lo = lax.shift_right_logical(bits[:, j * LANES:(j + 1) * LANES], jnp.uint32(16))
        hi = bits[:, d // 2 + j * LANES:d // 2 + (j + 1) * LANES] & jnp.uint32(0xFFFF0000)
        ref[pl.ds(row0 * spr + j, n, stride=spr), :] = lo | hi


def _load_row_slabs(ref, row0, n, spr):
    words = [ref[pl.ds(row0 * spr + j, n, stride=spr), :] for j in range(spr)]
    lo = [lax.bitcast_convert_type(lax.shift_left(w, jnp.uint32(16)), F32) for w in words]
    hi = [lax.bitcast_convert_type(w & jnp.uint32(0xFFFF0000), F32) for w in words]
    return jnp.concatenate(lo + hi, axis=1)


def _inproj_kernel(x_ref, g_ref, w_ref, z_ref, u_ref, h_scr):
    j = pl.program_id(1)
    last = pl.num_programs(1) - 1

    @pl.when(j == 0)
    def _():
        h_scr[...] = _rms(x_ref[...], g_ref[...], 1e-6).astype(BF16)

    acc = jnp.dot(h_scr[...], w_ref[...], preferred_element_type=F32)

    @pl.when(j < last)
    def _():
        z_ref[...] = acc.astype(z_ref.dtype)

    @pl.when(j == last)
    def _():
        u_ref[...] = acc


def _inproj(x2d, g, w, tm=512, tn=1024):
    n, d = x2d.shape
    pw = w.shape[1]
    n_col = pw // tn
    return pl.pallas_call(
        _inproj_kernel,
        grid=(n // tm, n_col),
        in_specs=[
            pl.BlockSpec((tm, d), lambda i, j: (i, 0)),
            pl.BlockSpec((1, d), lambda i, j: (0, 0)),
            pl.BlockSpec((d, tn), lambda i, j: (0, j)),
        ],
        out_specs=[pl.BlockSpec((tm, tn), lambda i, j: (i, jnp.minimum(j, n_col - 2))),
                   pl.BlockSpec((tm, tn), lambda i, j: (i, 0))],
        out_shape=[jax.ShapeDtypeStruct((n, pw - tn), BF16), jax.ShapeDtypeStruct((n, tn), F32)],
        scratch_shapes=[pltpu.VMEM((tm, d), BF16)],
        compiler_params=_params(("parallel", "arbitrary")),
        name="inproj",
    )(x2d, g.reshape(1, d), w)


def _attn_kernel(lam_ref, q_ref, k_ref, v_ref, bias_ref, g_ref, o_ref, qs, vt, *state,
                 tq, tk, cb, out_scale):
    n_blk = 2 * tq // cb
    m_scr, acc, s_scr = (state[i * n_blk:(i + 1) * n_blk] for i in range(3))
    qi = pl.program_id(2)
    n_sub = tq // tk
    n_kv = v_ref.shape[1] // tk

    @pl.when(qi == 0)
    def _():
        for c in range(n_kv):
            vt[c, 0:DV, :] = v_ref[0, c * tk:(c + 1) * tk, :].astype(F32).T.astype(BF16)
            vt[c, DV:DV + ONES_ROWS, :] = jnp.ones((ONES_ROWS, tk), BF16)

    q = q_ref[0]
    lane = lax.broadcasted_iota(jnp.int32, q.shape, 1)
    zero = jnp.zeros_like(q)
    qs[0:tq, :] = jnp.where(lane < DK, q, zero)
    qs[tq:2 * tq, :] = jnp.where(lane >= DK, q, zero)
    for c in range(n_blk):
        m_scr[c][...] = jnp.full(m_scr[c].shape, MASK_VALUE, F32)
        acc[c][...] = jnp.zeros(acc[c].shape, F32)

    all_blocks = tuple(range(n_blk))

    def visible(r):
        return tuple(c for c in all_blocks if (c * cb) % tq + cb > (r - 1) * tk)

    def scores(j, blocks):
        kj = k_ref[0, pl.ds(pl.multiple_of(j * tk, tk), tk), :]
        return [lax.dot_general(kj, qs[c * cb:(c + 1) * cb, :], (((1,), (1,)), ((), ())),
                                preferred_element_type=F32) for c in blocks]

    def consume(j, bias, blocks, next_blocks):
        nxt = scores(j + 1, next_blocks) if next_blocks else None
        vtj = vt[j]
        alphas, ps = [], []
        for c in blocks:
            s = s_scr[c][...]
            if bias is not None:
                qb = (c * cb) % tq
                s = s + bias[:, qb:qb + cb]
            m_old = m_scr[c][...]
            m_new = jnp.maximum(m_old, jnp.max(s, axis=0, keepdims=True))
            alpha = jnp.exp2(m_old - m_new)
            p = jnp.exp2(s - m_new)
            m_scr[c][...] = m_new
            alphas.append(alpha)
            ps.append(p.astype(BF16))
        pvs = [jnp.dot(vtj, p, preferred_element_type=F32) for p in ps]
        for c, alpha, pv in zip(blocks, alphas, pvs):
            acc[c][...] = alpha * acc[c][...] + pv
        if next_blocks:
            for c, sc in zip(next_blocks, nxt):
                s_scr[c][...] = sc

    first = qi * n_sub - 1
    for c, sc in zip(all_blocks, scores(0, all_blocks)):
        s_scr[c][...] = sc

    def far_pair(i, carry):
        consume(2 * i, None, all_blocks, all_blocks)
        consume(2 * i + 1, None, all_blocks, all_blocks)
        return carry

    lax.fori_loop(0, jnp.maximum(first, 0) // 2, far_pair, 0)

    @pl.when(qi >= 1)
    def _():
        consume(first - 1, None, all_blocks, all_blocks)
        consume(first, bias_ref[0, 0], all_blocks, all_blocks)

    for r in range(1, n_sub + 1):
        consume(first + r, bias_ref[0, r], visible(r), visible(r + 1) if r < n_sub else None)

    lam = lam_ref[0, 0]
    half = n_blk // 2
    for b in range(half):
        a1, a2 = acc[b], acc[half + b]
        ot = (a1[0:DV, :] / a1[DV:DV + 1, :] - lam * (a2[0:DV, :] / a2[DV:DV + 1, :]))
        ot = ot * lax.rsqrt(jnp.mean(ot * ot, axis=0, keepdims=True) + 1e-5)
        o_ref[0, b * cb:(b + 1) * cb, :] = (ot.T * (g_ref[...] * out_scale)).astype(o_ref.dtype)


def _t5_bucket(n):
    n = jnp.maximum(n, 0)
    max_exact = N_BUCKETS // 2
    nf = jnp.maximum(n, 1).astype(F32)
    large = max_exact + (jnp.log(nf / max_exact) / math.log(MAX_DISTANCE / max_exact)
                         * (N_BUCKETS - max_exact)).astype(jnp.int32)
    large = jnp.minimum(large, N_BUCKETS - 1)
    return jnp.where(n < max_exact, n, large)


def _attn_bias_tiles(rel_bias, tq, tk):
    assert tk >= MAX_DISTANCE and tq % (2 * tk) == 0
    table = rel_bias.astype(F32)
    rel_table = (table - table[N_BUCKETS - 1][None, :]) * LOG2E
    r = jnp.arange(tk, dtype=jnp.int32)[:, None]
    c = jnp.arange(tq, dtype=jnp.int32)[None, :]
    tiles = []
    for t in range(tq // tk + 1):
        dist = c - (r + (t - 1) * tk)
        onehot = (_t5_bucket(dist)[:, :, None] == jnp.arange(N_BUCKETS, dtype=jnp.int32)).astype(F32)
        b = jnp.einsum('rcn,nh->hrc', onehot, rel_table, precision=HIGHEST)
        tiles.append(jnp.where((dist >= 0)[None], b, MASK_VALUE))
    return jnp.stack(tiles, axis=1)


def _attention(z3, rel_bias, lam, subln_g, lam_init, tq=ATTN_TQ, tk=ATTN_TK, cb=ATTN_CB):
    bt, s_len, _ = z3.shape
    bias = _attn_bias_tiles(rel_bias, tq, tk)
    n_sp = bias.shape[1]
    kern = functools.partial(_attn_kernel, tq=tq, tk=tk, cb=cb, out_scale=1.0 - lam_init)
    n_blk = 2 * tq // cb
    return pl.pallas_call(
        kern,
        grid=(bt, N_HEADS, s_len // tq),
        in_specs=[
            pl.BlockSpec(memory_space=pltpu.SMEM),
            pl.BlockSpec((1, tq, 2 * DK), lambda b, h, i: (b, i, h)),
            pl.BlockSpec((1, s_len, 2 * DK), lambda b, h, i: (b, 0, N_HEADS + h)),
            pl.BlockSpec((1, s_len, DV), lambda b, h, i: (b, 0, 2 * N_HEADS + h)),
            pl.BlockSpec((1, n_sp, tk, tq), lambda b, h, i: (h, 0, 0, 0)),
            pl.BlockSpec((1, DV), lambda b, h, i: (0, 0)),
        ],
        out_specs=pl.BlockSpec((1, tq, DV), lambda b, h, i: (b, i, h)),
        out_shape=jax.ShapeDtypeStruct((bt, s_len, N_HEADS * DV), BF16),
        scratch_shapes=[
            pltpu.VMEM((2 * tq, 2 * DK), BF16),
            pltpu.VMEM((s_len // tk, DV + ONES_ROWS, tk), BF16),
        ] + [pltpu.VMEM((1, cb), F32)] * n_blk + [pltpu.VMEM((DV + ONES_ROWS, cb), F32)] * n_blk
        + [pltpu.VMEM((tk, cb), F32)] * n_blk,
        compiler_params=_params(("parallel", "parallel", "arbitrary")),
        name="diff_attention",
    )(lam.reshape(1, 1).astype(F32), z3, z3, z3, bias, subln_g.reshape(1, DV).astype(F32))


def _ssm_operators(lam_re, lam_im, log_dt, b_re, b_im, c_re, c_im, d_skip):
    L, H, P = SSM_CHUNK, SSM_GROUP, SSM_STATE
    lre = lam_re.astype(F32)
    lim = lam_im.astype(F32)
    dt = jnp.exp(log_dt.astype(F32))[:, None]
    mag = jnp.exp(lre * dt)
    ab_re = mag * jnp.cos(lim * dt)
    ab_im = mag * jnp.sin(lim * dt)
    den = lre * lre + lim * lim
    nr, ni = ab_re - 1.0, ab_im
    cr = ((nr * lre + ni * lim) / den)[..., None]
    ci = ((ni * lre - nr * lim) / den)[..., None]
    bre = b_re.astype(F32)
    bim = b_im.astype(F32)
    bb_re = cr * bre - ci * bim
    bb_im = cr * bim + ci * bre
    cre = c_re.astype(F32)
    cim = c_im.astype(F32)

    tau = jnp.arange(L + 1, dtype=F32)[:, None, None]
    pw_mag = jnp.exp(tau * (lre * dt)[None])
    pw_re = pw_mag * jnp.cos(tau * (lim * dt)[None])
    pw_im = pw_mag * jnp.sin(tau * (lim * dt)[None])

    ca_re = cre[None] * pw_re[:, :, None, :] - cim[None] * pw_im[:, :, None, :]
    ca_im = cre[None] * pw_im[:, :, None, :] + cim[None] * pw_re[:, :, None, :]
    bbt_re = bb_re.transpose(0, 2, 1)[None, :, None]
    bbt_im = bb_im.transpose(0, 2, 1)[None, :, None]
    k_tau = jnp.sum(ca_re[:L, :, :, None, :] * bbt_re - ca_im[:L, :, :, None, :] * bbt_im, axis=-1)
    ti = jnp.arange(L)
    t_src = jnp.tile(k_tau.transpose(0, 1, 3, 2).reshape(L, SSM_GROUPS * H, H), (1, 1, SSM_BUNDLE))

    rev_re = pw_re[L - 1 - ti]
    rev_im = pw_im[L - 1 - ti]
    w_re = rev_re[..., None] * bb_re[None] - rev_im[..., None] * bb_im[None]
    w_im = rev_re[..., None] * bb_im[None] + rev_im[..., None] * bb_re[None]
    w_src = jnp.concatenate([w_re, w_im], axis=2).transpose(0, 1, 3, 2)

    e_src = jnp.concatenate([ca_re[1:L + 1], -ca_im[1:L + 1]], axis=3)

    a1 = jnp.concatenate([pw_re[L], pw_re[L]], axis=-1)
    a2 = jnp.concatenate([-pw_im[L], pw_im[L]], axis=-1)
    d_tile = jnp.tile(d_skip.astype(F32).reshape(SSM_GROUPS // SSM_BUNDLE, 1, SSM_BUNDLE * H), (1, 1, L))
    return t_src.astype(BF16), w_src.astype(BF16), e_src.astype(BF16), a1, a2, d_tile


def _bundle_operator(dst, src_ref):
    n_l, n_g, h, x = src_ref.shape
    dst[...] = jnp.zeros(dst.shape, dst.dtype)
    for i in range(n_l):
        for g in range(n_g):
            r0 = (i * n_g + g) * h
            dst[r0:r0 + h, g * x:(g + 1) * x] = src_ref[i, g]


def _chunk_rows(u_ref, n_chunk):
    return jnp.concatenate([u_ref[pl.ds(t, n_chunk, stride=SSM_CHUNK), :] for t in range(SSM_CHUNK)], axis=1)


def _ssm_in_kernel(u_ref, w_ref, v_ref, w_scr):
    @pl.when(pl.program_id(1) == 0)
    def _():
        _bundle_operator(w_scr, w_ref)

    x = _chunk_rows(u_ref, v_ref.shape[0]).astype(BF16)
    v = jnp.dot(x, w_scr[...], preferred_element_type=F32)
    sw = v_ref.shape[2]
    for g in range(v_ref.shape[1]):
        v_ref[:, g, :] = v[:, g * sw:(g + 1) * sw]


def _ssm_scan_kernel(v_ref, a1_ref, a2_ref, o_ref, st):
    @pl.when(pl.program_id(0) == 0)
    def _():
        st[...] = jnp.zeros(st.shape, F32)

    a1 = a1_ref[...][None]
    a2 = a2_ref[...][None]
    n_chunk = v_ref.shape[1]
    half = v_ref.shape[3] // 2

    def body(c, s):
        o_ref[:, pl.ds(c, 1)] = s[:, None].astype(o_ref.dtype)
        v = v_ref[:, pl.ds(c, 1)][:, 0]
        return a1 * s + a2 * pltpu.roll(s, half, axis=2) + v

    st[...] = lax.fori_loop(0, n_chunk, body, st[...])


def _gelu_tanh(x):
    c = math.sqrt(2.0 / math.pi)
    return x * (0.5 * (1.0 + jnp.tanh(c * (x + 0.044715 * (x * x * x)))))


def _ssm_out_kernel(u_ref, t_ref, s_ref, e_ref, d_ref, y_ref, t_scr, et_scr):
    L, lw, h = SSM_CHUNK, t_ref.shape[1], SSM_GROUP

    @pl.when(pl.program_id(1) == 0)
    def _():
        _bundle_operator(et_scr, e_ref)
        same_group = (lax.broadcasted_iota(jnp.int32, (lw, lw), 0) // h
                      == lax.broadcasted_iota(jnp.int32, (lw, lw), 1) // h)
        t_scr[...] = jnp.zeros(t_scr.shape, t_scr.dtype)
        for tau in range(L):
            blk = jnp.where(same_group, t_ref[tau], jnp.zeros((lw, lw), t_ref.dtype))
            for i in range(L - tau):
                j = i + tau
                t_scr[i * lw:(i + 1) * lw, j * lw:(j + 1) * lw] = blk

    n_chunk = s_ref.shape[0]
    xf = _chunk_rows(u_ref, n_chunk)
    sp = jnp.concatenate([s_ref[:, g, :] for g in range(s_ref.shape[1])], axis=1).astype(BF16)
    xb = xf.astype(BF16)
    step = MXU_WIDTH
    y = jnp.concatenate([jnp.dot(xb[:, :c + step], t_scr[:c + step, c:c + step], preferred_element_type=F32)
                         for c in range(0, L * lw, step)], axis=1)
    y = y + lax.dot_general(sp, et_scr[...], (((1,), (1,)), ((), ())), preferred_element_type=F32)
    y = _gelu_tanh(y + d_ref[0] * xf)
    for t in range(L):
        y_ref[pl.ds(t, n_chunk, stride=L), :] = y[:, t * lw:(t + 1) * lw]


def _ssm(u, ops, bt, scan_block=32, chunk_tile=256):
    t_src, w_src, e_src, a1, a2, d_tile = ops
    n, width = u.shape
    L, G, H, P2, nbg = SSM_CHUNK, SSM_GROUPS, SSM_GROUP, 2 * SSM_STATE, SSM_BUNDLE
    nb = G // nbg
    lw = nbg * H
    nc = n // L
    n_c = nc // bt
    ct = min(chunk_tile, nc)
    u_spec = pl.BlockSpec((ct * L, lw), lambda b, i: (i, b))
    src_spec = pl.BlockSpec((L, nbg, H, P2), lambda b, i: (0, b, 0, 0))
    state_spec = pl.BlockSpec((ct, nbg, P2), lambda b, i: (i, b, 0))

    v = pl.pallas_call(
        _ssm_in_kernel,
        grid=(nb, nc // ct),
        in_specs=[u_spec, src_spec],
        out_specs=state_spec,
        out_shape=jax.ShapeDtypeStruct((nc, G, P2), F32),
        scratch_shapes=[pltpu.VMEM((L * lw, nbg * P2), BF16)],
        compiler_params=_params(("parallel", "arbitrary")),
        name="ssm_chunk_state",
    )(u, w_src)

    cb = min(scan_block, n_c)
    s_prev = pl.pallas_call(
        _ssm_scan_kernel,
        grid=(n_c // cb,),
        in_specs=[pl.BlockSpec((bt, cb, G, P2), lambda c: (0, c, 0, 0)),
                  pl.BlockSpec((G, P2), lambda c: (0, 0)),
                  pl.BlockSpec((G, P2), lambda c: (0, 0))],
        out_specs=pl.BlockSpec((bt, cb, G, P2), lambda c: (0, c, 0, 0)),
        out_shape=jax.ShapeDtypeStruct((bt, n_c, G, P2), F32),
        scratch_shapes=[pltpu.VMEM((bt, G, P2), F32)],
        compiler_params=_params(("arbitrary",)),
        name="ssm_scan",
    )(v.reshape(bt, n_c, G, P2), a1, a2)

    return pl.pallas_call(
        _ssm_out_kernel,
        grid=(nb, nc // ct),
        in_specs=[u_spec,
                  pl.BlockSpec((L, lw, lw), lambda b, i: (0, b, 0)),
                  state_spec,
                  src_spec,
                  pl.BlockSpec((1, 1, L * lw), lambda b, i: (b, 0, 0))],
        out_specs=pl.BlockSpec((ct * L, lw), lambda b, i: (i, b)),
        out_shape=jax.ShapeDtypeStruct((n, width), F32),
        scratch_shapes=[pltpu.VMEM((L * lw, L * lw), BF16), pltpu.VMEM((L * lw, nbg * P2), BF16)],
        compiler_params=_params(("parallel", "arbitrary")),
        name="ssm_output",
    )(u, t_src, s_prev.reshape(nc, G, P2), e_src, d_tile)


def _postmix_kernel(a_ref, s_ref, x_ref, wg_ref, bg_ref, gs_ref, woa_ref, wos_ref, gf_ref,
                    wrh_ref, wrl_ref, br_ref, x1_ref, h_ref, lg_ref):
    sf = s_ref[...]
    gate = _sigmoid(jnp.dot(sf.astype(BF16), wg_ref[...], preferred_element_type=F32) + bg_ref[...])
    sn = _rms(sf * gate, gs_ref[...], 1e-6).astype(BF16)
    x1 = (x_ref[...]
          + jnp.dot(a_ref[...], woa_ref[...], preferred_element_type=F32)
          + jnp.dot(sn, wos_ref[...], preferred_element_type=F32))
    x1_ref[...] = x1
    h = _rms(x1, gf_ref[...], 1e-6)
    h_hi = h.astype(BF16)
    h_lo = (h - h_hi.astype(F32)).astype(BF16)
    _store_row_slabs(h_ref, 0, h_hi)
    lg_ref[...] = (jnp.dot(h_hi, wrh_ref[...], preferred_element_type=F32)
                   + jnp.dot(h_lo, wrh_ref[...], preferred_element_type=F32)
                   + jnp.dot(h_hi, wrl_ref[...], preferred_element_type=F32)
                   + br_ref[...])


def _postmix(a, s, x2d, w_glu, b_glu, g_s, w_o, g_ffn, w_r, b_r, tm=256):
    n, d = x2d.shape
    wa = a.shape[1]
    ws = s.shape[1]
    wr_hi = w_r.astype(BF16)
    wr_lo = (w_r - wr_hi.astype(F32)).astype(BF16)
    row = lambda i: (i, 0)
    fixed = lambda i: (0, 0)
    return pl.pallas_call(
        _postmix_kernel,
        grid=(n // tm,),
        in_specs=[
            pl.BlockSpec((tm, wa), row),
            pl.BlockSpec((tm, ws), row),
            pl.BlockSpec((tm, d), row),
            pl.BlockSpec((ws, ws), fixed),
            pl.BlockSpec((1, ws), fixed),
            pl.BlockSpec((1, ws), fixed),
            pl.BlockSpec((wa, d), fixed),
            pl.BlockSpec((ws, d), fixed),
            pl.BlockSpec((1, d), fixed),
            pl.BlockSpec((d, ROUTER_LANES), fixed),
            pl.BlockSpec((d, ROUTER_LANES), fixed),
            pl.BlockSpec((1, ROUTER_LANES), fixed),
        ],
        out_specs=[pl.BlockSpec((tm, d), row), pl.BlockSpec((tm * (d // SLAB_COLS), LANES), row),
                   pl.BlockSpec((tm, ROUTER_LANES), row)],
        out_shape=[jax.ShapeDtypeStruct((n, d), F32), jax.ShapeDtypeStruct((n * (d // SLAB_COLS), LANES), jnp.uint32),
                   jax.ShapeDtypeStruct((n, ROUTER_LANES), F32)],
        compiler_params=_params(("parallel",)),
        name="postmix",
    )(a, s, x2d, w_glu.astype(BF16), b_glu.reshape(1, ws).astype(F32), g_s.reshape(1, ws).astype(F32),
      w_o[:wa].astype(BF16), w_o[wa:].astype(BF16), g_ffn.reshape(1, d).astype(F32),
      wr_hi, wr_lo, b_r.reshape(1, ROUTER_LANES).astype(F32))


def _route(logits, rows, slab_rows):
    n_tok = logits.shape[0]
    lg = logits[:, :N_EXPERT_GROUPS]
    le = logits[:, N_EXPERT_GROUPS:N_EXPERT_GROUPS + N_EXPERTS].reshape(
        n_tok, N_EXPERT_GROUPS, EXPERTS_PER_GROUP)
    pg = jax.nn.softmax(lg, axis=-1)
    gsel = jnp.argmax(lg, axis=-1).astype(jnp.int32)
    gate_g = jnp.max(pg, axis=-1, keepdims=True)
    sel = gsel[:, None] == jnp.arange(N_EXPERT_GROUPS, dtype=jnp.int32)[None, :]
    le_sel = jnp.sum(jnp.where(sel[:, :, None], le, 0.0), axis=1)
    pe = jax.nn.softmax(le_sel, axis=-1)
    top_p, top_i = lax.top_k(pe, TOP_K)
    w = gate_g * top_p / jnp.sum(top_p, axis=-1, keepdims=True)
    eid = gsel[:, None] * EXPERTS_PER_GROUP + top_i.astype(jnp.int32)

    n_assign = n_tok * TOP_K
    flat_e = eid.reshape(-1)
    counts = jnp.sum((flat_e[None, :] == jnp.arange(N_EXPERTS, dtype=jnp.int32)[:, None])
                     .astype(jnp.int32), axis=1)
    n_blk_e = (counts + rows - 1) // rows
    blk_end = jnp.cumsum(n_blk_e)
    blk_first = blk_end - n_blk_e
    start = jnp.cumsum(counts) - counts
    order = jnp.argsort(flat_e, stable=True).astype(jnp.int32)
    n_blk = (n_assign + rows - 1) // rows + N_EXPERTS
    blk = jnp.arange(n_blk, dtype=jnp.int32)
    blk_e = jnp.minimum(jnp.sum((blk[:, None] >= blk_end[None, :]).astype(jnp.int32), axis=1),
                        N_EXPERTS - 1)
    used = blk < blk_end[-1]
    experts = jnp.arange(N_EXPERTS, dtype=jnp.int32)
    is_e = blk_e[:, None] == experts[None, :]

    def per_block(table):
        return jnp.sum(jnp.where(is_e, table[None, :], 0), axis=1)

    in_e = (blk - per_block(blk_first)) * rows
    blk_cnt = jnp.where(used, jnp.clip(per_block(counts) - in_e, 0, rows), 0).astype(jnp.int32)
    blk_src = jnp.where(used, per_block(start) + in_e, 0).astype(jnp.int32)
    last_e = jnp.max(jnp.where(counts > 0, experts, 0))
    blk_e = jnp.where(used, blk_e, last_e)
    row_tok = order // TOP_K
    row_dst = (order % TOP_K) * n_tok + row_tok
    pad = jnp.zeros((rows,), jnp.int32)
    return (w, jnp.concatenate([row_tok * slab_rows, pad]), jnp.concatenate([row_dst * slab_rows, pad]),
            blk_e, blk_cnt, blk_src)


def _expert_kernel(blk_e_ref, blk_cnt_ref, blk_src_ref, tok_ref, dst_ref,
                   h_hbm, w1_ref, w3_ref, w2_ref, y_hbm, xbuf, ybuf, gsem, ssem):
    del blk_e_ref
    b = pl.program_id(0)
    nb = pl.num_programs(0)
    slot = b % N_SLOTS
    slot_m1 = (b + N_SLOTS - 1) % N_SLOTS
    slot_p1 = (b + 1) % N_SLOTS
    spr = w1_ref.shape[1] // SLAB_COLS
    rows = xbuf.shape[1] // spr
    dump = y_hbm.shape[0] - N_SLOTS * rows * spr
    cnt = blk_cnt_ref[b]
    prev = jnp.maximum(b - 1, 0)
    prev_cnt = jnp.where(b >= 1, blk_cnt_ref[prev], 0)

    def gather_copy(src, r, sl):
        return pltpu.make_async_copy(h_hbm.at[pl.ds(pl.multiple_of(tok_ref[src + r], spr), spr)],
                                     xbuf.at[sl, pl.ds(r * spr, spr)], gsem.at[sl])

    def scatter_copy(src, n_valid, r, sl):
        dst = jnp.where(r < n_valid, dst_ref[src + r], dump + (sl * rows + r) * spr)
        return pltpu.make_async_copy(ybuf.at[sl, pl.ds(r * spr, spr)],
                                     y_hbm.at[pl.ds(pl.multiple_of(dst, spr), spr)], ssem.at[sl])

    def start_gathers(blk, sl, lo=0, hi=None):
        src = blk_src_ref[blk]
        for r in range(lo, rows if hi is None else hi):
            gather_copy(src, r, sl).start()

    def wait_gathers(sl):
        for r in range(rows):
            gather_copy(0, r, sl).wait()

    def start_scatters(blk, n_valid, sl, lo=0, hi=None):
        src = blk_src_ref[blk]
        for r in range(lo, rows if hi is None else hi):
            scatter_copy(src, n_valid, r, sl).start()

    def wait_scatters(sl):
        for r in range(rows):
            scatter_copy(0, 0, r, sl).wait()

    @pl.when(b == 0)
    def _():
        ybuf[...] = jnp.zeros(ybuf.shape, ybuf.dtype)
        start_gathers(0, 0)
        start_gathers(1, 1)
        for sl in range(N_SLOTS - 1):
            start_scatters(0, 0, sl)
        for sl in range(N_SLOTS - 1):
            wait_scatters(sl)

    @pl.when(cnt > 0)
    def _():
        @pl.when(b >= 2)
        def _():
            wait_scatters(slot)

        wait_gathers(slot)
        x = _load_row_slabs(xbuf.at[slot], 0, rows, spr).astype(BF16)
        f = w1_ref.shape[2]
        n_f = f // MXU_WIDTH
        n_part = 3 * n_f
        bounds = [rows * i // n_part for i in range(n_part + 1)]
        part = iter(range(n_part))

        def issue_part():
            i = next(part)
            start_gathers(jnp.minimum(b + 2, nb - 1), slot_m1, bounds[i], bounds[i + 1])
            start_scatters(prev, prev_cnt, slot_m1, bounds[i], bounds[i + 1])

        acts = []
        for c in range(n_f):
            cols = slice(c * MXU_WIDTH, (c + 1) * MXU_WIDTH)
            issue_part()
            h1 = jnp.dot(x, w1_ref[0, :, cols].astype(BF16), preferred_element_type=F32)
            issue_part()
            h3 = jnp.dot(x, w3_ref[0, :, cols].astype(BF16), preferred_element_type=F32)
            acts.append((h1 * _sigmoid(h1) * h3).astype(BF16))
        y = None
        for c in range(n_f):
            issue_part()
            yc = jnp.dot(acts[c], w2_ref[0, c * MXU_WIDTH:(c + 1) * MXU_WIDTH, :].astype(BF16),
                         preferred_element_type=F32)
            y = yc if y is None else y + yc
        _store_row_slabs(ybuf.at[slot], 0, y)

    @pl.when((cnt == 0) & (prev_cnt > 0))
    def _():
        wait_gathers(slot)
        wait_gathers(slot_p1)

        @pl.when(b >= 2)
        def _():
            wait_scatters(slot)

        wait_scatters(slot_p1)
        start_scatters(prev, prev_cnt, slot_m1)
        wait_scatters(slot_m1)


def _experts(h, w1, w3, w2, row_tok, row_dst, blk_e, blk_cnt, blk_src, rows):
    d, f = w1.shape[1], w1.shape[2]
    spr = d // SLAB_COLS
    n_blk = blk_e.shape[0]
    wmap = lambda b, be, bc, bs, rt, rd: (be[b], 0, 0)
    grid_spec = pltpu.PrefetchScalarGridSpec(
        num_scalar_prefetch=5,
        grid=(n_blk,),
        in_specs=[
            pl.BlockSpec(memory_space=pl.ANY),
            pl.BlockSpec((1, d, f), wmap),
            pl.BlockSpec((1, d, f), wmap),
            pl.BlockSpec((1, f, d), wmap),
        ],
        out_specs=pl.BlockSpec(memory_space=pl.ANY),
        scratch_shapes=[
            pltpu.VMEM((N_SLOTS, rows * spr, LANES), jnp.uint32),
            pltpu.VMEM((N_SLOTS, rows * spr, LANES), jnp.uint32),
            pltpu.SemaphoreType.DMA((N_SLOTS,)),
            pltpu.SemaphoreType.DMA((N_SLOTS,)),
        ],
    )
    return pl.pallas_call(
        _expert_kernel,
        grid_spec=grid_spec,
        out_shape=jax.ShapeDtypeStruct((h.shape[0] * TOP_K + N_SLOTS * rows * spr, LANES), jnp.uint32),
        compiler_params=_params(("arbitrary",), disable_bounds_checks=True),
        name="experts",
    )(blk_e, blk_cnt, blk_src, row_tok, row_dst, h, w1, w3, w2)


def _ple_kernel(x1_ref, y0_ref, y1_ref, w_ref, p_ref, gp_ref, wg_ref, wp_ref, gf_ref, o_ref, *, final):
    w = w_ref[...]
    tm, d = x1_ref.shape
    spr = d // SLAB_COLS
    y0 = _load_row_slabs(y0_ref, 0, tm, spr)
    y1 = _load_row_slabs(y1_ref, 0, tm, spr)
    x2 = x1_ref[...] + w[:, 0:1] * y0 + w[:, 1:2] * y1
    hn = _rms(x2, gp_ref[...], 1e-6).astype(BF16)
    gate = _sigmoid(jnp.dot(hn, wg_ref[...], preferred_element_type=F32))
    pp = jnp.dot(p_ref[...].astype(BF16), wp_ref[...], preferred_element_type=F32)
    x3 = x2 + gate * pp
    o_ref[...] = _rms(x3, gf_ref[...], 1e-6) if final else x3


def _ple(x1, y2, w, p2d, g_ple, w_gate, w_proj, g_final, final, tm=256):
    n, d = x1.shape
    pd = p2d.shape[1]
    row = lambda i: (i, 0)
    fixed = lambda i: (0, 0)
    return pl.pallas_call(
        functools.partial(_ple_kernel, final=final),
        grid=(n // tm,),
        in_specs=[
            pl.BlockSpec((tm, d), row),
            pl.BlockSpec((tm * (d // SLAB_COLS), LANES), row),
            pl.BlockSpec((tm * (d // SLAB_COLS), LANES), lambda i: (n // tm + i, 0)),
            pl.BlockSpec((tm, TOP_K), row),
            pl.BlockSpec((tm, pd), row),
            pl.BlockSpec((1, d), fixed),
            pl.BlockSpec((d, d), fixed),
            pl.BlockSpec((pd, d), fixed),
            pl.BlockSpec((1, d), fixed),
        ],
        out_specs=pl.BlockSpec((tm, d), row),
        out_shape=jax.ShapeDtypeStruct((n, d), F32),
        compiler_params=_params(("parallel",)),
        name="ple_final",
    )(x1, y2, y2, w, p2d, g_ple.reshape(1, d).astype(F32), w_gate.astype(BF16), w_proj.astype(BF16),
      g_final.reshape(1, d).astype(F32))


def kernel(x, p, rel_bias, g_mix, w_in, lam_q1, lam_k1, lam_q2, lam_k2, subln_g, ssm_lam_re, ssm_lam_im, ssm_log_dt, ssm_b_re, ssm_b_im, ssm_c_re, ssm_c_im, ssm_d, w_glu, b_glu, ssm_norm_g, w_o, g_ffn, w_router_g, b_router_g, w_router_e, b_router_e, w1, w3, w2, g_ple, w_ple_gate, w_ple_proj, g_final):
    bt, s_len, d = x.shape
    n = bt * s_len
    depth = g_mix.shape[0]
    attn_w = N_HEADS * DV
    xc = x.reshape(n, d).astype(F32)
    for i in range(depth):
        lam_init = 0.8 - 0.6 * math.exp(-0.3 * i)
        col_scale = jnp.concatenate([jnp.full((attn_w,), LOG2E * DK ** -0.5, F32),
                                     jnp.ones((w_in.shape[2] - attn_w,), F32)])
        w_in_b = (w_in[i].astype(F32) * col_scale[None, :]).astype(BF16)
        z, u = _inproj(xc, g_mix[i].astype(F32), w_in_b)
        z3 = z.reshape(bt, s_len, z.shape[1])

        lam = (jnp.exp(jnp.sum(lam_q1[i].astype(F32) * lam_k1[i].astype(F32)))
               - jnp.exp(jnp.sum(lam_q2[i].astype(F32) * lam_k2[i].astype(F32))) + lam_init)
        a = _attention(z3, rel_bias, lam, subln_g[i], lam_init)

        ops = _ssm_operators(ssm_lam_re[i], ssm_lam_im[i], ssm_log_dt[i], ssm_b_re[i], ssm_b_im[i],
                             ssm_c_re[i], ssm_c_im[i], ssm_d[i])
        s = _ssm(u, ops, bt)

        w_r = jnp.zeros((d, ROUTER_LANES), F32)
        w_r = w_r.at[:, :N_EXPERT_GROUPS].set(w_router_g[i].astype(F32))
        w_r = w_r.at[:, N_EXPERT_GROUPS:N_EXPERT_GROUPS + N_EXPERTS].set(w_router_e[i].astype(F32))
        b_r = jnp.zeros((ROUTER_LANES,), F32)
        b_r = b_r.at[:N_EXPERT_GROUPS].set(b_router_g[i].astype(F32))
        b_r = b_r.at[N_EXPERT_GROUPS:N_EXPERT_GROUPS + N_EXPERTS].set(b_router_e[i].astype(F32))
        x1, h, logits = _postmix(a.reshape(n, attn_w), s, xc, w_glu[i], b_glu[i],
                                 ssm_norm_g[i], w_o[i], g_ffn[i], w_r, b_r)

        gate_w, row_tok, row_dst, blk_e, blk_cnt, blk_src = _route(logits, EXPERT_ROWS, d // SLAB_COLS)
        y2 = _experts(h, w1[i], w3[i], w2[i],
                      row_tok, row_dst, blk_e, blk_cnt, blk_src, EXPERT_ROWS)
        xc = _ple(x1, y2, gate_w.astype(F32), p[i].reshape(n, -1).astype(F32),
                  g_ple[i], w_ple_gate[i], w_ple_proj[i], g_final, final=(i == depth - 1))
    return xc.reshape(bt, s_len, d)
```

```python
import functools
import math

import jax
import jax.numpy as jnp
from jax import lax
from jax.experimental import pallas as pl
from jax.experimental.pallas import tpu as pltpu

F32 = jnp.float32
BF16 = jnp.bfloat16
HIGHEST = lax.Precision.HIGHEST

N_HEADS = 8
DK = 64
DV = 128
N_BUCKETS = 32
MAX_DISTANCE = 128
SSM_GROUP = 16
SSM_GROUPS = 64
SSM_STATE = 64
N_EXPERT_GROUPS = 4
EXPERTS_PER_GROUP = 8
N_EXPERTS = 32
TOP_K = 2
MASK_VALUE = -1e30
LOG2E = math.log2(math.e)

SSM_CHUNK = 16
SSM_BUNDLE = 8
ATTN_TQ = 1024
ONES_ROWS = 16
ATTN_CB = 256
ATTN_TK = 256
EXPERT_ROWS = 256
N_SLOTS = 3
MXU_WIDTH = 256
LANES = 128
SLAB_COLS = 2 * LANES
ROUTER_LANES = 128
VMEM_LIMIT = 56 << 20


def _params(semantics, **kw):
    return pltpu.CompilerParams(dimension_semantics=semantics, vmem_limit_bytes=VMEM_LIMIT, **kw)


def _rms(x, g, eps):
    return x * lax.rsqrt(jnp.mean(x * x, axis=-1, keepdims=True) + eps) * g


def _sigmoid(x):
    return 1.0 / (1.0 + jnp.exp(-x))


def _store_row_slabs(ref, row0, x):
    n, d = x.shape
    spr = d // (2 * LANES)
    bits = lax.bitcast_convert_type(x.astype(BF16).astype(F32), jnp.uint32)
    for j in range(spr):
        lo = lax.shift_right_logical(bits[:, j * LANES:(j + 1) * LANES], jnp.uint32(16))
        hi = bits[:, d // 2 + j * LANES:d // 2 + (j + 1) * LANES] & jnp.uint32(0xFFFF0000)
        ref[pl.ds(row0 * spr + j, n, stride=spr), :] = lo | hi


def _load_row_slabs(ref, row0, n, spr):
    words = [ref[pl.ds(row0 * spr + j, n, stride=spr), :] for j in range(spr)]
    lo = [lax.bitcast_convert_type(lax.shift_left(w, jnp.uint32(16)), F32) for w in words]
    hi = [lax.bitcast_convert_type(w & jnp.uint32(0xFFFF0000), F32) for w in words]
    return jnp.concatenate(lo + hi, axis=1)


def _inproj_kernel(x_ref, g_ref, w_ref, z_ref, u_ref, h_scr):
    j = pl.program_id(1)
    last = pl.num_programs(1) - 1

    @pl.when(j == 0)
    def _():
        h_scr[...] = _rms(x_ref[...], g_ref[...], 1e-6).astype(BF16)

    acc = jnp.dot(h_scr[...], w_ref[...], preferred_element_type=F32)

    @pl.when(j < last)
    def _():
        z_ref[...] = acc.astype(z_ref.dtype)

    @pl.when(j == last)
    def _():
        u_ref[...] = acc


def _inproj(x2d, g, w, tm=512, tn=1024):
    n, d = x2d.shape
    pw = w.shape[1]
    n_col = pw // tn
    return pl.pallas_call(
        _inproj_kernel,
        grid=(n // tm, n_col),
        in_specs=[
            pl.BlockSpec((tm, d), lambda i, j: (i, 0)),
            pl.BlockSpec((1, d), lambda i, j: (0, 0)),
            pl.BlockSpec((d, tn), lambda i, j: (0, j)),
        ],
        out_specs=[pl.BlockSpec((tm, tn), lambda i, j: (i, jnp.minimum(j, n_col - 2))),
                   pl.BlockSpec((tm, tn), lambda i, j: (i, 0))],
        out_shape=[jax.ShapeDtypeStruct((n, pw - tn), BF16), jax.ShapeDtypeStruct((n, tn), F32)],
        scratch_shapes=[pltpu.VMEM((tm, d), BF16)],
        compiler_params=_params(("parallel", "arbitrary")),
        name="inproj",
    )(x2d, g.reshape(1, d), w)


def _attn_kernel(lam_ref, q_ref, k_ref, v_ref, bias_ref, g_ref, o_ref, qs, vt, *state,
                 tq, tk, cb, out_scale):
    n_blk = 2 * tq // cb
    m_scr, acc, s_scr = (state[i * n_blk:(i + 1) * n_blk] for i in range(3))
    qi = pl.program_id(2)
    n_sub = tq // tk
    n_kv = v_ref.shape[1] // tk

    @pl.when(qi == 0)
    def _():
        for c in range(n_kv):
            vt[c, 0:DV, :] = v_ref[0, c * tk:(c + 1) * tk, :].astype(F32).T.astype(BF16)
            vt[c, DV:DV + ONES_ROWS, :] = jnp.ones((ONES_ROWS, tk), BF16)

    q = q_ref[0]
    lane = lax.broadcasted_iota(jnp.int32, q.shape, 1)
    zero = jnp.zeros_like(q)
    qs[0:tq, :] = jnp.where(lane < DK, q, zero)
    qs[tq:2 * tq, :] = jnp.where(lane >= DK, q, zero)
    for c in range(n_blk):
        m_scr[c][...] = jnp.full(m_scr[c].shape, MASK_VALUE, F32)
        acc[c][...] = jnp.zeros(acc[c].shape, F32)

    all_blocks = tuple(range(n_blk))

    def visible(r):
        return tuple(c for c in all_blocks if (c * cb) % tq + cb > (r - 1) * tk)

    def scores(j, blocks):
        kj = k_ref[0, pl.ds(pl.multiple_of(j * tk, tk), tk), :]
        return [lax.dot_general(kj, qs[c * cb:(c + 1) * cb, :], (((1,), (1,)), ((), ())),
                                preferred_element_type=F32) for c in blocks]

    def consume(j, near, blocks, next_blocks):
        nxt = scores(j + 1, next_blocks) if next_blocks else None
        vtj = vt[j]
        alphas, ps = [], []
        for c in blocks:
            s = s_scr[c][...]
            if near is not None:
                ahead = (c * cb) % tq - (near - 1) * tk
                if ahead in (0, tk):
                    s = s + bias_ref[0, ahead // tk]
            m_old = m_scr[c][...]
            m_new = jnp.maximum(m_old, jnp.max(s, axis=0, keepdims=True))
            alpha = jnp.exp2(m_old - m_new)
            p = jnp.exp2(s - m_new)
            m_scr[c][...] = m_new
            alphas.append(alpha)
            ps.append(p.astype(BF16))
        pvs = [jnp.dot(vtj, p, preferred_element_type=F32) for p in ps]
        for c, alpha, pv in zip(blocks, alphas, pvs):
            acc[c][...] = alpha * acc[c][...] + pv
        if next_blocks:
            for c, sc in zip(next_blocks, nxt):
                s_scr[c][...] = sc

    first = qi * n_sub - 1
    for c, sc in zip(all_blocks, scores(0, all_blocks)):
        s_scr[c][...] = sc

    def far_pair(i, carry):
        consume(2 * i, None, all_blocks, all_blocks)
        consume(2 * i + 1, None, all_blocks, all_blocks)
        return carry

    lax.fori_loop(0, jnp.maximum(first, 0) // 2, far_pair, 0)

    @pl.when(qi >= 1)
    def _():
        consume(first - 1, None, all_blocks, all_blocks)
        consume(first, 0, all_blocks, all_blocks)

    for r in range(1, n_sub + 1):
        consume(first + r, r, visible(r), visible(r + 1) if r < n_sub else None)

    lam = lam_ref[0, 0]
    half = n_blk // 2
    for b in range(half):
        a1, a2 = acc[b], acc[half + b]
        ot = (a1[0:DV, :] / a1[DV:DV + 1, :] - lam * (a2[0:DV, :] / a2[DV:DV + 1, :]))
        ot = ot * lax.rsqrt(jnp.mean(ot * ot, axis=0, keepdims=True) + 1e-5)
        o_ref[0, b * cb:(b + 1) * cb, :] = (ot.T * (g_ref[...] * out_scale)).astype(o_ref.dtype)


def _t5_bucket(n):
    n = jnp.maximum(n, 0)
    max_exact = N_BUCKETS // 2
    nf = jnp.maximum(n, 1).astype(F32)
    large = max_exact + (jnp.log(nf / max_exact) / math.log(MAX_DISTANCE / max_exact)
                         * (N_BUCKETS - max_exact)).astype(jnp.int32)
    large = jnp.minimum(large, N_BUCKETS - 1)
    return jnp.where(n < max_exact, n, large)


def _attn_bias_tiles(rel_bias, tk):
    assert tk >= MAX_DISTANCE
    table = rel_bias.astype(F32)
    rel_table = (table - table[N_BUCKETS - 1][None, :]) * LOG2E
    r = jnp.arange(tk, dtype=jnp.int32)[:, None]
    c = jnp.arange(tk, dtype=jnp.int32)[None, :]
    tiles = []
    for ahead in (0, tk):
        dist = c + ahead - r
        onehot = (_t5_bucket(dist)[:, :, None] == jnp.arange(N_BUCKETS, dtype=jnp.int32)).astype(F32)
        b = jnp.einsum('rcn,nh->hrc', onehot, rel_table, precision=HIGHEST)
        tiles.append(jnp.where((dist >= 0)[None], b, MASK_VALUE))
    return jnp.stack(tiles, axis=1)


def _attention(z3, rel_bias, lam, subln_g, lam_init, tq=ATTN_TQ, tk=ATTN_TK, cb=ATTN_CB):
    bt, s_len, _ = z3.shape
    assert cb == tk and tq % (2 * tk) == 0
    bias = _attn_bias_tiles(rel_bias, tk)
    n_sp = bias.shape[1]
    kern = functools.partial(_attn_kernel, tq=tq, tk=tk, cb=cb, out_scale=1.0 - lam_init)
    n_blk = 2 * tq // cb
    return pl.pallas_call(
        kern,
        grid=(bt, N_HEADS, s_len // tq),
        in_specs=[
            pl.BlockSpec(memory_space=pltpu.SMEM),
            pl.BlockSpec((1, tq, 2 * DK), lambda b, h, i: (b, i, h)),
            pl.BlockSpec((1, s_len, 2 * DK), lambda b, h, i: (b, 0, N_HEADS + h)),
            pl.BlockSpec((1, s_len, DV), lambda b, h, i: (b, 0, 2 * N_HEADS + h)),
            pl.BlockSpec((1, n_sp, tk, cb), lambda b, h, i: (h, 0, 0, 0)),
            pl.BlockSpec((1, DV), lambda b, h, i: (0, 0)),
        ],
        out_specs=pl.BlockSpec((1, tq, DV), lambda b, h, i: (b, i, h)),
        out_shape=jax.ShapeDtypeStruct((bt, s_len, N_HEADS * DV), BF16),
        scratch_shapes=[
            pltpu.VMEM((2 * tq, 2 * DK), BF16),
            pltpu.VMEM((s_len // tk, DV + ONES_ROWS, tk), BF16),
        ] + [pltpu.VMEM((1, cb), F32)] * n_blk + [pltpu.VMEM((DV + ONES_ROWS, cb), F32)] * n_blk
        + [pltpu.VMEM((tk, cb), F32)] * n_blk,
        compiler_params=_params(("parallel", "parallel", "arbitrary")),
        name="diff_attention",
    )(lam.reshape(1, 1).astype(F32), z3, z3, z3, bias, subln_g.reshape(1, DV).astype(F32))


def _ssm_operators(lam_re, lam_im, log_dt, b_re, b_im, c_re, c_im, d_skip):
    L, H, P = SSM_CHUNK, SSM_GROUP, SSM_STATE
    lre = lam_re.astype(F32)
    lim = lam_im.astype(F32)
    dt = jnp.exp(log_dt.astype(F32))[:, None]
    mag = jnp.exp(lre * dt)
    ab_re = mag * jnp.cos(lim * dt)
    ab_im = mag * jnp.sin(lim * dt)
    den = lre * lre + lim * lim
    nr, ni = ab_re - 1.0, ab_im
    cr = ((nr * lre + ni * lim) / den)[..., None]
    ci = ((ni * lre - nr * lim) / den)[..., None]
    bre = b_re.astype(F32)
    bim = b_im.astype(F32)
    bb_re = cr * bre - ci * bim
    bb_im = cr * bim + ci * bre
    cre = c_re.astype(F32)
    cim = c_im.astype(F32)

    tau = jnp.arange(L + 1, dtype=F32)[:, None, None]
    pw_mag = jnp.exp(tau * (lre * dt)[None])
    pw_re = pw_mag * jnp.cos(tau * (lim * dt)[None])
    pw_im = pw_mag * jnp.sin(tau * (lim * dt)[None])

    ca_re = cre[None] * pw_re[:, :, None, :] - cim[None] * pw_im[:, :, None, :]
    ca_im = cre[None] * pw_im[:, :, None, :] + cim[None] * pw_re[:, :, None, :]
    bbt_re = bb_re.transpose(0, 2, 1)[None, :, None]
    bbt_im = bb_im.transpose(0, 2, 1)[None, :, None]
    k_tau = jnp.sum(ca_re[:L, :, :, None, :] * bbt_re - ca_im[:L, :, :, None, :] * bbt_im, axis=-1)
    ti = jnp.arange(L)
    t_src = jnp.tile(k_tau.transpose(0, 1, 3, 2).reshape(L, SSM_GROUPS * H, H), (1, 1, SSM_BUNDLE))

    rev_re = pw_re[L - 1 - ti]
    rev_im = pw_im[L - 1 - ti]
    w_re = rev_re[..., None] * bb_re[None] - rev_im[..., None] * bb_im[None]
    w_im = rev_re[..., None] * bb_im[None] + rev_im[..., None] * bb_re[None]
    w_src = jnp.concatenate([w_re, w_im], axis=2).transpose(0, 1, 3, 2)

    e_src = jnp.concatenate([ca_re[1:L + 1], -ca_im[1:L + 1]], axis=3)

    a1 = jnp.concatenate([pw_re[L], pw_re[L]], axis=-1)
    a2 = jnp.concatenate([-pw_im[L], pw_im[L]], axis=-1)
    d_tile = jnp.tile(d_skip.astype(F32).reshape(SSM_GROUPS // SSM_BUNDLE, 1, SSM_BUNDLE * H), (1, 1, L))
    return t_src.astype(BF16), w_src.astype(BF16), e_src.astype(BF16), a1, a2, d_tile


def _bundle_operator(dst, src_ref):
    n_l, n_g, h, x = src_ref.shape
    dst[...] = jnp.zeros(dst.shape, dst.dtype)
    for i in range(n_l):
        for g in range(n_g):
            r0 = (i * n_g + g) * h
            dst[r0:r0 + h, g * x:(g + 1) * x] = src_ref[i, g]


def _chunk_rows(u_ref, n_chunk):
    return jnp.concatenate([u_ref[pl.ds(t, n_chunk, stride=SSM_CHUNK), :] for t in range(SSM_CHUNK)], axis=1)


def _ssm_in_kernel(u_ref, w_ref, v_ref, w_scr):
    @pl.when(pl.program_id(1) == 0)
    def _():
        _bundle_operator(w_scr, w_ref)

    x = _chunk_rows(u_ref, v_ref.shape[0]).astype(BF16)
    v = jnp.dot(x, w_scr[...], preferred_element_type=F32)
    sw = v_ref.shape[2]
    for g in range(v_ref.shape[1]):
        v_ref[:, g, :] = v[:, g * sw:(g + 1) * sw]


def _ssm_scan_kernel(v_ref, a1_ref, a2_ref, o_ref, st):
    @pl.when(pl.program_id(0) == 0)
    def _():
        st[...] = jnp.zeros(st.shape, F32)

    a1 = a1_ref[...][None]
    a2 = a2_ref[...][None]
    n_chunk = v_ref.shape[1]
    half = v_ref.shape[3] // 2

    def body(c, s):
        o_ref[:, pl.ds(c, 1)] = s[:, None].astype(o_ref.dtype)
        v = v_ref[:, pl.ds(c, 1)][:, 0]
        return a1 * s + a2 * pltpu.roll(s, half, axis=2) + v

    st[...] = lax.fori_loop(0, n_chunk, body, st[...])


def _gelu_tanh(x):
    c = math.sqrt(2.0 / math.pi)
    return x * (0.5 * (1.0 + jnp.tanh(c * (x + 0.044715 * (x * x * x)))))


def _ssm_out_kernel(u_ref, t_ref, s_ref, e_ref, d_ref, y_ref, t_scr, et_scr):
    L, lw, h = SSM_CHUNK, t_ref.shape[1], SSM_GROUP

    @pl.when(pl.program_id(1) == 0)
    def _():
        _bundle_operator(et_scr, e_ref)
        same_group = (lax.broadcasted_iota(jnp.int32, (lw, lw), 0) // h
                      == lax.broadcasted_iota(jnp.int32, (lw, lw), 1) // h)
        t_scr[...] = jnp.zeros(t_scr.shape, t_scr.dtype)
        for tau in range(L):
            blk = jnp.where(same_group, t_ref[tau], jnp.zeros((lw, lw), t_ref.dtype))
            for i in range(L - tau):
                j = i + tau
                t_scr[i * lw:(i + 1) * lw, j * lw:(j + 1) * lw] = blk

    n_chunk = s_ref.shape[0]
    xf = _chunk_rows(u_ref, n_chunk)
    sp = jnp.concatenate([s_ref[:, g, :] for g in range(s_ref.shape[1])], axis=1).astype(BF16)
    xb = xf.astype(BF16)
    step = MXU_WIDTH
    y = jnp.concatenate([jnp.dot(xb[:, :c + step], t_scr[:c + step, c:c + step], preferred_element_type=F32)
                         for c in range(0, L * lw, step)], axis=1)
    y = y + lax.dot_general(sp, et_scr[...], (((1,), (1,)), ((), ())), preferred_element_type=F32)
    y = _gelu_tanh(y + d_ref[0] * xf)
    for t in range(L):
        y_ref[pl.ds(t, n_chunk, stride=L), :] = y[:, t * lw:(t + 1) * lw]


def _ssm(u, ops, bt, scan_block=32, chunk_tile=256):
    t_src, w_src, e_src, a1, a2, d_tile = ops
    n, width = u.shape
    L, G, H, P2, nbg = SSM_CHUNK, SSM_GROUPS, SSM_GROUP, 2 * SSM_STATE, SSM_BUNDLE
    nb = G // nbg
    lw = nbg * H
    nc = n // L
    n_c = nc // bt
    ct = min(chunk_tile, nc)
    u_spec = pl.BlockSpec((ct * L, lw), lambda b, i: (i, b))
    src_spec = pl.BlockSpec((L, nbg, H, P2), lambda b, i: (0, b, 0, 0))
    state_spec = pl.BlockSpec((ct, nbg, P2), lambda b, i: (i, b, 0))

    v = pl.pallas_call(
        _ssm_in_kernel,
        grid=(nb, nc // ct),
        in_specs=[u_spec, src_spec],
        out_specs=state_spec,
        out_shape=jax.ShapeDtypeStruct((nc, G, P2), F32),
        scratch_shapes=[pltpu.VMEM((L * lw, nbg * P2), BF16)],
        compiler_params=_params(("parallel", "arbitrary")),
        name="ssm_chunk_state",
    )(u, w_src)

    cb = min(scan_block, n_c)
    s_prev = pl.pallas_call(
        _ssm_scan_kernel,
        grid=(n_c // cb,),
        in_specs=[pl.BlockSpec((bt, cb, G, P2), lambda c: (0, c, 0, 0)),
                  pl.BlockSpec((G, P2), lambda c: (0, 0)),
                  pl.BlockSpec((G, P2), lambda c: (0, 0))],
        out_specs=pl.BlockSpec((bt, cb, G, P2), lambda c: (0, c, 0, 0)),
        out_shape=jax.ShapeDtypeStruct((bt, n_c, G, P2), F32),
        scratch_shapes=[pltpu.VMEM((bt, G, P2), F32)],
        compiler_params=_params(("arbitrary",)),
        name="ssm_scan",
    )(v.reshape(bt, n_c, G, P2), a1, a2)

    return pl.pallas_call(
        _ssm_out_kernel,
        grid=(nb, nc // ct),
        in_specs=[u_spec,
                  pl.BlockSpec((L, lw, lw), lambda b, i: (0, b, 0)),
                  state_spec,
                  src_spec,
                  pl.BlockSpec((1, 1, L * lw), lambda b, i: (b, 0, 0))],
        out_specs=pl.BlockSpec((ct * L, lw), lambda b, i: (i, b)),
        out_shape=jax.ShapeDtypeStruct((n, width), F32),
        scratch_shapes=[pltpu.VMEM((L * lw, L * lw), BF16), pltpu.VMEM((L * lw, nbg * P2), BF16)],
        compiler_params=_params(("parallel", "arbitrary")),
        name="ssm_output",
    )(u, t_src, s_prev.reshape(nc, G, P2), e_src, d_tile)


def _postmix_kernel(a_ref, s_ref, x_ref, wg_ref, bg_ref, gs_ref, woa_ref, wos_ref, gf_ref,
                    wrh_ref, wrl_ref, br_ref, x1_ref, h_ref, lg_ref):
    sf = s_ref[...]
    gate = _sigmoid(jnp.dot(sf.astype(BF16), wg_ref[...], preferred_element_type=F32) + bg_ref[...])
    sn = _rms(sf * gate, gs_ref[...], 1e-6).astype(BF16)
    x1 = (x_ref[...]
          + jnp.dot(a_ref[...], woa_ref[...], preferred_element_type=F32)
          + jnp.dot(sn, wos_ref[...], preferred_element_type=F32))
    x1_ref[...] = x1
    h = _rms(x1, gf_ref[...], 1e-6)
    h_hi = h.astype(BF16)
    h_lo = (h - h_hi.astype(F32)).astype(BF16)
    _store_row_slabs(h_ref, 0, h_hi)
    lg_ref[...] = (jnp.dot(h_hi, wrh_ref[...], preferred_element_type=F32)
                   + jnp.dot(h_lo, wrh_ref[...], preferred_element_type=F32)
                   + jnp.dot(h_hi, wrl_ref[...], preferred_element_type=F32)
                   + br_ref[...])


def _postmix(a, s, x2d, w_glu, b_glu, g_s, w_o, g_ffn, w_r, b_r, tm=256):
    n, d = x2d.shape
    wa = a.shape[1]
    ws = s.shape[1]
    wr_hi = w_r.astype(BF16)
    wr_lo = (w_r - wr_hi.astype(F32)).astype(BF16)
    row = lambda i: (i, 0)
    fixed = lambda i: (0, 0)
    return pl.pallas_call(
        _postmix_kernel,
        grid=(n // tm,),
        in_specs=[
            pl.BlockSpec((tm, wa), row),
            pl.BlockSpec((tm, ws), row),
            pl.BlockSpec((tm, d), row),
            pl.BlockSpec((ws, ws), fixed),
            pl.BlockSpec((1, ws), fixed),
            pl.BlockSpec((1, ws), fixed),
            pl.BlockSpec((wa, d), fixed),
            pl.BlockSpec((ws, d), fixed),
            pl.BlockSpec((1, d), fixed),
            pl.BlockSpec((d, ROUTER_LANES), fixed),
            pl.BlockSpec((d, ROUTER_LANES), fixed),
            pl.BlockSpec((1, ROUTER_LANES), fixed),
        ],
        out_specs=[pl.BlockSpec((tm, d), row), pl.BlockSpec((tm * (d // SLAB_COLS), LANES), row),
                   pl.BlockSpec((tm, ROUTER_LANES), row)],
        out_shape=[jax.ShapeDtypeStruct((n, d), F32), jax.ShapeDtypeStruct((n * (d // SLAB_COLS), LANES), jnp.uint32),
                   jax.ShapeDtypeStruct((n, ROUTER_LANES), F32)],
        compiler_params=_params(("parallel",)),
        name="postmix",
    )(a, s, x2d, w_glu.astype(BF16), b_glu.reshape(1, ws).astype(F32), g_s.reshape(1, ws).astype(F32),
      w_o[:wa].astype(BF16), w_o[wa:].astype(BF16), g_ffn.reshape(1, d).astype(F32),
      wr_hi, wr_lo, b_r.reshape(1, ROUTER_LANES).astype(F32))


def _route(logits, rows, slab_rows):
    n_tok = logits.shape[0]
    lg = logits[:, :N_EXPERT_GROUPS]
    le = logits[:, N_EXPERT_GROUPS:N_EXPERT_GROUPS + N_EXPERTS].reshape(
        n_tok, N_EXPERT_GROUPS, EXPERTS_PER_GROUP)
    pg = jax.nn.softmax(lg, axis=-1)
    gsel = jnp.argmax(lg, axis=-1).astype(jnp.int32)
    gate_g = jnp.max(pg, axis=-1, keepdims=True)
    sel = gsel[:, None] == jnp.arange(N_EXPERT_GROUPS, dtype=jnp.int32)[None, :]
    le_sel = jnp.sum(jnp.where(sel[:, :, None], le, 0.0), axis=1)
    pe = jax.nn.softmax(le_sel, axis=-1)
    top_p, top_i = lax.top_k(pe, TOP_K)
    w = gate_g * top_p / jnp.sum(top_p, axis=-1, keepdims=True)
    eid = gsel[:, None] * EXPERTS_PER_GROUP + top_i.astype(jnp.int32)

    n_assign = n_tok * TOP_K
    flat_e = eid.reshape(-1)
    counts = jnp.sum((flat_e[None, :] == jnp.arange(N_EXPERTS, dtype=jnp.int32)[:, None])
                     .astype(jnp.int32), axis=1)
    n_blk_e = (counts + rows - 1) // rows
    blk_end = jnp.cumsum(n_blk_e)
    blk_first = blk_end - n_blk_e
    start = jnp.cumsum(counts) - counts
    order = jnp.argsort(flat_e, stable=True).astype(jnp.int32)
    n_blk = (n_assign + rows - 1) // rows + N_EXPERTS
    blk = jnp.arange(n_blk, dtype=jnp.int32)
    blk_e = jnp.minimum(jnp.sum((blk[:, None] >= blk_end[None, :]).astype(jnp.int32), axis=1),
                        N_EXPERTS - 1)
    used = blk < blk_end[-1]
    experts = jnp.arange(N_EXPERTS, dtype=jnp.int32)
    is_e = blk_e[:, None] == experts[None, :]

    def per_block(table):
        return jnp.sum(jnp.where(is_e, table[None, :], 0), axis=1)

    in_e = (blk - per_block(blk_first)) * rows
    blk_cnt = jnp.where(used, jnp.clip(per_block(counts) - in_e, 0, rows), 0).astype(jnp.int32)
    blk_src = jnp.where(used, per_block(start) + in_e, 0).astype(jnp.int32)
    last_e = jnp.max(jnp.where(counts > 0, experts, 0))
    blk_e = jnp.where(used, blk_e, last_e)
    row_tok = order // TOP_K
    row_dst = (order % TOP_K) * n_tok + row_tok
    pad = jnp.zeros((rows,), jnp.int32)
    return (w, jnp.concatenate([row_tok * slab_rows, pad]), jnp.concatenate([row_dst * slab_rows, pad]),
            blk_e, blk_cnt, blk_src)


def _expert_kernel(blk_e_ref, blk_cnt_ref, blk_src_ref, tok_ref, dst_ref,
                   h_hbm, w1_ref, w3_ref, w2_ref, y_hbm, xbuf, ybuf, gsem, ssem):
    del blk_e_ref
    b = pl.program_id(0)
    nb = pl.num_programs(0)
    slot = b % N_SLOTS
    slot_m1 = (b + N_SLOTS - 1) % N_SLOTS
    slot_p1 = (b + 1) % N_SLOTS
    spr = w1_ref.shape[1] // SLAB_COLS
    rows = xbuf.shape[1] // spr
    dump = y_hbm.shape[0] - N_SLOTS * rows * spr
    cnt = blk_cnt_ref[b]
    prev = jnp.maximum(b - 1, 0)
    prev_cnt = jnp.where(b >= 1, blk_cnt_ref[prev], 0)

    def gather_copy(src, r, sl):
        return pltpu.make_async_copy(h_hbm.at[pl.ds(pl.multiple_of(tok_ref[src + r], spr), spr)],
                                     xbuf.at[sl, pl.ds(r * spr, spr)], gsem.at[sl])

    def scatter_copy(src, n_valid, r, sl):
        dst = jnp.where(r < n_valid, dst_ref[src + r], dump + (sl * rows + r) * spr)
        return pltpu.make_async_copy(ybuf.at[sl, pl.ds(r * spr, spr)],
                                     y_hbm.at[pl.ds(pl.multiple_of(dst, spr), spr)], ssem.at[sl])

    def start_gathers(blk, sl, lo=0, hi=None):
        src = blk_src_ref[blk]
        for r in range(lo, rows if hi is None else hi):
            gather_copy(src, r, sl).start()

    def wait_gathers(sl):
        for r in range(rows):
            gather_copy(0, r, sl).wait()

    def start_scatters(blk, n_valid, sl, lo=0, hi=None):
        src = blk_src_ref[blk]
        for r in range(lo, rows if hi is None else hi):
            scatter_copy(src, n_valid, r, sl).start()

    def wait_scatters(sl):
        for r in range(rows):
            scatter_copy(0, 0, r, sl).wait()

    @pl.when(b == 0)
    def _():
        ybuf[...] = jnp.zeros(ybuf.shape, ybuf.dtype)
        start_gathers(0, 0)
        start_gathers(1, 1)
        for sl in range(N_SLOTS - 1):
            start_scatters(0, 0, sl)
        for sl in range(N_SLOTS - 1):
            wait_scatters(sl)

    @pl.when(cnt > 0)
    def _():
        @pl.when(b >= 2)
        def _():
            wait_scatters(slot)

        wait_gathers(slot)
        x = _load_row_slabs(xbuf.at[slot], 0, rows, spr).astype(BF16)
        f = w1_ref.shape[2]
        n_f = f // MXU_WIDTH
        n_part = 3 * n_f
        bounds = [rows * i // n_part for i in range(n_part + 1)]
        part = iter(range(n_part))

        def issue_part():
            i = next(part)
            start_gathers(jnp.minimum(b + 2, nb - 1), slot_m1, bounds[i], bounds[i + 1])
            start_scatters(prev, prev_cnt, slot_m1, bounds[i], bounds[i + 1])

        acts = []
        for c in range(n_f):
            cols = slice(c * MXU_WIDTH, (c + 1) * MXU_WIDTH)
            issue_part()
            h1 = jnp.dot(x, w1_ref[0, :, cols].astype(BF16), preferred_element_type=F32)
            issue_part()
            h3 = jnp.dot(x, w3_ref[0, :, cols].astype(BF16), preferred_element_type=F32)
            acts.append((h1 * _sigmoid(h1) * h3).astype(BF16))
        y = None
        for c in range(n_f):
            issue_part()
            yc = jnp.dot(acts[c], w2_ref[0, c * MXU_WIDTH:(c + 1) * MXU_WIDTH, :].astype(BF16),
                         preferred_element_type=F32)
            y = yc if y is None else y + yc
        _store_row_slabs(ybuf.at[slot], 0, y)

    @pl.when((cnt == 0) & (prev_cnt > 0))
    def _():
        wait_gathers(slot)
        wait_gathers(slot_p1)

        @pl.when(b >= 2)
        def _():
            wait_scatters(slot)

        wait_scatters(slot_p1)
        start_scatters(prev, prev_cnt, slot_m1)
        wait_scatters(slot_m1)


def _experts(h, w1, w3, w2, row_tok, row_dst, blk_e, blk_cnt, blk_src, rows):
    d, f = w1.shape[1], w1.shape[2]
    spr = d // SLAB_COLS
    n_blk = blk_e.shape[0]
    wmap = lambda b, be, bc, bs, rt, rd: (be[b], 0, 0)
    grid_spec = pltpu.PrefetchScalarGridSpec(
        num_scalar_prefetch=5,
        grid=(n_blk,),
        in_specs=[
            pl.BlockSpec(memory_space=pl.ANY),
            pl.BlockSpec((1, d, f), wmap),
            pl.BlockSpec((1, d, f), wmap),
            pl.BlockSpec((1, f, d), wmap),
        ],
        out_specs=pl.BlockSpec(memory_space=pl.ANY),
        scratch_shapes=[
            pltpu.VMEM((N_SLOTS, rows * spr, LANES), jnp.uint32),
            pltpu.VMEM((N_SLOTS, rows * spr, LANES), jnp.uint32),
            pltpu.SemaphoreType.DMA((N_SLOTS,)),
            pltpu.SemaphoreType.DMA((N_SLOTS,)),
        ],
    )
    return pl.pallas_call(
        _expert_kernel,
        grid_spec=grid_spec,
        out_shape=jax.ShapeDtypeStruct((h.shape[0] * TOP_K + N_SLOTS * rows * spr, LANES), jnp.uint32),
        compiler_params=_params(("arbitrary",), disable_bounds_checks=True),
        name="experts",
    )(blk_e, blk_cnt, blk_src, row_tok, row_dst, h, w1, w3, w2)


def _ple_kernel(x1_ref, y0_ref, y1_ref, w_ref, p_ref, gp_ref, wg_ref, wp_ref, gf_ref, o_ref, *, final):
    w = w_ref[...]
    tm, d = x1_ref.shape
    spr = d // SLAB_COLS
    y0 = _load_row_slabs(y0_ref, 0, tm, spr)
    y1 = _load_row_slabs(y1_ref, 0, tm, spr)
    x2 = x1_ref[...] + w[:, 0:1] * y0 + w[:, 1:2] * y1
    hn = _rms(x2, gp_ref[...], 1e-6).astype(BF16)
    gate = _sigmoid(jnp.dot(hn, wg_ref[...], preferred_element_type=F32))
    pp = jnp.dot(p_ref[...].astype(BF16), wp_ref[...], preferred_element_type=F32)
    x3 = x2 + gate * pp
    o_ref[...] = _rms(x3, gf_ref[...], 1e-6) if final else x3


def _ple(x1, y2, w, p2d, g_ple, w_gate, w_proj, g_final, final, tm=256):
    n, d = x1.shape
    pd = p2d.shape[1]
    row = lambda i: (i, 0)
    fixed = lambda i: (0, 0)
    return pl.pallas_call(
        functools.partial(_ple_kernel, final=final),
        grid=(n // tm,),
        in_specs=[
            pl.BlockSpec((tm, d), row),
            pl.BlockSpec((tm * (d // SLAB_COLS), LANES), row),
            pl.BlockSpec((tm * (d // SLAB_COLS), LANES), lambda i: (n // tm + i, 0)),
            pl.BlockSpec((tm, TOP_K), row),
            pl.BlockSpec((tm, pd), row),
            pl.BlockSpec((1, d), fixed),
            pl.BlockSpec((d, d), fixed),
            pl.BlockSpec((pd, d), fixed),
            pl.BlockSpec((1, d), fixed),
        ],
        out_specs=pl.BlockSpec((tm, d), row),
        out_shape=jax.ShapeDtypeStruct((n, d), F32),
        compiler_params=_params(("parallel",)),
        name="ple_final",
    )(x1, y2, y2, w, p2d, g_ple.reshape(1, d).astype(F32), w_gate.astype(BF16), w_proj.astype(BF16),
      g_final.reshape(1, d).astype(F32))


def kernel(x, p, rel_bias, g_mix, w_in, lam_q1, lam_k1, lam_q2, lam_k2, subln_g, ssm_lam_re, ssm_lam_im, ssm_log_dt, ssm_b_re, ssm_b_im, ssm_c_re, ssm_c_im, ssm_d, w_glu, b_glu, ssm_norm_g, w_o, g_ffn, w_router_g, b_router_g, w_router_e, b_router_e, w1, w3, w2, g_ple, w_ple_gate, w_ple_proj, g_final):
    bt, s_len, d = x.shape
    n = bt * s_len
    depth = g_mix.shape[0]
    attn_w = N_HEADS * DV
    xc = x.reshape(n, d).astype(F32)
    for i in range(depth):
        lam_init = 0.8 - 0.6 * math.exp(-0.3 * i)
        col_scale = jnp.concatenate([jnp.full((attn_w,), LOG2E * DK ** -0.5, F32),
                                     jnp.ones((w_in.shape[2] - attn_w,), F32)])
        w_in_b = (w_in[i].astype(F32) * col_scale[None, :]).astype(BF16)
        z, u = _inproj(xc, g_mix[i].astype(F32), w_in_b)
        z3 = z.reshape(bt, s_len, z.shape[1])

        lam = (jnp.exp(jnp.sum(lam_q1[i].astype(F32) * lam_k1[i].astype(F32)))
               - jnp.exp(jnp.sum(lam_q2[i].astype(F32) * lam_k2[i].astype(F32))) + lam_init)
        a = _attention(z3, rel_bias, lam, subln_g[i], lam_init)

        ops = _ssm_operators(ssm_lam_re[i], ssm_lam_im[i], ssm_log_dt[i], ssm_b_re[i], ssm_b_im[i],
                             ssm_c_re[i], ssm_c_im[i], ssm_d[i])
        s = _ssm(u, ops, bt)

        w_r = jnp.zeros((d, ROUTER_LANES), F32)
        w_r = w_r.at[:, :N_EXPERT_GROUPS].set(w_router_g[i].astype(F32))
        w_r = w_r.at[:, N_EXPERT_GROUPS:N_EXPERT_GROUPS + N_EXPERTS].set(w_router_e[i].astype(F32))
        b_r = jnp.zeros((ROUTER_LANES,), F32)
        b_r = b_r.at[:N_EXPERT_GROUPS].set(b_router_g[i].astype(F32))
        b_r = b_r.at[N_EXPERT_GROUPS:N_EXPERT_GROUPS + N_EXPERTS].set(b_router_e[i].astype(F32))
        x1, h, logits = _postmix(a.reshape(n, attn_w), s, xc, w_glu[i], b_glu[i],
                                 ssm_norm_g[i], w_o[i], g_ffn[i], w_r, b_r)

        gate_w, row_tok, row_dst, blk_e, blk_cnt, blk_src = _route(logits, EXPERT_ROWS, d // SLAB_COLS)
        y2 = _experts(h, w1[i], w3[i], w2[i],
                      row_tok, row_dst, blk_e, blk_cnt, blk_src, EXPERT_ROWS)
        xc = _ple(x1, y2, gate_w.astype(F32), p[i].reshape(n, -1).astype(F32),
                  g_ple[i], w_ple_gate[i], w_ple_proj[i], g_final, final=(i == depth - 1))
    return xc.reshape(bt, s_len, d)
```

```python
import functools
import math

import jax
import jax.numpy as jnp
from jax import lax
from jax.experimental import pallas as pl
from jax.experimental.pallas import tpu as pltpu

F32 = jnp.float32
BF16 = jnp.bfloat16
HIGHEST = lax.Precision.HIGHEST

N_HEADS = 8
DK = 64
DV = 128
N_BUCKETS = 32
MAX_DISTANCE = 128
SSM_GROUP = 16
SSM_GROUPS = 64
SSM_STATE = 64
N_EXPERT_GROUPS = 4
EXPERTS_PER_GROUP = 8
N_EXPERTS = 32
TOP_K = 2
MASK_VALUE = -1e30
LOG2E = math.log2(math.e)

SSM_CHUNK = 16
SSM_BUNDLE = 8
ATTN_TQ = 2048
ONES_ROWS = 16
ATTN_CB = 256
ATTN_TK = 256
EXPERT_ROWS = 256
N_SLOTS = 3
MXU_WIDTH = 256
LANES = 128
SLAB_COLS = 2 * LANES
ROUTER_LANES = 128
VMEM_LIMIT = 56 << 20


def _params(semantics, **kw):
    return pltpu.CompilerParams(dimension_semantics=semantics, vmem_limit_bytes=VMEM_LIMIT, **kw)


def _rms(x, g, eps):
    return x * lax.rsqrt(jnp.mean(x * x, axis=-1, keepdims=True) + eps) * g


def _sigmoid(x):
    return 1.0 / (1.0 + jnp.exp(-x))


def _store_row_slabs(ref, row0, x):
    n, d = x.shape
    spr = d // (2 * LANES)
    bits = lax.bitcast_convert_type(x.astype(BF16).astype(F32), jnp.uint32)
    for j in range(spr):
        lo = lax.shift_right_logical(bits[:, j * LANES:(j + 1) * LANES], jnp.uint32(16))
        hi = bits[:, d // 2 + j * LANES:d // 2 + (j + 1) * LANES] & jnp.uint32(0xFFFF0000)
        ref[pl.ds(row0 * spr + j, n, stride=spr), :] = lo | hi


def _load_row_slabs(ref, row0, n, spr):
    words = [ref[pl.ds(row0 * spr + j, n, stride=spr), :] for j in range(spr)]
    lo = [lax.bitcast_convert_type(lax.shift_left(w, jnp.uint32(16)), F32) for w in words]
    hi = [lax.bitcast_convert_type(w & jnp.uint32(0xFFFF0000), F32) for w in words]
    return jnp.concatenate(lo + hi, axis=1)


def _inproj_kernel(x_ref, g_ref, w_ref, z_ref, u_ref, h_scr):
    j = pl.program_id(1)
    last = pl.num_programs(1) - 1

    @pl.when(j == 0)
    def _():
        h_scr[...] = _rms(x_ref[...], g_ref[...], 1e-6).astype(BF16)

    acc = jnp.dot(h_scr[...], w_ref[...], preferred_element_type=F32)

    @pl.when(j < last)
    def _():
        z_ref[...] = acc.astype(z_ref.dtype)

    @pl.when(j == last)
    def _():
        u_ref[...] = acc


def _inproj(x2d, g, w, tm=512, tn=1024):
    n, d = x2d.shape
    pw = w.shape[1]
    n_col = pw // tn
    return pl.pallas_call(
        _inproj_kernel,
        grid=(n // tm, n_col),
        in_specs=[
            pl.BlockSpec((tm, d), lambda i, j: (i, 0)),
            pl.BlockSpec((1, d), lambda i, j: (0, 0)),
            pl.BlockSpec((d, tn), lambda i, j: (0, j)),
        ],
        out_specs=[pl.BlockSpec((tm, tn), lambda i, j: (i, jnp.minimum(j, n_col - 2))),
                   pl.BlockSpec((tm, tn), lambda i, j: (i, 0))],
        out_shape=[jax.ShapeDtypeStruct((n, pw - tn), BF16), jax.ShapeDtypeStruct((n, tn), F32)],
        scratch_shapes=[pltpu.VMEM((tm, d), BF16)],
        compiler_params=_params(("parallel", "arbitrary")),
        name="inproj",
    )(x2d, g.reshape(1, d), w)


def _attn_kernel(lam_ref, q_ref, k_ref, v_ref, bias_ref, g_ref, o_ref, qs, vt, *state,
                 tq, tk, cb, out_scale):
    n_blk = 2 * tq // cb
    m_scr, acc, s_scr = (state[i * n_blk:(i + 1) * n_blk] for i in range(3))
    qi = pl.program_id(2)
    n_sub = tq // tk
    n_kv = v_ref.shape[1] // tk

    @pl.when(qi == 0)
    def _():
        for c in range(n_kv):
            vt[c, 0:DV, :] = v_ref[0, c * tk:(c + 1) * tk, :].astype(F32).T.astype(BF16)
            vt[c, DV:DV + ONES_ROWS, :] = jnp.ones((ONES_ROWS, tk), BF16)

    q = q_ref[0]
    lane = lax.broadcasted_iota(jnp.int32, q.shape, 1)
    zero = jnp.zeros_like(q)
    qs[0:tq, :] = jnp.where(lane < DK, q, zero)
    qs[tq:2 * tq, :] = jnp.where(lane >= DK, q, zero)
    for c in range(n_blk):
        m_scr[c][...] = jnp.full(m_scr[c].shape, MASK_VALUE, F32)
        acc[c][...] = jnp.zeros(acc[c].shape, F32)

    all_blocks = tuple(range(n_blk))

    def visible(r):
        return tuple(c for c in all_blocks if (c * cb) % tq + cb > (r - 1) * tk)

    def scores(j, blocks):
        kj = k_ref[0, pl.ds(pl.multiple_of(j * tk, tk), tk), :]
        return [lax.dot_general(kj, qs[c * cb:(c + 1) * cb, :], (((1,), (1,)), ((), ())),
                                preferred_element_type=F32) for c in blocks]

    def consume(j, near, blocks, next_blocks):
        nxt = scores(j + 1, next_blocks) if next_blocks else None
        vtj = vt[j]
        alphas, ps = [], []
        for c in blocks:
            s = s_scr[c][...]
            if near is not None:
                ahead = (c * cb) % tq - (near - 1) * tk
                if ahead in (0, tk):
                    s = s + bias_ref[0, ahead // tk]
            m_old = m_scr[c][...]
            m_new = jnp.maximum(m_old, jnp.max(s, axis=0, keepdims=True))
            alpha = jnp.exp2(m_old - m_new)
            p = jnp.exp2(s - m_new)
            m_scr[c][...] = m_new
            alphas.append(alpha)
            ps.append(p.astype(BF16))
        pvs = [jnp.dot(vtj, p, preferred_element_type=F32) for p in ps]
        for c, alpha, pv in zip(blocks, alphas, pvs):
            acc[c][...] = alpha * acc[c][...] + pv
        if next_blocks:
            for c, sc in zip(next_blocks, nxt):
                s_scr[c][...] = sc

    first = qi * n_sub - 1
    for c, sc in zip(all_blocks, scores(0, all_blocks)):
        s_scr[c][...] = sc

    def far_pair(i, carry):
        consume(2 * i, None, all_blocks, all_blocks)
        consume(2 * i + 1, None, all_blocks, all_blocks)
        return carry

    lax.fori_loop(0, jnp.maximum(first, 0) // 2, far_pair, 0)

    @pl.when(qi >= 1)
    def _():
        consume(first - 1, None, all_blocks, all_blocks)
        consume(first, 0, all_blocks, all_blocks)

    for r in range(1, n_sub + 1):
        consume(first + r, r, visible(r), visible(r + 1) if r < n_sub else None)

    lam = lam_ref[0, 0]
    half = n_blk // 2
    for b in range(half):
        a1, a2 = acc[b], acc[half + b]
        ot = (a1[0:DV, :] / a1[DV:DV + 1, :] - lam * (a2[0:DV, :] / a2[DV:DV + 1, :]))
        ot = ot * lax.rsqrt(jnp.mean(ot * ot, axis=0, keepdims=True) + 1e-5)
        o_ref[0, b * cb:(b + 1) * cb, :] = (ot.T * (g_ref[...] * out_scale)).astype(o_ref.dtype)


def _t5_bucket(n):
    n = jnp.maximum(n, 0)
    max_exact = N_BUCKETS // 2
    nf = jnp.maximum(n, 1).astype(F32)
    large = max_exact + (jnp.log(nf / max_exact) / math.log(MAX_DISTANCE / max_exact)
                         * (N_BUCKETS - max_exact)).astype(jnp.int32)
    large = jnp.minimum(large, N_BUCKETS - 1)
    return jnp.where(n < max_exact, n, large)


def _attn_bias_tiles(rel_bias, tk):
    assert tk >= MAX_DISTANCE
    table = rel_bias.astype(F32)
    rel_table = (table - table[N_BUCKETS - 1][None, :]) * LOG2E
    r = jnp.arange(tk, dtype=jnp.int32)[:, None]
    c = jnp.arange(tk, dtype=jnp.int32)[None, :]
    tiles = []
    for ahead in (0, tk):
        dist = c + ahead - r
        onehot = (_t5_bucket(dist)[:, :, None] == jnp.arange(N_BUCKETS, dtype=jnp.int32)).astype(F32)
        b = jnp.einsum('rcn,nh->hrc', onehot, rel_table, precision=HIGHEST)
        tiles.append(jnp.where((dist >= 0)[None], b, MASK_VALUE))
    return jnp.stack(tiles, axis=1)


def _attention(z3, rel_bias, lam, subln_g, lam_init, tq=ATTN_TQ, tk=ATTN_TK, cb=ATTN_CB):
    bt, s_len, _ = z3.shape
    assert cb == tk and tq % (2 * tk) == 0
    bias = _attn_bias_tiles(rel_bias, tk)
    n_sp = bias.shape[1]
    kern = functools.partial(_attn_kernel, tq=tq, tk=tk, cb=cb, out_scale=1.0 - lam_init)
    n_blk = 2 * tq // cb
    return pl.pallas_call(
        kern,
        grid=(bt, N_HEADS, s_len // tq),
        in_specs=[
            pl.BlockSpec(memory_space=pltpu.SMEM),
            pl.BlockSpec((1, tq, 2 * DK), lambda b, h, i: (b, i, h)),
            pl.BlockSpec((1, s_len, 2 * DK), lambda b, h, i: (b, 0, N_HEADS + h)),
            pl.BlockSpec((1, s_len, DV), lambda b, h, i: (b, 0, 2 * N_HEADS + h)),
            pl.BlockSpec((1, n_sp, tk, cb), lambda b, h, i: (h, 0, 0, 0)),
            pl.BlockSpec((1, DV), lambda b, h, i: (0, 0)),
        ],
        out_specs=pl.BlockSpec((1, tq, DV), lambda b, h, i: (b, i, h)),
        out_shape=jax.ShapeDtypeStruct((bt, s_len, N_HEADS * DV), BF16),
        scratch_shapes=[
            pltpu.VMEM((2 * tq, 2 * DK), BF16),
            pltpu.VMEM((s_len // tk, DV + ONES_ROWS, tk), BF16),
        ] + [pltpu.VMEM((1, cb), F32)] * n_blk + [pltpu.VMEM((DV + ONES_ROWS, cb), F32)] * n_blk
        + [pltpu.VMEM((tk, cb), F32)] * n_blk,
        compiler_params=_params(("parallel", "parallel", "arbitrary")),
        name="diff_attention",
    )(lam.reshape(1, 1).astype(F32), z3, z3, z3, bias, subln_g.reshape(1, DV).astype(F32))


def _ssm_operators(lam_re, lam_im, log_dt, b_re, b_im, c_re, c_im, d_skip):
    L, H, P = SSM_CHUNK, SSM_GROUP, SSM_STATE
    lre = lam_re.astype(F32)
    lim = lam_im.astype(F32)
    dt = jnp.exp(log_dt.astype(F32))[:, None]
    mag = jnp.exp(lre * dt)
    ab_re = mag * jnp.cos(lim * dt)
    ab_im = mag * jnp.sin(lim * dt)
    den = lre * lre + lim * lim
    nr, ni = ab_re - 1.0, ab_im
    cr = ((nr * lre + ni * lim) / den)[..., None]
    ci = ((ni * lre - nr * lim) / den)[..., None]
    bre = b_re.astype(F32)
    bim = b_im.astype(F32)
    bb_re = cr * bre - ci * bim
    bb_im = cr * bim + ci * bre
    cre = c_re.astype(F32)
    cim = c_im.astype(F32)

    tau = jnp.arange(L + 1, dtype=F32)[:, None, None]
    pw_mag = jnp.exp(tau * (lre * dt)[None])
    pw_re = pw_mag * jnp.cos(tau * (lim * dt)[None])
    pw_im = pw_mag * jnp.sin(tau * (lim * dt)[None])

    ca_re = cre[None] * pw_re[:, :, None, :] - cim[None] * pw_im[:, :, None, :]
    ca_im = cre[None] * pw_im[:, :, None, :] + cim[None] * pw_re[:, :, None, :]
    bbt_re = bb_re.transpose(0, 2, 1)[None, :, None]
    bbt_im = bb_im.transpose(0, 2, 1)[None, :, None]
    k_tau = jnp.sum(ca_re[:L, :, :, None, :] * bbt_re - ca_im[:L, :, :, None, :] * bbt_im, axis=-1)
    ti = jnp.arange(L)
    t_src = jnp.tile(k_tau.transpose(0, 1, 3, 2).reshape(L, SSM_GROUPS * H, H), (1, 1, SSM_BUNDLE))

    rev_re = pw_re[L - 1 - ti]
    rev_im = pw_im[L - 1 - ti]
    w_re = rev_re[..., None] * bb_re[None] - rev_im[..., None] * bb_im[None]
    w_im = rev_re[..., None] * bb_im[None] + rev_im[..., None] * bb_re[None]
    w_src = jnp.concatenate([w_re, w_im], axis=2).transpose(0, 1, 3, 2)

    e_src = jnp.concatenate([ca_re[1:L + 1], -ca_im[1:L + 1]], axis=3)

    a1 = jnp.concatenate([pw_re[L], pw_re[L]], axis=-1)
    a2 = jnp.concatenate([-pw_im[L], pw_im[L]], axis=-1)
    d_tile = jnp.tile(d_skip.astype(F32).reshape(SSM_GROUPS // SSM_BUNDLE, 1, SSM_BUNDLE * H), (1, 1, L))
    return t_src.astype(BF16), w_src.astype(BF16), e_src.astype(BF16), a1, a2, d_tile


def _bundle_operator(dst, src_ref):
    n_l, n_g, h, x = src_ref.shape
    dst[...] = jnp.zeros(dst.shape, dst.dtype)
    for i in range(n_l):
        for g in range(n_g):
            r0 = (i * n_g + g) * h
            dst[r0:r0 + h, g * x:(g + 1) * x] = src_ref[i, g]


def _chunk_rows(u_ref, n_chunk):
    return jnp.concatenate([u_ref[pl.ds(t, n_chunk, stride=SSM_CHUNK), :] for t in range(SSM_CHUNK)], axis=1)


def _ssm_in_kernel(u_ref, w_ref, v_ref, w_scr):
    @pl.when(pl.program_id(1) == 0)
    def _():
        _bundle_operator(w_scr, w_ref)

    x = _chunk_rows(u_ref, v_ref.shape[0]).astype(BF16)
    v = jnp.dot(x, w_scr[...], preferred_element_type=F32)
    sw = v_ref.shape[2]
    for g in range(v_ref.shape[1]):
        v_ref[:, g, :] = v[:, g * sw:(g + 1) * sw]


def _ssm_scan_kernel(v_ref, a1_ref, a2_ref, o_ref, st):
    @pl.when(pl.program_id(0) == 0)
    def _():
        st[...] = jnp.zeros(st.shape, F32)

    a1 = a1_ref[...][None]
    a2 = a2_ref[...][None]
    n_chunk = v_ref.shape[1]
    half = v_ref.shape[3] // 2

    def body(c, s):
        o_ref[:, pl.ds(c, 1)] = s[:, None].astype(o_ref.dtype)
        v = v_ref[:, pl.ds(c, 1)][:, 0]
        return a1 * s + a2 * pltpu.roll(s, half, axis=2) + v

    st[...] = lax.fori_loop(0, n_chunk, body, st[...])


def _gelu_tanh(x):
    c = math.sqrt(2.0 / math.pi)
    return x * (0.5 * (1.0 + jnp.tanh(c * (x + 0.044715 * (x * x * x)))))


def _ssm_out_kernel(u_ref, t_ref, s_ref, e_ref, d_ref, y_ref, t_scr, et_scr):
    L, lw, h = SSM_CHUNK, t_ref.shape[1], SSM_GROUP

    @pl.when(pl.program_id(1) == 0)
    def _():
        _bundle_operator(et_scr, e_ref)
        same_group = (lax.broadcasted_iota(jnp.int32, (lw, lw), 0) // h
                      == lax.broadcasted_iota(jnp.int32, (lw, lw), 1) // h)
        t_scr[...] = jnp.zeros(t_scr.shape, t_scr.dtype)
        for tau in range(L):
            blk = jnp.where(same_group, t_ref[tau], jnp.zeros((lw, lw), t_ref.dtype))
            for i in range(L - tau):
                j = i + tau
                t_scr[i * lw:(i + 1) * lw, j * lw:(j + 1) * lw] = blk

    n_chunk = s_ref.shape[0]
    xf = _chunk_rows(u_ref, n_chunk)
    sp = jnp.concatenate([s_ref[:, g, :] for g in range(s_ref.shape[1])], axis=1).astype(BF16)
    xb = xf.astype(BF16)
    step = MXU_WIDTH
    y = jnp.concatenate([jnp.dot(xb[:, :c + step], t_scr[:c + step, c:c + step], preferred_element_type=F32)
                         for c in range(0, L * lw, step)], axis=1)
    y = y + lax.dot_general(sp, et_scr[...], (((1,), (1,)), ((), ())), preferred_element_type=F32)
    y = _gelu_tanh(y + d_ref[0] * xf)
    for t in range(L):
        y_ref[pl.ds(t, n_chunk, stride=L), :] = y[:, t * lw:(t + 1) * lw]


def _ssm(u, ops, bt, scan_block=32, chunk_tile=256):
    t_src, w_src, e_src, a1, a2, d_tile = ops
    n, width = u.shape
    L, G, H, P2, nbg = SSM_CHUNK, SSM_GROUPS, SSM_GROUP, 2 * SSM_STATE, SSM_BUNDLE
    nb = G // nbg
    lw = nbg * H
    nc = n // L
    n_c = nc // bt
    ct = min(chunk_tile, nc)
    u_spec = pl.BlockSpec((ct * L, lw), lambda b, i: (i, b))
    src_spec = pl.BlockSpec((L, nbg, H, P2), lambda b, i: (0, b, 0, 0))
    state_spec = pl.BlockSpec((ct, nbg, P2), lambda b, i: (i, b, 0))

    v = pl.pallas_call(
        _ssm_in_kernel,
        grid=(nb, nc // ct),
        in_specs=[u_spec, src_spec],
        out_specs=state_spec,
        out_shape=jax.ShapeDtypeStruct((nc, G, P2), F32),
        scratch_shapes=[pltpu.VMEM((L * lw, nbg * P2), BF16)],
        compiler_params=_params(("parallel", "arbitrary")),
        name="ssm_chunk_state",
    )(u, w_src)

    cb = min(scan_block, n_c)
    s_prev = pl.pallas_call(
        _ssm_scan_kernel,
        grid=(n_c // cb,),
        in_specs=[pl.BlockSpec((bt, cb, G, P2), lambda c: (0, c, 0, 0)),
                  pl.BlockSpec((G, P2), lambda c: (0, 0)),
                  pl.BlockSpec((G, P2), lambda c: (0, 0))],
        out_specs=pl.BlockSpec((bt, cb, G, P2), lambda c: (0, c, 0, 0)),
        out_shape=jax.ShapeDtypeStruct((bt, n_c, G, P2), F32),
        scratch_shapes=[pltpu.VMEM((bt, G, P2), F32)],
        compiler_params=_params(("arbitrary",)),
        name="ssm_scan",
    )(v.reshape(bt, n_c, G, P2), a1, a2)

    return pl.pallas_call(
        _ssm_out_kernel,
        grid=(nb, nc // ct),
        in_specs=[u_spec,
                  pl.BlockSpec((L, lw, lw), lambda b, i: (0, b, 0)),
                  state_spec,
                  src_spec,
                  pl.BlockSpec((1, 1, L * lw), lambda b, i: (b, 0, 0))],
        out_specs=pl.BlockSpec((ct * L, lw), lambda b, i: (i, b)),
        out_shape=jax.ShapeDtypeStruct((n, width), F32),
        scratch_shapes=[pltpu.VMEM((L * lw, L * lw), BF16), pltpu.VMEM((L * lw, nbg * P2), BF16)],
        compiler_params=_params(("parallel", "arbitrary")),
        name="ssm_output",
    )(u, t_src, s_prev.reshape(nc, G, P2), e_src, d_tile)


def _postmix_kernel(a_ref, s_ref, x_ref, wg_ref, bg_ref, gs_ref, woa_ref, wos_ref, gf_ref,
                    wrh_ref, wrl_ref, br_ref, x1_ref, h_ref, lg_ref):
    sf = s_ref[...]
    gate = _sigmoid(jnp.dot(sf.astype(BF16), wg_ref[...], preferred_element_type=F32) + bg_ref[...])
    sn = _rms(sf * gate, gs_ref[...], 1e-6).astype(BF16)
    x1 = (x_ref[...]
          + jnp.dot(a_ref[...], woa_ref[...], preferred_element_type=F32)
          + jnp.dot(sn, wos_ref[...], preferred_element_type=F32))
    x1_ref[...] = x1
    h = _rms(x1, gf_ref[...], 1e-6)
    h_hi = h.astype(BF16)
    h_lo = (h - h_hi.astype(F32)).astype(BF16)
    _store_row_slabs(h_ref, 0, h_hi)
    lg_ref[...] = (jnp.dot(h_hi, wrh_ref[...], preferred_element_type=F32)
                   + jnp.dot(h_lo, wrh_ref[...], preferred_element_type=F32)
                   + jnp.dot(h_hi, wrl_ref[...], preferred_element_type=F32)
                   + br_ref[...])


def _postmix(a, s, x2d, w_glu, b_glu, g_s, w_o, g_ffn, w_r, b_r, tm=256):
    n, d = x2d.shape
    wa = a.shape[1]
    ws = s.shape[1]
    wr_hi = w_r.astype(BF16)
    wr_lo = (w_r - wr_hi.astype(F32)).astype(BF16)
    row = lambda i: (i, 0)
    fixed = lambda i: (0, 0)
    return pl.pallas_call(
        _postmix_kernel,
        grid=(n // tm,),
        in_specs=[
            pl.BlockSpec((tm, wa), row),
            pl.BlockSpec((tm, ws), row),
            pl.BlockSpec((tm, d), row),
            pl.BlockSpec((ws, ws), fixed),
            pl.BlockSpec((1, ws), fixed),
            pl.BlockSpec((1, ws), fixed),
            pl.BlockSpec((wa, d), fixed),
            pl.BlockSpec((ws, d), fixed),
            pl.BlockSpec((1, d), fixed),
            pl.BlockSpec((d, ROUTER_LANES), fixed),
            pl.BlockSpec((d, ROUTER_LANES), fixed),
            pl.BlockSpec((1, ROUTER_LANES), fixed),
        ],
        out_specs=[pl.BlockSpec((tm, d), row), pl.BlockSpec((tm * (d // SLAB_COLS), LANES), row),
                   pl.BlockSpec((tm, ROUTER_LANES), row)],
        out_shape=[jax.ShapeDtypeStruct((n, d), F32), jax.ShapeDtypeStruct((n * (d // SLAB_COLS), LANES), jnp.uint32),
                   jax.ShapeDtypeStruct((n, ROUTER_LANES), F32)],
        compiler_params=_params(("parallel",)),
        name="postmix",
    )(a, s, x2d, w_glu.astype(BF16), b_glu.reshape(1, ws).astype(F32), g_s.reshape(1, ws).astype(F32),
      w_o[:wa].astype(BF16), w_o[wa:].astype(BF16), g_ffn.reshape(1, d).astype(F32),
      wr_hi, wr_lo, b_r.reshape(1, ROUTER_LANES).astype(F32))


def _route(logits, rows, slab_rows):
    n_tok = logits.shape[0]
    lg = logits[:, :N_EXPERT_GROUPS]
    le = logits[:, N_EXPERT_GROUPS:N_EXPERT_GROUPS + N_EXPERTS].reshape(
        n_tok, N_EXPERT_GROUPS, EXPERTS_PER_GROUP)
    pg = jax.nn.softmax(lg, axis=-1)
    gsel = jnp.argmax(lg, axis=-1).astype(jnp.int32)
    gate_g = jnp.max(pg, axis=-1, keepdims=True)
    sel = gsel[:, None] == jnp.arange(N_EXPERT_GROUPS, dtype=jnp.int32)[None, :]
    le_sel = jnp.sum(jnp.where(sel[:, :, None], le, 0.0), axis=1)
    pe = jax.nn.softmax(le_sel, axis=-1)
    top_p, top_i = lax.top_k(pe, TOP_K)
    w = gate_g * top_p / jnp.sum(top_p, axis=-1, keepdims=True)
    eid = gsel[:, None] * EXPERTS_PER_GROUP + top_i.astype(jnp.int32)

    n_assign = n_tok * TOP_K
    flat_e = eid.reshape(-1)
    counts = jnp.sum((flat_e[None, :] == jnp.arange(N_EXPERTS, dtype=jnp.int32)[:, None])
                     .astype(jnp.int32), axis=1)
    n_blk_e = (counts + rows - 1) // rows
    blk_end = jnp.cumsum(n_blk_e)
    blk_first = blk_end - n_blk_e
    start = jnp.cumsum(counts) - counts
    order = jnp.argsort(flat_e, stable=True).astype(jnp.int32)
    n_blk = (n_assign + rows - 1) // rows + N_EXPERTS
    blk = jnp.arange(n_blk, dtype=jnp.int32)
    blk_e = jnp.minimum(jnp.sum((blk[:, None] >= blk_end[None, :]).astype(jnp.int32), axis=1),
                        N_EXPERTS - 1)
    used = blk < blk_end[-1]
    experts = jnp.arange(N_EXPERTS, dtype=jnp.int32)
    is_e = blk_e[:, None] == experts[None, :]

    def per_block(table):
        return jnp.sum(jnp.where(is_e, table[None, :], 0), axis=1)

    in_e = (blk - per_block(blk_first)) * rows
    blk_cnt = jnp.where(used, jnp.clip(per_block(counts) - in_e, 0, rows), 0).astype(jnp.int32)
    blk_src = jnp.where(used, per_block(start) + in_e, 0).astype(jnp.int32)
    last_e = jnp.max(jnp.where(counts > 0, experts, 0))
    blk_e = jnp.where(used, blk_e, last_e)
    row_tok = order // TOP_K
    row_dst = (order % TOP_K) * n_tok + row_tok
    pad = jnp.zeros((rows,), jnp.int32)
    return (w, jnp.concatenate([row_tok * slab_rows, pad]), jnp.concatenate([row_dst * slab_rows, pad]),
            blk_e, blk_cnt, blk_src)


def _expert_kernel(blk_e_ref, blk_cnt_ref, blk_src_ref, tok_ref, dst_ref,
                   h_hbm, w1_ref, w3_ref, w2_ref, y_hbm, xbuf, ybuf, gsem, ssem):
    del blk_e_ref
    b = pl.program_id(0)
    nb = pl.num_programs(0)
    slot = b % N_SLOTS
    slot_m1 = (b + N_SLOTS - 1) % N_SLOTS
    slot_p1 = (b + 1) % N_SLOTS
    spr = w1_ref.shape[1] // SLAB_COLS
    rows = xbuf.shape[1] // spr
    dump = y_hbm.shape[0] - N_SLOTS * rows * spr
    cnt = blk_cnt_ref[b]
    prev = jnp.maximum(b - 1, 0)
    prev_cnt = jnp.where(b >= 1, blk_cnt_ref[prev], 0)

    def gather_copy(src, r, sl):
        return pltpu.make_async_copy(h_hbm.at[pl.ds(pl.multiple_of(tok_ref[src + r], spr), spr)],
                                     xbuf.at[sl, pl.ds(r * spr, spr)], gsem.at[sl])

    def scatter_copy(src, n_valid, r, sl):
        dst = jnp.where(r < n_valid, dst_ref[src + r], dump + (sl * rows + r) * spr)
        return pltpu.make_async_copy(ybuf.at[sl, pl.ds(r * spr, spr)],
                                     y_hbm.at[pl.ds(pl.multiple_of(dst, spr), spr)], ssem.at[sl])

    def start_gathers(blk, sl, lo=0, hi=None):
        src = blk_src_ref[blk]
        for r in range(lo, rows if hi is None else hi):
            gather_copy(src, r, sl).start()

    def wait_gathers(sl):
        for r in range(rows):
            gather_copy(0, r, sl).wait()

    def start_scatters(blk, n_valid, sl, lo=0, hi=None):
        src = blk_src_ref[blk]
        for r in range(lo, rows if hi is None else hi):
            scatter_copy(src, n_valid, r, sl).start()

    def wait_scatters(sl):
        for r in range(rows):
            scatter_copy(0, 0, r, sl).wait()

    @pl.when(b == 0)
    def _():
        ybuf[...] = jnp.zeros(ybuf.shape, ybuf.dtype)
        start_gathers(0, 0)
        start_gathers(1, 1)
        for sl in range(N_SLOTS - 1):
            start_scatters(0, 0, sl)
        for sl in range(N_SLOTS - 1):
            wait_scatters(sl)

    @pl.when(cnt > 0)
    def _():
        @pl.when(b >= 2)
        def _():
            wait_scatters(slot)

        wait_gathers(slot)
        x = _load_row_slabs(xbuf.at[slot], 0, rows, spr).astype(BF16)
        f = w1_ref.shape[2]
        n_f = f // MXU_WIDTH
        n_part = 3 * n_f
        bounds = [rows * i // n_part for i in range(n_part + 1)]
        part = iter(range(n_part))

        def issue_part():
            i = next(part)
            start_gathers(jnp.minimum(b + 2, nb - 1), slot_m1, bounds[i], bounds[i + 1])
            start_scatters(prev, prev_cnt, slot_m1, bounds[i], bounds[i + 1])

        acts = []
        for c in range(n_f):
            cols = slice(c * MXU_WIDTH, (c + 1) * MXU_WIDTH)
            issue_part()
            h1 = jnp.dot(x, w1_ref[0, :, cols].astype(BF16), preferred_element_type=F32)
            issue_part()
            h3 = jnp.dot(x, w3_ref[0, :, cols].astype(BF16), preferred_element_type=F32)
            acts.append((h1 * _sigmoid(h1) * h3).astype(BF16))
        y = None
        for c in range(n_f):
            issue_part()
            yc = jnp.dot(acts[c], w2_ref[0, c * MXU_WIDTH:(c + 1) * MXU_WIDTH, :].astype(BF16),
                         preferred_element_type=F32)
            y = yc if y is None else y + yc
        _store_row_slabs(ybuf.at[slot], 0, y)

    @pl.when((cnt == 0) & (prev_cnt > 0))
    def _():
        wait_gathers(slot)
        wait_gathers(slot_p1)

        @pl.when(b >= 2)
        def _():
            wait_scatters(slot)

        wait_scatters(slot_p1)
        start_scatters(prev, prev_cnt, slot_m1)
        wait_scatters(slot_m1)


def _experts(h, w1, w3, w2, row_tok, row_dst, blk_e, blk_cnt, blk_src, rows):
    d, f = w1.shape[1], w1.shape[2]
    spr = d // SLAB_COLS
    n_blk = blk_e.shape[0]
    wmap = lambda b, be, bc, bs, rt, rd: (be[b], 0, 0)
    grid_spec = pltpu.PrefetchScalarGridSpec(
        num_scalar_prefetch=5,
        grid=(n_blk,),
        in_specs=[
            pl.BlockSpec(memory_space=pl.ANY),
            pl.BlockSpec((1, d, f), wmap),
            pl.BlockSpec((1, d, f), wmap),
            pl.BlockSpec((1, f, d), wmap),
        ],
        out_specs=pl.BlockSpec(memory_space=pl.ANY),
        scratch_shapes=[
            pltpu.VMEM((N_SLOTS, rows * spr, LANES), jnp.uint32),
            pltpu.VMEM((N_SLOTS, rows * spr, LANES), jnp.uint32),
            pltpu.SemaphoreType.DMA((N_SLOTS,)),
            pltpu.SemaphoreType.DMA((N_SLOTS,)),
        ],
    )
    return pl.pallas_call(
        _expert_kernel,
        grid_spec=grid_spec,
        out_shape=jax.ShapeDtypeStruct((h.shape[0] * TOP_K + N_SLOTS * rows * spr, LANES), jnp.uint32),
        compiler_params=_params(("arbitrary",), disable_bounds_checks=True),
        name="experts",
    )(blk_e, blk_cnt, blk_src, row_tok, row_dst, h, w1, w3, w2)


def _ple_kernel(x1_ref, y0_ref, y1_ref, w_ref, p_ref, gp_ref, wg_ref, wp_ref, gf_ref, o_ref, *, final):
    w = w_ref[...]
    tm, d = x1_ref.shape
    spr = d // SLAB_COLS
    y0 = _load_row_slabs(y0_ref, 0, tm, spr)
    y1 = _load_row_slabs(y1_ref, 0, tm, spr)
    x2 = x1_ref[...] + w[:, 0:1] * y0 + w[:, 1:2] * y1
    hn = _rms(x2, gp_ref[...], 1e-6).astype(BF16)
    gate = _sigmoid(jnp.dot(hn, wg_ref[...], preferred_element_type=F32))
    pp = jnp.dot(p_ref[...].astype(BF16), wp_ref[...], preferred_element_type=F32)
    x3 = x2 + gate * pp
    o_ref[...] = _rms(x3, gf_ref[...], 1e-6) if final else x3


def _ple(x1, y2, w, p2d, g_ple, w_gate, w_proj, g_final, final, tm=256):
    n, d = x1.shape
    pd = p2d.shape[1]
    row = lambda i: (i, 0)
    fixed = lambda i: (0, 0)
    return pl.pallas_call(
        functools.partial(_ple_kernel, final=final),
        grid=(n // tm,),
        in_specs=[
            pl.BlockSpec((tm, d), row),
            pl.BlockSpec((tm * (d // SLAB_COLS), LANES), row),
            pl.BlockSpec((tm * (d // SLAB_COLS), LANES), lambda i: (n // tm + i, 0)),
            pl.BlockSpec((tm, TOP_K), row),
            pl.BlockSpec((tm, pd), row),
            pl.BlockSpec((1, d), fixed),
            pl.BlockSpec((d, d), fixed),
            pl.BlockSpec((pd, d), fixed),
            pl.BlockSpec((1, d), fixed),
        ],
        out_specs=pl.BlockSpec((tm, d), row),
        out_shape=jax.ShapeDtypeStruct((n, d), F32),
        compiler_params=_params(("parallel",)),
        name="ple_final",
    )(x1, y2, y2, w, p2d, g_ple.reshape(1, d).astype(F32), w_gate.astype(BF16), w_proj.astype(BF16),
      g_final.reshape(1, d).astype(F32))


def kernel(x, p, rel_bias, g_mix, w_in, lam_q1, lam_k1, lam_q2, lam_k2, subln_g, ssm_lam_re, ssm_lam_im, ssm_log_dt, ssm_b_re, ssm_b_im, ssm_c_re, ssm_c_im, ssm_d, w_glu, b_glu, ssm_norm_g, w_o, g_ffn, w_router_g, b_router_g, w_router_e, b_router_e, w1, w3, w2, g_ple, w_ple_gate, w_ple_proj, g_final):
    bt, s_len, d = x.shape
    n = bt * s_len
    depth = g_mix.shape[0]
    attn_w = N_HEADS * DV
    xc = x.reshape(n, d).astype(F32)
    for i in range(depth):
        lam_init = 0.8 - 0.6 * math.exp(-0.3 * i)
        col_scale = jnp.concatenate([jnp.full((attn_w,), LOG2E * DK ** -0.5, F32),
                                     jnp.ones((w_in.shape[2] - attn_w,), F32)])
        w_in_b = (w_in[i].astype(F32) * col_scale[None, :]).astype(BF16)
        z, u = _inproj(xc, g_mix[i].astype(F32), w_in_b)
        z3 = z.reshape(bt, s_len, z.shape[1])

        lam = (jnp.exp(jnp.sum(lam_q1[i].astype(F32) * lam_k1[i].astype(F32)))
               - jnp.exp(jnp.sum(lam_q2[i].astype(F32) * lam_k2[i].astype(F32))) + lam_init)
        a = _attention(z3, rel_bias, lam, subln_g[i], lam_init)

        ops = _ssm_operators(ssm_lam_re[i], ssm_lam_im[i], ssm_log_dt[i], ssm_b_re[i], ssm_b_im[i],
                             ssm_c_re[i], ssm_c_im[i], ssm_d[i])
        s = _ssm(u, ops, bt)

        w_r = jnp.zeros((d, ROUTER_LANES), F32)
        w_r = w_r.at[:, :N_EXPERT_GROUPS].set(w_router_g[i].astype(F32))
        w_r = w_r.at[:, N_EXPERT_GROUPS:N_EXPERT_GROUPS + N_EXPERTS].set(w_router_e[i].astype(F32))
        b_r = jnp.zeros((ROUTER_LANES,), F32)
        b_r = b_r.at[:N_EXPERT_GROUPS].set(b_router_g[i].astype(F32))
        b_r = b_r.at[N_EXPERT_GROUPS:N_EXPERT_GROUPS + N_EXPERTS].set(b_router_e[i].astype(F32))
        x1, h, logits = _postmix(a.reshape(n, attn_w), s, xc, w_glu[i], b_glu[i],
                                 ssm_norm_g[i], w_o[i], g_ffn[i], w_r, b_r)

        gate_w, row_tok, row_dst, blk_e, blk_cnt, blk_src = _route(logits, EXPERT_ROWS, d // SLAB_COLS)
        y2 = _experts(h, w1[i], w3[i], w2[i],
                      row_tok, row_dst, blk_e, blk_cnt, blk_src, EXPERT_ROWS)
        xc = _ple(x1, y2, gate_w.astype(F32), p[i].reshape(n, -1).astype(F32),
                  g_ple[i], w_ple_gate[i], w_ple_proj[i], g_final, final=(i == depth - 1))
    return xc.reshape(bt, s_len, d)
```

```python
import functools
import math

import jax
import jax.numpy as jnp
from jax import lax
from jax.experimental import pallas as pl
from jax.experimental.pallas import tpu as pltpu

F32 = jnp.float32
BF16 = jnp.bfloat16
HIGHEST = lax.Precision.HIGHEST

N_HEADS = 8
DK = 64
DV = 128
N_BUCKETS = 32
MAX_DISTANCE = 128
SSM_GROUP = 16
SSM_GROUPS = 64
SSM_STATE = 64
N_EXPERT_GROUPS = 4
EXPERTS_PER_GROUP = 8
N_EXPERTS = 32
TOP_K = 2
MASK_VALUE = -1e30
LOG2E = math.log2(math.e)

SSM_CHUNK = 16
SSM_BUNDLE = 8
ATTN_TQ = 2048
ONES_ROWS = 16
ATTN_CB = 256
ATTN_TK = 256
EXPERT_ROWS = 256
N_SLOTS = 3
MXU_WIDTH = 256
LANES = 128
SLAB_COLS = 2 * LANES
ROUTER_LANES = 128
VMEM_LIMIT = 56 << 20


def _params(semantics, **kw):
    return pltpu.CompilerParams(dimension_semantics=semantics, vmem_limit_bytes=VMEM_LIMIT, **kw)


def _rms(x, g, eps):
    return x * lax.rsqrt(jnp.mean(x * x, axis=-1, keepdims=True) + eps) * g


def _sigmoid(x):
    return 1.0 / (1.0 + jnp.exp(-x))


def _store_row_slabs(ref, row0, x):
    n, d = x.shape
    spr = d // (2 * LANES)
    bits = lax.bitcast_convert_type(x.astype(BF16).astype(F32), jnp.uint32)
    for j in range(spr):
        lo = lax.shift_right_logical(bits[:, j * LANES:(j + 1) * LANES], jnp.uint32(16))
        hi = bits[:, d // 2 + j * LANES:d // 2 + (j + 1) * LANES] & jnp.uint32(0xFFFF0000)
        ref[pl.ds(row0 * spr + j, n, stride=spr), :] = lo | hi


def _load_row_slabs(ref, row0, n, spr):
    words = [ref[pl.ds(row0 * spr + j, n, stride=spr), :] for j in range(spr)]
    lo = [lax.bitcast_convert_type(lax.shift_left(w, jnp.uint32(16)), F32) for w in words]
    hi = [lax.bitcast_convert_type(w & jnp.uint32(0xFFFF0000), F32) for w in words]
    return jnp.concatenate(lo + hi, axis=1)


def _inproj_kernel(x_ref, g_ref, w_ref, z_ref, u_ref, h_scr):
    j = pl.program_id(1)
    last = pl.num_programs(1) - 1

    @pl.when(j == 0)
    def _():
        h_scr[...] = _rms(x_ref[...], g_ref[...], 1e-6).astype(BF16)

    acc = jnp.dot(h_scr[...], w_ref[...], preferred_element_type=F32)

    @pl.when(j < last)
    def _():
        z_ref[...] = acc.astype(z_ref.dtype)

    @pl.when(j == last)
    def _():
        u_ref[...] = acc


def _inproj(x2d, g, w, tm=1024, tn=1024):
    n, d = x2d.shape
    pw = w.shape[1]
    n_col = pw // tn
    return pl.pallas_call(
        _inproj_kernel,
        grid=(n // tm, n_col),
        in_specs=[
            pl.BlockSpec((tm, d), lambda i, j: (i, 0)),
            pl.BlockSpec((1, d), lambda i, j: (0, 0)),
            pl.BlockSpec((d, tn), lambda i, j: (0, j)),
        ],
        out_specs=[pl.BlockSpec((tm, tn), lambda i, j: (i, jnp.minimum(j, n_col - 2))),
                   pl.BlockSpec((tm, tn), lambda i, j: (i, 0))],
        out_shape=[jax.ShapeDtypeStruct((n, pw - tn), BF16), jax.ShapeDtypeStruct((n, tn), F32)],
        scratch_shapes=[pltpu.VMEM((tm, d), BF16)],
        compiler_params=_params(("parallel", "arbitrary")),
        name="inproj",
    )(x2d, g.reshape(1, d), w)


def _attn_kernel(lam_ref, q_ref, k_ref, v_ref, bias_ref, g_ref, o_ref, qs, vt, *state,
                 tq, tk, cb, out_scale):
    n_blk = 2 * tq // cb
    m_scr, acc, s_scr = (state[i * n_blk:(i + 1) * n_blk] for i in range(3))
    qi = pl.program_id(2)
    n_sub = tq // tk
    n_kv = v_ref.shape[1] // tk

    @pl.when(qi == 0)
    def _():
        for c in range(n_kv):
            vt[c, 0:DV, :] = v_ref[0, c * tk:(c + 1) * tk, :].astype(F32).T.astype(BF16)
            vt[c, DV:DV + ONES_ROWS, :] = jnp.ones((ONES_ROWS, tk), BF16)

    q = q_ref[0]
    lane = lax.broadcasted_iota(jnp.int32, q.shape, 1)
    zero = jnp.zeros_like(q)
    qs[0:tq, :] = jnp.where(lane < DK, q, zero)
    qs[tq:2 * tq, :] = jnp.where(lane >= DK, q, zero)
    for c in range(n_blk):
        m_scr[c][...] = jnp.full(m_scr[c].shape, MASK_VALUE, F32)
        acc[c][...] = jnp.zeros(acc[c].shape, F32)

    all_blocks = tuple(range(n_blk))

    def visible(r):
        return tuple(c for c in all_blocks if (c * cb) % tq + cb > (r - 1) * tk)

    def scores(j, blocks):
        kj = k_ref[0, pl.ds(pl.multiple_of(j * tk, tk), tk), :]
        return [lax.dot_general(kj, qs[c * cb:(c + 1) * cb, :], (((1,), (1,)), ((), ())),
                                preferred_element_type=F32) for c in blocks]

    def consume(j, near, blocks, next_blocks):
        nxt = scores(j + 1, next_blocks) if next_blocks else None
        vtj = vt[j]
        alphas, ps = [], []
        for c in blocks:
            s = s_scr[c][...]
            if near is not None:
                ahead = (c * cb) % tq - (near - 1) * tk
                if ahead in (0, tk):
                    s = s + bias_ref[0, ahead // tk]
            m_old = m_scr[c][...]
            m_new = jnp.maximum(m_old, jnp.max(s, axis=0, keepdims=True))
            alpha = jnp.exp2(m_old - m_new)
            p = jnp.exp2(s - m_new)
            m_scr[c][...] = m_new
            alphas.append(alpha)
            ps.append(p.astype(BF16))
        pvs = [jnp.dot(vtj, p, preferred_element_type=F32) for p in ps]
        for c, alpha, pv in zip(blocks, alphas, pvs):
            acc[c][...] = alpha * acc[c][...] + pv
        if next_blocks:
            for c, sc in zip(next_blocks, nxt):
                s_scr[c][...] = sc

    first = qi * n_sub - 1
    for c, sc in zip(all_blocks, scores(0, all_blocks)):
        s_scr[c][...] = sc

    def far_pair(i, carry):
        consume(2 * i, None, all_blocks, all_blocks)
        consume(2 * i + 1, None, all_blocks, all_blocks)
        return carry

    lax.fori_loop(0, jnp.maximum(first, 0) // 2, far_pair, 0)

    @pl.when(qi >= 1)
    def _():
        consume(first - 1, None, all_blocks, all_blocks)
        consume(first, 0, all_blocks, all_blocks)

    for r in range(1, n_sub + 1):
        consume(first + r, r, visible(r), visible(r + 1) if r < n_sub else None)

    lam = lam_ref[0, 0]
    half = n_blk // 2
    for b in range(half):
        a1, a2 = acc[b], acc[half + b]
        ot = (a1[0:DV, :] / a1[DV:DV + 1, :] - lam * (a2[0:DV, :] / a2[DV:DV + 1, :]))
        ot = ot * lax.rsqrt(jnp.mean(ot * ot, axis=0, keepdims=True) + 1e-5)
        o_ref[0, b * cb:(b + 1) * cb, :] = (ot.T * (g_ref[...] * out_scale)).astype(o_ref.dtype)


def _t5_bucket(n):
    n = jnp.maximum(n, 0)
    max_exact = N_BUCKETS // 2
    nf = jnp.maximum(n, 1).astype(F32)
    large = max_exact + (jnp.log(nf / max_exact) / math.log(MAX_DISTANCE / max_exact)
                         * (N_BUCKETS - max_exact)).astype(jnp.int32)
    large = jnp.minimum(large, N_BUCKETS - 1)
    return jnp.where(n < max_exact, n, large)


def _attn_bias_tiles(rel_bias, tk):
    assert tk >= MAX_DISTANCE
    table = rel_bias.astype(F32)
    rel_table = (table - table[N_BUCKETS - 1][None, :]) * LOG2E
    r = jnp.arange(tk, dtype=jnp.int32)[:, None]
    c = jnp.arange(tk, dtype=jnp.int32)[None, :]
    tiles = []
    for ahead in (0, tk):
        dist = c + ahead - r
        onehot = (_t5_bucket(dist)[:, :, None] == jnp.arange(N_BUCKETS, dtype=jnp.int32)).astype(F32)
        b = jnp.einsum('rcn,nh->hrc', onehot, rel_table, precision=HIGHEST)
        tiles.append(jnp.where((dist >= 0)[None], b, MASK_VALUE))
    return jnp.stack(tiles, axis=1)


def _attention(z3, rel_bias, lam, subln_g, lam_init, tq=ATTN_TQ, tk=ATTN_TK, cb=ATTN_CB):
    bt, s_len, _ = z3.shape
    assert cb == tk and tq % (2 * tk) == 0
    bias = _attn_bias_tiles(rel_bias, tk)
    n_sp = bias.shape[1]
    kern = functools.partial(_attn_kernel, tq=tq, tk=tk, cb=cb, out_scale=1.0 - lam_init)
    n_blk = 2 * tq // cb
    return pl.pallas_call(
        kern,
        grid=(bt, N_HEADS, s_len // tq),
        in_specs=[
            pl.BlockSpec(memory_space=pltpu.SMEM),
            pl.BlockSpec((1, tq, 2 * DK), lambda b, h, i: (b, i, h)),
            pl.BlockSpec((1, s_len, 2 * DK), lambda b, h, i: (b, 0, N_HEADS + h)),
            pl.BlockSpec((1, s_len, DV), lambda b, h, i: (b, 0, 2 * N_HEADS + h)),
            pl.BlockSpec((1, n_sp, tk, cb), lambda b, h, i: (h, 0, 0, 0)),
            pl.BlockSpec((1, DV), lambda b, h, i: (0, 0)),
        ],
        out_specs=pl.BlockSpec((1, tq, DV), lambda b, h, i: (b, i, h)),
        out_shape=jax.ShapeDtypeStruct((bt, s_len, N_HEADS * DV), BF16),
        scratch_shapes=[
            pltpu.VMEM((2 * tq, 2 * DK), BF16),
            pltpu.VMEM((s_len // tk, DV + ONES_ROWS, tk), BF16),
        ] + [pltpu.VMEM((1, cb), F32)] * n_blk + [pltpu.VMEM((DV + ONES_ROWS, cb), F32)] * n_blk
        + [pltpu.VMEM((tk, cb), F32)] * n_blk,
        compiler_params=_params(("parallel", "parallel", "arbitrary")),
        name="diff_attention",
    )(lam.reshape(1, 1).astype(F32), z3, z3, z3, bias, subln_g.reshape(1, DV).astype(F32))


def _ssm_operators(lam_re, lam_im, log_dt, b_re, b_im, c_re, c_im, d_skip):
    L, H, P = SSM_CHUNK, SSM_GROUP, SSM_STATE
    lre = lam_re.astype(F32)
    lim = lam_im.astype(F32)
    dt = jnp.exp(log_dt.astype(F32))[:, None]
    mag = jnp.exp(lre * dt)
    ab_re = mag * jnp.cos(lim * dt)
    ab_im = mag * jnp.sin(lim * dt)
    den = lre * lre + lim * lim
    nr, ni = ab_re - 1.0, ab_im
    cr = ((nr * lre + ni * lim) / den)[..., None]
    ci = ((ni * lre - nr * lim) / den)[..., None]
    bre = b_re.astype(F32)
    bim = b_im.astype(F32)
    bb_re = cr * bre - ci * bim
    bb_im = cr * bim + ci * bre
    cre = c_re.astype(F32)
    cim = c_im.astype(F32)

    tau = jnp.arange(L + 1, dtype=F32)[:, None, None]
    pw_mag = jnp.exp(tau * (lre * dt)[None])
    pw_re = pw_mag * jnp.cos(tau * (lim * dt)[None])
    pw_im = pw_mag * jnp.sin(tau * (lim * dt)[None])

    ca_re = cre[None] * pw_re[:, :, None, :] - cim[None] * pw_im[:, :, None, :]
    ca_im = cre[None] * pw_im[:, :, None, :] + cim[None] * pw_re[:, :, None, :]
    bbt_re = bb_re.transpose(0, 2, 1)[None, :, None]
    bbt_im = bb_im.transpose(0, 2, 1)[None, :, None]
    k_tau = jnp.sum(ca_re[:L, :, :, None, :] * bbt_re - ca_im[:L, :, :, None, :] * bbt_im, axis=-1)
    ti = jnp.arange(L)
    t_src = jnp.tile(k_tau.transpose(0, 1, 3, 2).reshape(L, SSM_GROUPS * H, H), (1, 1, SSM_BUNDLE))

    rev_re = pw_re[L - 1 - ti]
    rev_im = pw_im[L - 1 - ti]
    w_re = rev_re[..., None] * bb_re[None] - rev_im[..., None] * bb_im[None]
    w_im = rev_re[..., None] * bb_im[None] + rev_im[..., None] * bb_re[None]
    w_src = jnp.concatenate([w_re, w_im], axis=2).transpose(0, 1, 3, 2)

    e_src = jnp.concatenate([ca_re[1:L + 1], -ca_im[1:L + 1]], axis=3)

    a1 = jnp.concatenate([pw_re[L], pw_re[L]], axis=-1)
    a2 = jnp.concatenate([-pw_im[L], pw_im[L]], axis=-1)
    d_tile = jnp.tile(d_skip.astype(F32).reshape(SSM_GROUPS // SSM_BUNDLE, 1, SSM_BUNDLE * H), (1, 1, L))
    return t_src.astype(BF16), w_src.astype(BF16), e_src.astype(BF16), a1, a2, d_tile


def _bundle_operator(dst, src_ref):
    n_l, n_g, h, x = src_ref.shape
    dst[...] = jnp.zeros(dst.shape, dst.dtype)
    for i in range(n_l):
        for g in range(n_g):
            r0 = (i * n_g + g) * h
            dst[r0:r0 + h, g * x:(g + 1) * x] = src_ref[i, g]


def _chunk_rows(u_ref, n_chunk):
    return jnp.concatenate([u_ref[pl.ds(t, n_chunk, stride=SSM_CHUNK), :] for t in range(SSM_CHUNK)], axis=1)


def _ssm_in_kernel(u_ref, w_ref, v_ref, w_scr):
    @pl.when(pl.program_id(1) == 0)
    def _():
        _bundle_operator(w_scr, w_ref)

    x = _chunk_rows(u_ref, v_ref.shape[0]).astype(BF16)
    v = jnp.dot(x, w_scr[...], preferred_element_type=F32)
    sw = v_ref.shape[2]
    for g in range(v_ref.shape[1]):
        v_ref[:, g, :] = v[:, g * sw:(g + 1) * sw]


def _ssm_scan_kernel(v_ref, a1_ref, a2_ref, o_ref, st):
    @pl.when(pl.program_id(0) == 0)
    def _():
        st[...] = jnp.zeros(st.shape, F32)

    a1 = a1_ref[...][None]
    a2 = a2_ref[...][None]
    n_chunk = v_ref.shape[1]
    half = v_ref.shape[3] // 2

    def body(c, s):
        o_ref[:, pl.ds(c, 1)] = s[:, None].astype(o_ref.dtype)
        v = v_ref[:, pl.ds(c, 1)][:, 0]
        return a1 * s + a2 * pltpu.roll(s, half, axis=2) + v

    st[...] = lax.fori_loop(0, n_chunk, body, st[...])


def _gelu_tanh(x):
    c = math.sqrt(2.0 / math.pi)
    return x * (0.5 * (1.0 + jnp.tanh(c * (x + 0.044715 * (x * x * x)))))


def _ssm_out_kernel(u_ref, t_ref, s_ref, e_ref, d_ref, y_ref, t_scr, et_scr):
    L, lw, h = SSM_CHUNK, t_ref.shape[1], SSM_GROUP

    @pl.when(pl.program_id(1) == 0)
    def _():
        _bundle_operator(et_scr, e_ref)
        same_group = (lax.broadcasted_iota(jnp.int32, (lw, lw), 0) // h
                      == lax.broadcasted_iota(jnp.int32, (lw, lw), 1) // h)
        t_scr[...] = jnp.zeros(t_scr.shape, t_scr.dtype)
        for tau in range(L):
            blk = jnp.where(same_group, t_ref[tau], jnp.zeros((lw, lw), t_ref.dtype))
            for i in range(L - tau):
                j = i + tau
                t_scr[i * lw:(i + 1) * lw, j * lw:(j + 1) * lw] = blk

    n_chunk = s_ref.shape[0]
    xf = _chunk_rows(u_ref, n_chunk)
    sp = jnp.concatenate([s_ref[:, g, :] for g in range(s_ref.shape[1])], axis=1).astype(BF16)
    xb = xf.astype(BF16)
    step = MXU_WIDTH
    y = jnp.concatenate([jnp.dot(xb[:, :c + step], t_scr[:c + step, c:c + step], preferred_element_type=F32)
                         for c in range(0, L * lw, step)], axis=1)
    y = y + lax.dot_general(sp, et_scr[...], (((1,), (1,)), ((), ())), preferred_element_type=F32)
    y = _gelu_tanh(y + d_ref[0] * xf)
    for t in range(L):
        y_ref[pl.ds(t, n_chunk, stride=L), :] = y[:, t * lw:(t + 1) * lw]


def _ssm(u, ops, bt, scan_block=32, chunk_tile=256):
    t_src, w_src, e_src, a1, a2, d_tile = ops
    n, width = u.shape
    L, G, H, P2, nbg = SSM_CHUNK, SSM_GROUPS, SSM_GROUP, 2 * SSM_STATE, SSM_BUNDLE
    nb = G // nbg
    lw = nbg * H
    nc = n // L
    n_c = nc // bt
    ct = min(chunk_tile, nc)
    u_spec = pl.BlockSpec((ct * L, lw), lambda b, i: (i, b))
    src_spec = pl.BlockSpec((L, nbg, H, P2), lambda b, i: (0, b, 0, 0))
    state_spec = pl.BlockSpec((ct, nbg, P2), lambda b, i: (i, b, 0))

    v = pl.pallas_call(
        _ssm_in_kernel,
        grid=(nb, nc // ct),
        in_specs=[u_spec, src_spec],
        out_specs=state_spec,
        out_shape=jax.ShapeDtypeStruct((nc, G, P2), F32),
        scratch_shapes=[pltpu.VMEM((L * lw, nbg * P2), BF16)],
        compiler_params=_params(("parallel", "arbitrary")),
        name="ssm_chunk_state",
    )(u, w_src)

    cb = min(scan_block, n_c)
    s_prev = pl.pallas_call(
        _ssm_scan_kernel,
        grid=(n_c // cb,),
        in_specs=[pl.BlockSpec((bt, cb, G, P2), lambda c: (0, c, 0, 0)),
                  pl.BlockSpec((G, P2), lambda c: (0, 0)),
                  pl.BlockSpec((G, P2), lambda c: (0, 0))],
        out_specs=pl.BlockSpec((bt, cb, G, P2), lambda c: (0, c, 0, 0)),
        out_shape=jax.ShapeDtypeStruct((bt, n_c, G, P2), F32),
        scratch_shapes=[pltpu.VMEM((bt, G, P2), F32)],
        compiler_params=_params(("arbitrary",)),
        name="ssm_scan",
    )(v.reshape(bt, n_c, G, P2), a1, a2)

    return pl.pallas_call(
        _ssm_out_kernel,
        grid=(nb, nc // ct),
        in_specs=[u_spec,
                  pl.BlockSpec((L, lw, lw), lambda b, i: (0, b, 0)),
                  state_spec,
                  src_spec,
                  pl.BlockSpec((1, 1, L * lw), lambda b, i: (b, 0, 0))],
        out_specs=pl.BlockSpec((ct * L, lw), lambda b, i: (i, b)),
        out_shape=jax.ShapeDtypeStruct((n, width), F32),
        scratch_shapes=[pltpu.VMEM((L * lw, L * lw), BF16), pltpu.VMEM((L * lw, nbg * P2), BF16)],
        compiler_params=_params(("parallel", "arbitrary")),
        name="ssm_output",
    )(u, t_src, s_prev.reshape(nc, G, P2), e_src, d_tile)


def _postmix_kernel(a_ref, s_ref, x_ref, wg_ref, bg_ref, gs_ref, woa_ref, wos_ref, gf_ref,
                    wrh_ref, wrl_ref, br_ref, x1_ref, h_ref, lg_ref):
    sf = s_ref[...]
    gate = _sigmoid(jnp.dot(sf.astype(BF16), wg_ref[...], preferred_element_type=F32) + bg_ref[...])
    sn = _rms(sf * gate, gs_ref[...], 1e-6).astype(BF16)
    x1 = (x_ref[...]
          + jnp.dot(a_ref[...], woa_ref[...], preferred_element_type=F32)
          + jnp.dot(sn, wos_ref[...], preferred_element_type=F32))
    x1_ref[...] = x1
    h = _rms(x1, gf_ref[...], 1e-6)
    h_hi = h.astype(BF16)
    h_lo = (h - h_hi.astype(F32)).astype(BF16)
    _store_row_slabs(h_ref, 0, h_hi)
    lg_ref[...] = (jnp.dot(h_hi, wrh_ref[...], preferred_element_type=F32)
                   + jnp.dot(h_lo, wrh_ref[...], preferred_element_type=F32)
                   + jnp.dot(h_hi, wrl_ref[...], preferred_element_type=F32)
                   + br_ref[...])


def _postmix(a, s, x2d, w_glu, b_glu, g_s, w_o, g_ffn, w_r, b_r, tm=512):
    n, d = x2d.shape
    wa = a.shape[1]
    ws = s.shape[1]
    wr_hi = w_r.astype(BF16)
    wr_lo = (w_r - wr_hi.astype(F32)).astype(BF16)
    row = lambda i: (i, 0)
    fixed = lambda i: (0, 0)
    return pl.pallas_call(
        _postmix_kernel,
        grid=(n // tm,),
        in_specs=[
            pl.BlockSpec((tm, wa), row),
            pl.BlockSpec((tm, ws), row),
            pl.BlockSpec((tm, d), row),
            pl.BlockSpec((ws, ws), fixed, pipeline_mode=pl.Buffered(1)),
            pl.BlockSpec((1, ws), fixed),
            pl.BlockSpec((1, ws), fixed),
            pl.BlockSpec((wa, d), fixed, pipeline_mode=pl.Buffered(1)),
            pl.BlockSpec((ws, d), fixed, pipeline_mode=pl.Buffered(1)),
            pl.BlockSpec((1, d), fixed),
            pl.BlockSpec((d, ROUTER_LANES), fixed, pipeline_mode=pl.Buffered(1)),
            pl.BlockSpec((d, ROUTER_LANES), fixed, pipeline_mode=pl.Buffered(1)),
            pl.BlockSpec((1, ROUTER_LANES), fixed),
        ],
        out_specs=[pl.BlockSpec((tm, d), row), pl.BlockSpec((tm * (d // SLAB_COLS), LANES), row),
                   pl.BlockSpec((tm, ROUTER_LANES), row)],
        out_shape=[jax.ShapeDtypeStruct((n, d), F32), jax.ShapeDtypeStruct((n * (d // SLAB_COLS), LANES), jnp.uint32),
                   jax.ShapeDtypeStruct((n, ROUTER_LANES), F32)],
        compiler_params=_params(("parallel",)),
        name="postmix",
    )(a, s, x2d, w_glu.astype(BF16), b_glu.reshape(1, ws).astype(F32), g_s.reshape(1, ws).astype(F32),
      w_o[:wa].astype(BF16), w_o[wa:].astype(BF16), g_ffn.reshape(1, d).astype(F32),
      wr_hi, wr_lo, b_r.reshape(1, ROUTER_LANES).astype(F32))


def _route(logits, rows, slab_rows):
    n_tok = logits.shape[0]
    lg = logits[:, :N_EXPERT_GROUPS]
    le = logits[:, N_EXPERT_GROUPS:N_EXPERT_GROUPS + N_EXPERTS].reshape(
        n_tok, N_EXPERT_GROUPS, EXPERTS_PER_GROUP)
    pg = jax.nn.softmax(lg, axis=-1)
    gsel = jnp.argmax(lg, axis=-1).astype(jnp.int32)
    gate_g = jnp.max(pg, axis=-1, keepdims=True)
    sel = gsel[:, None] == jnp.arange(N_EXPERT_GROUPS, dtype=jnp.int32)[None, :]
    le_sel = jnp.sum(jnp.where(sel[:, :, None], le, 0.0), axis=1)
    pe = jax.nn.softmax(le_sel, axis=-1)
    top_p, top_i = lax.top_k(pe, TOP_K)
    w = gate_g * top_p / jnp.sum(top_p, axis=-1, keepdims=True)
    eid = gsel[:, None] * EXPERTS_PER_GROUP + top_i.astype(jnp.int32)

    n_assign = n_tok * TOP_K
    flat_e = eid.reshape(-1)
    counts = jnp.sum((flat_e[None, :] == jnp.arange(N_EXPERTS, dtype=jnp.int32)[:, None])
                     .astype(jnp.int32), axis=1)
    n_blk_e = (counts + rows - 1) // rows
    blk_end = jnp.cumsum(n_blk_e)
    blk_first = blk_end - n_blk_e
    start = jnp.cumsum(counts) - counts
    order = jnp.argsort(flat_e, stable=True).astype(jnp.int32)
    n_blk = (n_assign + rows - 1) // rows + N_EXPERTS
    blk = jnp.arange(n_blk, dtype=jnp.int32)
    blk_e = jnp.minimum(jnp.sum((blk[:, None] >= blk_end[None, :]).astype(jnp.int32), axis=1),
                        N_EXPERTS - 1)
    used = blk < blk_end[-1]
    experts = jnp.arange(N_EXPERTS, dtype=jnp.int32)
    is_e = blk_e[:, None] == experts[None, :]

    def per_block(table):
        return jnp.sum(jnp.where(is_e, table[None, :], 0), axis=1)

    in_e = (blk - per_block(blk_first)) * rows
    blk_cnt = jnp.where(used, jnp.clip(per_block(counts) - in_e, 0, rows), 0).astype(jnp.int32)
    blk_src = jnp.where(used, per_block(start) + in_e, 0).astype(jnp.int32)
    last_e = jnp.max(jnp.where(counts > 0, experts, 0))
    blk_e = jnp.where(used, blk_e, last_e)
    row_tok = order // TOP_K
    row_dst = (order % TOP_K) * n_tok + row_tok
    pad = jnp.zeros((rows,), jnp.int32)
    return (w, jnp.concatenate([row_tok * slab_rows, pad]), jnp.concatenate([row_dst * slab_rows, pad]),
            blk_e, blk_cnt, blk_src)


def _expert_kernel(blk_e_ref, blk_cnt_ref, blk_src_ref, tok_ref, dst_ref,
                   h_hbm, w1_ref, w3_ref, w2_ref, y_hbm, xbuf, ybuf, gsem, ssem):
    del blk_e_ref
    b = pl.program_id(0)
    nb = pl.num_programs(0)
    slot = b % N_SLOTS
    slot_m1 = (b + N_SLOTS - 1) % N_SLOTS
    slot_p1 = (b + 1) % N_SLOTS
    spr = w1_ref.shape[1] // SLAB_COLS
    rows = xbuf.shape[1] // spr
    dump = y_hbm.shape[0] - N_SLOTS * rows * spr
    cnt = blk_cnt_ref[b]
    prev = jnp.maximum(b - 1, 0)
    prev_cnt = jnp.where(b >= 1, blk_cnt_ref[prev], 0)

    def gather_copy(src, r, sl):
        return pltpu.make_async_copy(h_hbm.at[pl.ds(pl.multiple_of(tok_ref[src + r], spr), spr)],
                                     xbuf.at[sl, pl.ds(r * spr, spr)], gsem.at[sl])

    def scatter_copy(src, n_valid, r, sl):
        dst = jnp.where(r < n_valid, dst_ref[src + r], dump + (sl * rows + r) * spr)
        return pltpu.make_async_copy(ybuf.at[sl, pl.ds(r * spr, spr)],
                                     y_hbm.at[pl.ds(pl.multiple_of(dst, spr), spr)], ssem.at[sl])

    def start_gathers(blk, sl, lo=0, hi=None):
        src = blk_src_ref[blk]
        for r in range(lo, rows if hi is None else hi):
            gather_copy(src, r, sl).start()

    def wait_gathers(sl):
        for r in range(rows):
            gather_copy(0, r, sl).wait()

    def start_scatters(blk, n_valid, sl, lo=0, hi=None):
        src = blk_src_ref[blk]
        for r in range(lo, rows if hi is None else hi):
            scatter_copy(src, n_valid, r, sl).start()

    def wait_scatters(sl):
        for r in range(rows):
            scatter_copy(0, 0, r, sl).wait()

    @pl.when(b == 0)
    def _():
        ybuf[...] = jnp.zeros(ybuf.shape, ybuf.dtype)
        start_gathers(0, 0)
        start_gathers(1, 1)
        for sl in range(N_SLOTS - 1):
            start_scatters(0, 0, sl)
        for sl in range(N_SLOTS - 1):
            wait_scatters(sl)

    @pl.when(cnt > 0)
    def _():
        @pl.when(b >= 2)
        def _():
            wait_scatters(slot)

        wait_gathers(slot)
        x = _load_row_slabs(xbuf.at[slot], 0, rows, spr).astype(BF16)
        f = w1_ref.shape[2]
        n_f = f // MXU_WIDTH
        n_part = 3 * n_f
        bounds = [rows * i // n_part for i in range(n_part + 1)]
        part = iter(range(n_part))

        def issue_part():
            i = next(part)
            start_gathers(jnp.minimum(b + 2, nb - 1), slot_m1, bounds[i], bounds[i + 1])
            start_scatters(prev, prev_cnt, slot_m1, bounds[i], bounds[i + 1])

        acts = []
        for c in range(n_f):
            cols = slice(c * MXU_WIDTH, (c + 1) * MXU_WIDTH)
            issue_part()
            h1 = jnp.dot(x, w1_ref[0, :, cols].astype(BF16), preferred_element_type=F32)
            issue_part()
            h3 = jnp.dot(x, w3_ref[0, :, cols].astype(BF16), preferred_element_type=F32)
            acts.append((h1 * _sigmoid(h1) * h3).astype(BF16))
        y = None
        for c in range(n_f):
            issue_part()
            yc = jnp.dot(acts[c], w2_ref[0, c * MXU_WIDTH:(c + 1) * MXU_WIDTH, :].astype(BF16),
                         preferred_element_type=F32)
            y = yc if y is None else y + yc
        _store_row_slabs(ybuf.at[slot], 0, y)

    @pl.when((cnt == 0) & (prev_cnt > 0))
    def _():
        wait_gathers(slot)
        wait_gathers(slot_p1)

        @pl.when(b >= 2)
        def _():
            wait_scatters(slot)

        wait_scatters(slot_p1)
        start_scatters(prev, prev_cnt, slot_m1)
        wait_scatters(slot_m1)


def _experts(h, w1, w3, w2, row_tok, row_dst, blk_e, blk_cnt, blk_src, rows):
    d, f = w1.shape[1], w1.shape[2]
    spr = d // SLAB_COLS
    n_blk = blk_e.shape[0]
    wmap = lambda b, be, bc, bs, rt, rd: (be[b], 0, 0)
    grid_spec = pltpu.PrefetchScalarGridSpec(
        num_scalar_prefetch=5,
        grid=(n_blk,),
        in_specs=[
            pl.BlockSpec(memory_space=pl.ANY),
            pl.BlockSpec((1, d, f), wmap),
            pl.BlockSpec((1, d, f), wmap),
            pl.BlockSpec((1, f, d), wmap),
        ],
        out_specs=pl.BlockSpec(memory_space=pl.ANY),
        scratch_shapes=[
            pltpu.VMEM((N_SLOTS, rows * spr, LANES), jnp.uint32),
            pltpu.VMEM((N_SLOTS, rows * spr, LANES), jnp.uint32),
            pltpu.SemaphoreType.DMA((N_SLOTS,)),
            pltpu.SemaphoreType.DMA((N_SLOTS,)),
        ],
    )
    return pl.pallas_call(
        _expert_kernel,
        grid_spec=grid_spec,
        out_shape=jax.ShapeDtypeStruct((h.shape[0] * TOP_K + N_SLOTS * rows * spr, LANES), jnp.uint32),
        compiler_params=_params(("arbitrary",), disable_bounds_checks=True),
        name="experts",
    )(blk_e, blk_cnt, blk_src, row_tok, row_dst, h, w1, w3, w2)


def _ple_kernel(x1_ref, y0_ref, y1_ref, w_ref, p_ref, gp_ref, wg_ref, wp_ref, gf_ref, o_ref, *, final):
    w = w_ref[...]
    tm, d = x1_ref.shape
    spr = d // SLAB_COLS
    y0 = _load_row_slabs(y0_ref, 0, tm, spr)
    y1 = _load_row_slabs(y1_ref, 0, tm, spr)
    x2 = x1_ref[...] + w[:, 0:1] * y0 + w[:, 1:2] * y1
    hn = _rms(x2, gp_ref[...], 1e-6).astype(BF16)
    gate = _sigmoid(jnp.dot(hn, wg_ref[...], preferred_element_type=F32))
    pp = jnp.dot(p_ref[...].astype(BF16), wp_ref[...], preferred_element_type=F32)
    x3 = x2 + gate * pp
    o_ref[...] = _rms(x3, gf_ref[...], 1e-6) if final else x3


def _ple(x1, y2, w, p2d, g_ple, w_gate, w_proj, g_final, final, tm=512):
    n, d = x1.shape
    pd = p2d.shape[1]
    row = lambda i: (i, 0)
    fixed = lambda i: (0, 0)
    return pl.pallas_call(
        functools.partial(_ple_kernel, final=final),
        grid=(n // tm,),
        in_specs=[
            pl.BlockSpec((tm, d), row),
            pl.BlockSpec((tm * (d // SLAB_COLS), LANES), row),
            pl.BlockSpec((tm * (d // SLAB_COLS), LANES), lambda i: (n // tm + i, 0)),
            pl.BlockSpec((tm, TOP_K), row),
            pl.BlockSpec((tm, pd), row),
            pl.BlockSpec((1, d), fixed),
            pl.BlockSpec((d, d), fixed, pipeline_mode=pl.Buffered(1)),
            pl.BlockSpec((pd, d), fixed, pipeline_mode=pl.Buffered(1)),
            pl.BlockSpec((1, d), fixed),
        ],
        out_specs=pl.BlockSpec((tm, d), row),
        out_shape=jax.ShapeDtypeStruct((n, d), F32),
        compiler_params=_params(("parallel",)),
        name="ple_final",
    )(x1, y2, y2, w, p2d, g_ple.reshape(1, d).astype(F32), w_gate.astype(BF16), w_proj.astype(BF16),
      g_final.reshape(1, d).astype(F32))


def kernel(x, p, rel_bias, g_mix, w_in, lam_q1, lam_k1, lam_q2, lam_k2, subln_g, ssm_lam_re, ssm_lam_im, ssm_log_dt, ssm_b_re, ssm_b_im, ssm_c_re, ssm_c_im, ssm_d, w_glu, b_glu, ssm_norm_g, w_o, g_ffn, w_router_g, b_router_g, w_router_e, b_router_e, w1, w3, w2, g_ple, w_ple_gate, w_ple_proj, g_final):
    bt, s_len, d = x.shape
    n = bt * s_len
    depth = g_mix.shape[0]
    attn_w = N_HEADS * DV
    xc = x.reshape(n, d).astype(F32)
    for i in range(depth):
        lam_init = 0.8 - 0.6 * math.exp(-0.3 * i)
        col_scale = jnp.concatenate([jnp.full((attn_w,), LOG2E * DK ** -0.5, F32),
                                     jnp.ones((w_in.shape[2] - attn_w,), F32)])
        w_in_b = (w_in[i].astype(F32) * col_scale[None, :]).astype(BF16)
        z, u = _inproj(xc, g_mix[i].astype(F32), w_in_b)
        z3 = z.reshape(bt, s_len, z.shape[1])

        lam = (jnp.exp(jnp.sum(lam_q1[i].astype(F32) * lam_k1[i].astype(F32)))
               - jnp.exp(jnp.sum(lam_q2[i].astype(F32) * lam_k2[i].astype(F32))) + lam_init)
        a = _attention(z3, rel_bias, lam, subln_g[i], lam_init)

        ops = _ssm_operators(ssm_lam_re[i], ssm_lam_im[i], ssm_log_dt[i], ssm_b_re[i], ssm_b_im[i],
                             ssm_c_re[i], ssm_c_im[i], ssm_d[i])
        s = _ssm(u, ops, bt)

        w_r = jnp.zeros((d, ROUTER_LANES), F32)
        w_r = w_r.at[:, :N_EXPERT_GROUPS].set(w_router_g[i].astype(F32))
        w_r = w_r.at[:, N_EXPERT_GROUPS:N_EXPERT_GROUPS + N_EXPERTS].set(w_router_e[i].astype(F32))
        b_r = jnp.zeros((ROUTER_LANES,), F32)
        b_r = b_r.at[:N_EXPERT_GROUPS].set(b_router_g[i].astype(F32))
        b_r = b_r.at[N_EXPERT_GROUPS:N_EXPERT_GROUPS + N_EXPERTS].set(b_router_e[i].astype(F32))
        x1, h, logits = _postmix(a.reshape(n, attn_w), s, xc, w_glu[i], b_glu[i],
                                 ssm_norm_g[i], w_o[i], g_ffn[i], w_r, b_r)

        gate_w, row_tok, row_dst, blk_e, blk_cnt, blk_src = _route(logits, EXPERT_ROWS, d // SLAB_COLS)
        y2 = _experts(h, w1[i], w3[i], w2[i],
                      row_tok, row_dst, blk_e, blk_cnt, blk_src, EXPERT_ROWS)
        xc = _ple(x1, y2, gate_w.astype(F32), p[i].reshape(n, -1).astype(F32),
                  g_ple[i], w_ple_gate[i], w_ple_proj[i], g_final, final=(i == depth - 1))
    return xc.reshape(bt, s_len, d)
```

```python
import functools
import math

import jax
import jax.numpy as jnp
from jax import lax
from jax.experimental import pallas as pl
from jax.experimental.pallas import tpu as pltpu

F32 = jnp.float32
BF16 = jnp.bfloat16
HIGHEST = lax.Precision.HIGHEST

N_HEADS = 8
DK = 64
DV = 128
N_BUCKETS = 32
MAX_DISTANCE = 128
SSM_GROUP = 16
SSM_GROUPS = 64
SSM_STATE = 64
N_EXPERT_GROUPS = 4
EXPERTS_PER_GROUP = 8
N_EXPERTS = 32
TOP_K = 2
MASK_VALUE = -1e30
LOG2E = math.log2(math.e)

SSM_CHUNK = 16
SSM_BUNDLE = 8
ATTN_TQ = 2048
ONES_ROWS = 16
ATTN_CB = 256
ATTN_TK = 256
EXPERT_ROWS = 256
N_SLOTS = 3
MXU_WIDTH = 256
LANES = 128
SLAB_COLS = 2 * LANES
ROUTER_LANES = 128
VMEM_LIMIT = 56 << 20


def _params(semantics, **kw):
    return pltpu.CompilerParams(dimension_semantics=semantics, vmem_limit_bytes=VMEM_LIMIT, **kw)


def _rms(x, g, eps):
    return x * lax.rsqrt(jnp.mean(x * x, axis=-1, keepdims=True) + eps) * g


def _sigmoid(x):
    return 1.0 / (1.0 + jnp.exp(-x))


def _store_row_slabs(ref, row0, x):
    n, d = x.shape
    spr = d // (2 * LANES)
    bits = lax.bitcast_convert_type(x.astype(BF16).astype(F32), jnp.uint32)
    for j in range(spr):
        lo = lax.shift_right_logical(bits[:, j * LANES:(j + 1) * LANES], jnp.uint32(16))
        hi = bits[:, d // 2 + j * LANES:d // 2 + (j + 1) * LANES] & jnp.uint32(0xFFFF0000)
        ref[pl.ds(row0 * spr + j, n, stride=spr), :] = lo | hi


def _load_row_slabs(ref, row0, n, spr):
    words = [ref[pl.ds(row0 * spr + j, n, stride=spr), :] for j in range(spr)]
    lo = [lax.bitcast_convert_type(lax.shift_left(w, jnp.uint32(16)), F32) for w in words]
    hi = [lax.bitcast_convert_type(w & jnp.uint32(0xFFFF0000), F32) for w in words]
    return jnp.concatenate(lo + hi, axis=1)


def _inproj_kernel(x_ref, g_ref, w_ref, z_ref, u_ref, h_scr):
    j = pl.program_id(1)
    last = pl.num_programs(1) - 1

    @pl.when(j == 0)
    def _():
        h_scr[...] = _rms(x_ref[...], g_ref[...], 1e-6).astype(BF16)

    acc = jnp.dot(h_scr[...], w_ref[...], preferred_element_type=F32)

    @pl.when(j < last)
    def _():
        z_ref[...] = acc.astype(z_ref.dtype)

    @pl.when(j == last)
    def _():
        u_ref[...] = acc


def _inproj(x2d, g, w, tm=1024, tn=1024):
    n, d = x2d.shape
    pw = w.shape[1]
    n_col = pw // tn
    return pl.pallas_call(
        _inproj_kernel,
        grid=(n // tm, n_col),
        in_specs=[
            pl.BlockSpec((tm, d), lambda i, j: (i, 0)),
            pl.BlockSpec((1, d), lambda i, j: (0, 0)),
            pl.BlockSpec((d, tn), lambda i, j: (0, j)),
        ],
        out_specs=[pl.BlockSpec((tm, tn), lambda i, j: (i, jnp.minimum(j, n_col - 2))),
                   pl.BlockSpec((tm, tn), lambda i, j: (i, 0))],
        out_shape=[jax.ShapeDtypeStruct((n, pw - tn), BF16), jax.ShapeDtypeStruct((n, tn), F32)],
        scratch_shapes=[pltpu.VMEM((tm, d), BF16)],
        compiler_params=_params(("parallel", "arbitrary")),
        name="inproj",
    )(x2d, g.reshape(1, d), w)


def _attn_kernel(lam_ref, q_ref, k_ref, v_ref, bias_ref, g_ref, o_ref, qs, vt, *state,
                 tq, tk, cb, out_scale):
    n_blk = 2 * tq // cb
    m_scr, acc, s_scr = (state[i * n_blk:(i + 1) * n_blk] for i in range(3))
    qi = pl.program_id(2)
    n_sub = tq // tk
    n_kv = v_ref.shape[1] // tk

    @pl.when(qi == 0)
    def _():
        for c in range(n_kv):
            vt[c, 0:DV, :] = v_ref[0, c * tk:(c + 1) * tk, :].astype(F32).T.astype(BF16)
            vt[c, DV:DV + ONES_ROWS, :] = jnp.ones((ONES_ROWS, tk), BF16)

    q = q_ref[0]
    lane = lax.broadcasted_iota(jnp.int32, q.shape, 1)
    zero = jnp.zeros_like(q)
    qs[0:tq, :] = jnp.where(lane < DK, q, zero)
    qs[tq:2 * tq, :] = jnp.where(lane >= DK, q, zero)
    for c in range(n_blk):
        m_scr[c][...] = jnp.full(m_scr[c].shape, MASK_VALUE, F32)
        acc[c][...] = jnp.zeros(acc[c].shape, F32)

    all_blocks = tuple(range(n_blk))

    def visible(r):
        return tuple(c for c in all_blocks if (c * cb) % tq + cb > (r - 1) * tk)

    def scores(j, blocks):
        kj = k_ref[0, pl.ds(pl.multiple_of(j * tk, tk), tk), :]
        return [lax.dot_general(kj, qs[c * cb:(c + 1) * cb, :], (((1,), (1,)), ((), ())),
                                preferred_element_type=F32) for c in blocks]

    def consume(j, near, blocks, next_blocks):
        nxt = scores(j + 1, next_blocks) if next_blocks else None
        vtj = vt[j]
        alphas, ps = [], []
        for c in blocks:
            s = s_scr[c][...]
            if near is not None:
                ahead = (c * cb) % tq - (near - 1) * tk
                if ahead in (0, tk):
                    s = s + bias_ref[0, ahead // tk]
            m_old = m_scr[c][...]
            m_new = jnp.maximum(m_old, jnp.max(s, axis=0, keepdims=True))
            alpha = jnp.exp2(m_old - m_new)
            p = jnp.exp2(s - m_new)
            m_scr[c][...] = m_new
            alphas.append(alpha)
            ps.append(p.astype(BF16))
        pvs = [jnp.dot(vtj, p, preferred_element_type=F32) for p in ps]
        for c, alpha, pv in zip(blocks, alphas, pvs):
            acc[c][...] = alpha * acc[c][...] + pv
        if next_blocks:
            for c, sc in zip(next_blocks, nxt):
                s_scr[c][...] = sc

    first = qi * n_sub - 1
    for c, sc in zip(all_blocks, scores(0, all_blocks)):
        s_scr[c][...] = sc

    def far_pair(i, carry):
        consume(2 * i, None, all_blocks, all_blocks)
        consume(2 * i + 1, None, all_blocks, all_blocks)
        return carry

    lax.fori_loop(0, jnp.maximum(first, 0) // 2, far_pair, 0)

    @pl.when(qi >= 1)
    def _():
        consume(first - 1, None, all_blocks, all_blocks)
        consume(first, 0, all_blocks, all_blocks)

    for r in range(1, n_sub + 1):
        consume(first + r, r, visible(r), visible(r + 1) if r < n_sub else None)

    lam = lam_ref[0, 0]
    half = n_blk // 2
    for b in range(half):
        a1, a2 = acc[b], acc[half + b]
        ot = (a1[0:DV, :] / a1[DV:DV + 1, :] - lam * (a2[0:DV, :] / a2[DV:DV + 1, :]))
        ot = ot * lax.rsqrt(jnp.mean(ot * ot, axis=0, keepdims=True) + 1e-5)
        o_ref[0, b * cb:(b + 1) * cb, :] = (ot.T * (g_ref[...] * out_scale)).astype(o_ref.dtype)


def _t5_bucket(n):
    n = jnp.maximum(n, 0)
    max_exact = N_BUCKETS // 2
    nf = jnp.maximum(n, 1).astype(F32)
    large = max_exact + (jnp.log(nf / max_exact) / math.log(MAX_DISTANCE / max_exact)
                         * (N_BUCKETS - max_exact)).astype(jnp.int32)
    large = jnp.minimum(large, N_BUCKETS - 1)
    return jnp.where(n < max_exact, n, large)


def _attn_bias_tiles(rel_bias, tk):
    assert tk >= MAX_DISTANCE
    table = rel_bias.astype(F32)
    rel_table = (table - table[N_BUCKETS - 1][None, :]) * LOG2E
    r = jnp.arange(tk, dtype=jnp.int32)[:, None]
    c = jnp.arange(tk, dtype=jnp.int32)[None, :]
    tiles = []
    for ahead in (0, tk):
        dist = c + ahead - r
        onehot = (_t5_bucket(dist)[:, :, None] == jnp.arange(N_BUCKETS, dtype=jnp.int32)).astype(F32)
        b = jnp.einsum('rcn,nh->hrc', onehot, rel_table, precision=HIGHEST)
        tiles.append(jnp.where((dist >= 0)[None], b, MASK_VALUE))
    return jnp.stack(tiles, axis=1)


def _attention(z3, rel_bias, lam, subln_g, lam_init, tq=ATTN_TQ, tk=ATTN_TK, cb=ATTN_CB):
    bt, s_len, _ = z3.shape
    assert cb == tk and tq % (2 * tk) == 0
    bias = _attn_bias_tiles(rel_bias, tk)
    n_sp = bias.shape[1]
    kern = functools.partial(_attn_kernel, tq=tq, tk=tk, cb=cb, out_scale=1.0 - lam_init)
    n_blk = 2 * tq // cb
    return pl.pallas_call(
        kern,
        grid=(bt, N_HEADS, s_len // tq),
        in_specs=[
            pl.BlockSpec(memory_space=pltpu.SMEM),
            pl.BlockSpec((1, tq, 2 * DK), lambda b, h, i: (b, i, h)),
            pl.BlockSpec((1, s_len, 2 * DK), lambda b, h, i: (b, 0, N_HEADS + h)),
            pl.BlockSpec((1, s_len, DV), lambda b, h, i: (b, 0, 2 * N_HEADS + h)),
            pl.BlockSpec((1, n_sp, tk, cb), lambda b, h, i: (h, 0, 0, 0)),
            pl.BlockSpec((1, DV), lambda b, h, i: (0, 0)),
        ],
        out_specs=pl.BlockSpec((1, tq, DV), lambda b, h, i: (b, i, h)),
        out_shape=jax.ShapeDtypeStruct((bt, s_len, N_HEADS * DV), BF16),
        scratch_shapes=[
            pltpu.VMEM((2 * tq, 2 * DK), BF16),
            pltpu.VMEM((s_len // tk, DV + ONES_ROWS, tk), BF16),
        ] + [pltpu.VMEM((1, cb), F32)] * n_blk + [pltpu.VMEM((DV + ONES_ROWS, cb), F32)] * n_blk
        + [pltpu.VMEM((tk, cb), F32)] * n_blk,
        compiler_params=_params(("parallel", "parallel", "arbitrary")),
        name="diff_attention",
    )(lam.reshape(1, 1).astype(F32), z3, z3, z3, bias, subln_g.reshape(1, DV).astype(F32))


def _ssm_operators(lam_re, lam_im, log_dt, b_re, b_im, c_re, c_im, d_skip):
    L, H, P = SSM_CHUNK, SSM_GROUP, SSM_STATE
    lre = lam_re.astype(F32)
    lim = lam_im.astype(F32)
    dt = jnp.exp(log_dt.astype(F32))[:, None]
    mag = jnp.exp(lre * dt)
    ab_re = mag * jnp.cos(lim * dt)
    ab_im = mag * jnp.sin(lim * dt)
    den = lre * lre + lim * lim
    nr, ni = ab_re - 1.0, ab_im
    cr = ((nr * lre + ni * lim) / den)[..., None]
    ci = ((ni * lre - nr * lim) / den)[..., None]
    bre = b_re.astype(F32)
    bim = b_im.astype(F32)
    bb_re = cr * bre - ci * bim
    bb_im = cr * bim + ci * bre
    cre = c_re.astype(F32)
    cim = c_im.astype(F32)

    tau = jnp.arange(L + 1, dtype=F32)[:, None, None]
    pw_mag = jnp.exp(tau * (lre * dt)[None])
    pw_re = pw_mag * jnp.cos(tau * (lim * dt)[None])
    pw_im = pw_mag * jnp.sin(tau * (lim * dt)[None])

    ca_re = cre[None] * pw_re[:, :, None, :] - cim[None] * pw_im[:, :, None, :]
    ca_im = cre[None] * pw_im[:, :, None, :] + cim[None] * pw_re[:, :, None, :]
    bbt_re = bb_re.transpose(0, 2, 1)[None, :, None]
    bbt_im = bb_im.transpose(0, 2, 1)[None, :, None]
    k_tau = jnp.sum(ca_re[:L, :, :, None, :] * bbt_re - ca_im[:L, :, :, None, :] * bbt_im, axis=-1)
    ti = jnp.arange(L)
    t_src = jnp.tile(k_tau.transpose(0, 1, 3, 2).reshape(L, SSM_GROUPS * H, H), (1, 1, SSM_BUNDLE))

    rev_re = pw_re[L - 1 - ti]
    rev_im = pw_im[L - 1 - ti]
    w_re = rev_re[..., None] * bb_re[None] - rev_im[..., None] * bb_im[None]
    w_im = rev_re[..., None] * bb_im[None] + rev_im[..., None] * bb_re[None]
    w_src = jnp.concatenate([w_re, w_im], axis=2).transpose(0, 1, 3, 2)

    e_src = jnp.concatenate([ca_re[1:L + 1], -ca_im[1:L + 1]], axis=3)

    a1 = jnp.concatenate([pw_re[L], pw_re[L]], axis=-1)
    a2 = jnp.concatenate([-pw_im[L], pw_im[L]], axis=-1)
    d_tile = jnp.tile(d_skip.astype(F32).reshape(SSM_GROUPS // SSM_BUNDLE, 1, SSM_BUNDLE * H), (1, 1, L))
    return t_src.astype(BF16), w_src.astype(BF16), e_src.astype(BF16), a1, a2, d_tile


def _bundle_operator(dst, src_ref):
    n_l, n_g, h, x = src_ref.shape
    dst[...] = jnp.zeros(dst.shape, dst.dtype)
    for i in range(n_l):
        for g in range(n_g):
            r0 = (i * n_g + g) * h
            dst[r0:r0 + h, g * x:(g + 1) * x] = src_ref[i, g]


def _chunk_rows(u_ref, n_chunk):
    return jnp.concatenate([u_ref[pl.ds(t, n_chunk, stride=SSM_CHUNK), :] for t in range(SSM_CHUNK)], axis=1)


def _ssm_in_kernel(u_ref, w_ref, v_ref, w_scr):
    @pl.when(pl.program_id(1) == 0)
    def _():
        _bundle_operator(w_scr, w_ref)

    x = _chunk_rows(u_ref, v_ref.shape[0]).astype(BF16)
    v = jnp.dot(x, w_scr[...], preferred_element_type=F32)
    sw = v_ref.shape[2]
    for g in range(v_ref.shape[1]):
        v_ref[:, g, :] = v[:, g * sw:(g + 1) * sw]


def _ssm_scan_kernel(v_ref, a1_ref, a2_ref, o_ref, st):
    @pl.when(pl.program_id(0) == 0)
    def _():
        st[...] = jnp.zeros(st.shape, F32)

    a1 = a1_ref[...][None]
    a2 = a2_ref[...][None]
    n_chunk = v_ref.shape[1]
    half = v_ref.shape[3] // 2

    def body(c, s):
        o_ref[:, pl.ds(c, 1)] = s[:, None].astype(o_ref.dtype)
        v = v_ref[:, pl.ds(c, 1)][:, 0]
        return a1 * s + a2 * pltpu.roll(s, half, axis=2) + v

    st[...] = lax.fori_loop(0, n_chunk, body, st[...])


def _gelu_tanh(x):
    c = math.sqrt(2.0 / math.pi)
    return x * (0.5 * (1.0 + jnp.tanh(c * (x + 0.044715 * (x * x * x)))))


def _ssm_out_kernel(u_ref, t_ref, s_ref, e_ref, d_ref, y_ref, t_scr, et_scr):
    L, lw, h = SSM_CHUNK, t_ref.shape[1], SSM_GROUP

    @pl.when(pl.program_id(1) == 0)
    def _():
        _bundle_operator(et_scr, e_ref)
        same_group = (lax.broadcasted_iota(jnp.int32, (lw, lw), 0) // h
                      == lax.broadcasted_iota(jnp.int32, (lw, lw), 1) // h)
        t_scr[...] = jnp.zeros(t_scr.shape, t_scr.dtype)
        for tau in range(L):
            blk = jnp.where(same_group, t_ref[tau], jnp.zeros((lw, lw), t_ref.dtype))
            for i in range(L - tau):
                j = i + tau
                t_scr[i * lw:(i + 1) * lw, j * lw:(j + 1) * lw] = blk

    n_chunk = s_ref.shape[0]
    xf = _chunk_rows(u_ref, n_chunk)
    sp = jnp.concatenate([s_ref[:, g, :] for g in range(s_ref.shape[1])], axis=1).astype(BF16)
    xb = xf.astype(BF16)
    step = MXU_WIDTH
    y = jnp.concatenate([jnp.dot(xb[:, :c + step], t_scr[:c + step, c:c + step], preferred_element_type=F32)
                         for c in range(0, L * lw, step)], axis=1)
    y = y + lax.dot_general(sp, et_scr[...], (((1,), (1,)), ((), ())), preferred_element_type=F32)
    y = _gelu_tanh(y + d_ref[0] * xf)
    for t in range(L):
        y_ref[pl.ds(t, n_chunk, stride=L), :] = y[:, t * lw:(t + 1) * lw]


def _ssm(u, ops, bt, scan_block=32, chunk_tile=512):
    t_src, w_src, e_src, a1, a2, d_tile = ops
    n, width = u.shape
    L, G, H, P2, nbg = SSM_CHUNK, SSM_GROUPS, SSM_GROUP, 2 * SSM_STATE, SSM_BUNDLE
    nb = G // nbg
    lw = nbg * H
    nc = n // L
    n_c = nc // bt
    ct = min(chunk_tile, nc)
    u_spec = pl.BlockSpec((ct * L, lw), lambda b, i: (i, b))
    src_spec = pl.BlockSpec((L, nbg, H, P2), lambda b, i: (0, b, 0, 0))
    state_spec = pl.BlockSpec((ct, nbg, P2), lambda b, i: (i, b, 0))

    v = pl.pallas_call(
        _ssm_in_kernel,
        grid=(nb, nc // ct),
        in_specs=[u_spec, src_spec],
        out_specs=state_spec,
        out_shape=jax.ShapeDtypeStruct((nc, G, P2), F32),
        scratch_shapes=[pltpu.VMEM((L * lw, nbg * P2), BF16)],
        compiler_params=_params(("parallel", "arbitrary")),
        name="ssm_chunk_state",
    )(u, w_src)

    cb = min(scan_block, n_c)
    s_prev = pl.pallas_call(
        _ssm_scan_kernel,
        grid=(n_c // cb,),
        in_specs=[pl.BlockSpec((bt, cb, G, P2), lambda c: (0, c, 0, 0)),
                  pl.BlockSpec((G, P2), lambda c: (0, 0)),
                  pl.BlockSpec((G, P2), lambda c: (0, 0))],
        out_specs=pl.BlockSpec((bt, cb, G, P2), lambda c: (0, c, 0, 0)),
        out_shape=jax.ShapeDtypeStruct((bt, n_c, G, P2), F32),
        scratch_shapes=[pltpu.VMEM((bt, G, P2), F32)],
        compiler_params=_params(("arbitrary",)),
        name="ssm_scan",
    )(v.reshape(bt, n_c, G, P2), a1, a2)

    return pl.pallas_call(
        _ssm_out_kernel,
        grid=(nb, nc // ct),
        in_specs=[u_spec,
                  pl.BlockSpec((L, lw, lw), lambda b, i: (0, b, 0)),
                  state_spec,
                  src_spec,
                  pl.BlockSpec((1, 1, L * lw), lambda b, i: (b, 0, 0))],
        out_specs=pl.BlockSpec((ct * L, lw), lambda b, i: (i, b)),
        out_shape=jax.ShapeDtypeStruct((n, width), F32),
        scratch_shapes=[pltpu.VMEM((L * lw, L * lw), BF16), pltpu.VMEM((L * lw, nbg * P2), BF16)],
        compiler_params=_params(("parallel", "arbitrary")),
        name="ssm_output",
    )(u, t_src, s_prev.reshape(nc, G, P2), e_src, d_tile)


def _postmix_kernel(a_ref, s_ref, x_ref, wg_ref, bg_ref, gs_ref, woa_ref, wos_ref, gf_ref,
                    wrh_ref, wrl_ref, br_ref, x1_ref, h_ref, lg_ref):
    sf = s_ref[...]
    gate = _sigmoid(jnp.dot(sf.astype(BF16), wg_ref[...], preferred_element_type=F32) + bg_ref[...])
    sn = _rms(sf * gate, gs_ref[...], 1e-6).astype(BF16)
    x1 = (x_ref[...]
          + jnp.dot(a_ref[...], woa_ref[...], preferred_element_type=F32)
          + jnp.dot(sn, wos_ref[...], preferred_element_type=F32))
    x1_ref[...] = x1
    h = _rms(x1, gf_ref[...], 1e-6)
    h_hi = h.astype(BF16)
    h_lo = (h - h_hi.astype(F32)).astype(BF16)
    _store_row_slabs(h_ref, 0, h_hi)
    lg_ref[...] = (jnp.dot(h_hi, wrh_ref[...], preferred_element_type=F32)
                   + jnp.dot(h_lo, wrh_ref[...], preferred_element_type=F32)
                   + jnp.dot(h_hi, wrl_ref[...], preferred_element_type=F32)
                   + br_ref[...])


def _postmix(a, s, x2d, w_glu, b_glu, g_s, w_o, g_ffn, w_r, b_r, tm=512):
    n, d = x2d.shape
    wa = a.shape[1]
    ws = s.shape[1]
    wr_hi = w_r.astype(BF16)
    wr_lo = (w_r - wr_hi.astype(F32)).astype(BF16)
    row = lambda i: (i, 0)
    fixed = lambda i: (0, 0)
    return pl.pallas_call(
        _postmix_kernel,
        grid=(n // tm,),
        in_specs=[
            pl.BlockSpec((tm, wa), row),
            pl.BlockSpec((tm, ws), row),
            pl.BlockSpec((tm, d), row),
            pl.BlockSpec((ws, ws), fixed, pipeline_mode=pl.Buffered(1)),
            pl.BlockSpec((1, ws), fixed),
            pl.BlockSpec((1, ws), fixed),
            pl.BlockSpec((wa, d), fixed, pipeline_mode=pl.Buffered(1)),
            pl.BlockSpec((ws, d), fixed, pipeline_mode=pl.Buffered(1)),
            pl.BlockSpec((1, d), fixed),
            pl.BlockSpec((d, ROUTER_LANES), fixed, pipeline_mode=pl.Buffered(1)),
            pl.BlockSpec((d, ROUTER_LANES), fixed, pipeline_mode=pl.Buffered(1)),
            pl.BlockSpec((1, ROUTER_LANES), fixed),
        ],
        out_specs=[pl.BlockSpec((tm, d), row), pl.BlockSpec((tm * (d // SLAB_COLS), LANES), row),
                   pl.BlockSpec((tm, ROUTER_LANES), row)],
        out_shape=[jax.ShapeDtypeStruct((n, d), F32), jax.ShapeDtypeStruct((n * (d // SLAB_COLS), LANES), jnp.uint32),
                   jax.ShapeDtypeStruct((n, ROUTER_LANES), F32)],
        compiler_params=_params(("parallel",)),
        name="postmix",
    )(a, s, x2d, w_glu.astype(BF16), b_glu.reshape(1, ws).astype(F32), g_s.reshape(1, ws).astype(F32),
      w_o[:wa].astype(BF16), w_o[wa:].astype(BF16), g_ffn.reshape(1, d).astype(F32),
      wr_hi, wr_lo, b_r.reshape(1, ROUTER_LANES).astype(F32))


def _route(logits, rows, slab_rows):
    n_tok = logits.shape[0]
    lg = logits[:, :N_EXPERT_GROUPS]
    le = logits[:, N_EXPERT_GROUPS:N_EXPERT_GROUPS + N_EXPERTS].reshape(
        n_tok, N_EXPERT_GROUPS, EXPERTS_PER_GROUP)
    pg = jax.nn.softmax(lg, axis=-1)
    gsel = jnp.argmax(lg, axis=-1).astype(jnp.int32)
    gate_g = jnp.max(pg, axis=-1, keepdims=True)
    sel = gsel[:, None] == jnp.arange(N_EXPERT_GROUPS, dtype=jnp.int32)[None, :]
    le_sel = jnp.sum(jnp.where(sel[:, :, None], le, 0.0), axis=1)
    pe = jax.nn.softmax(le_sel, axis=-1)
    top_p, top_i = lax.top_k(pe, TOP_K)
    w = gate_g * top_p / jnp.sum(top_p, axis=-1, keepdims=True)
    eid = gsel[:, None] * EXPERTS_PER_GROUP + top_i.astype(jnp.int32)

    n_assign = n_tok * TOP_K
    flat_e = eid.reshape(-1)
    counts = jnp.sum((flat_e[None, :] == jnp.arange(N_EXPERTS, dtype=jnp.int32)[:, None])
                     .astype(jnp.int32), axis=1)
    n_blk_e = (counts + rows - 1) // rows
    blk_end = jnp.cumsum(n_blk_e)
    blk_first = blk_end - n_blk_e
    start = jnp.cumsum(counts) - counts
    order = jnp.argsort(flat_e, stable=True).astype(jnp.int32)
    n_blk = (n_assign + rows - 1) // rows + N_EXPERTS
    blk = jnp.arange(n_blk, dtype=jnp.int32)
    blk_e = jnp.minimum(jnp.sum((blk[:, None] >= blk_end[None, :]).astype(jnp.int32), axis=1),
                        N_EXPERTS - 1)
    used = blk < blk_end[-1]
    experts = jnp.arange(N_EXPERTS, dtype=jnp.int32)
    is_e = blk_e[:, None] == experts[None, :]

    def per_block(table):
        return jnp.sum(jnp.where(is_e, table[None, :], 0), axis=1)

    in_e = (blk - per_block(blk_first)) * rows
    blk_cnt = jnp.where(used, jnp.clip(per_block(counts) - in_e, 0, rows), 0).astype(jnp.int32)
    blk_src = jnp.where(used, per_block(start) + in_e, 0).astype(jnp.int32)
    last_e = jnp.max(jnp.where(counts > 0, experts, 0))
    blk_e = jnp.where(used, blk_e, last_e)
    row_tok = order // TOP_K
    row_dst = (order % TOP_K) * n_tok + row_tok
    pad = jnp.zeros((rows,), jnp.int32)
    return (w, jnp.concatenate([row_tok * slab_rows, pad]), jnp.concatenate([row_dst * slab_rows, pad]),
            blk_e, blk_cnt, blk_src)


def _expert_kernel(blk_e_ref, blk_cnt_ref, blk_src_ref, tok_ref, dst_ref,
                   h_hbm, w1_ref, w3_ref, w2_ref, y_hbm, xbuf, ybuf, gsem, ssem):
    del blk_e_ref
    b = pl.program_id(0)
    nb = pl.num_programs(0)
    slot = b % N_SLOTS
    slot_m1 = (b + N_SLOTS - 1) % N_SLOTS
    slot_p1 = (b + 1) % N_SLOTS
    spr = w1_ref.shape[1] // SLAB_COLS
    rows = xbuf.shape[1] // spr
    dump = y_hbm.shape[0] - N_SLOTS * rows * spr
    cnt = blk_cnt_ref[b]
    prev = jnp.maximum(b - 1, 0)
    prev_cnt = jnp.where(b >= 1, blk_cnt_ref[prev], 0)

    def gather_copy(src, r, sl):
        return pltpu.make_async_copy(h_hbm.at[pl.ds(pl.multiple_of(tok_ref[src + r], spr), spr)],
                                     xbuf.at[sl, pl.ds(r * spr, spr)], gsem.at[sl])

    def scatter_copy(src, n_valid, r, sl):
        dst = jnp.where(r < n_valid, dst_ref[src + r], dump + (sl * rows + r) * spr)
        return pltpu.make_async_copy(ybuf.at[sl, pl.ds(r * spr, spr)],
                                     y_hbm.at[pl.ds(pl.multiple_of(dst, spr), spr)], ssem.at[sl])

    def start_gathers(blk, sl, lo=0, hi=None):
        src = blk_src_ref[blk]
        for r in range(lo, rows if hi is None else hi):
            gather_copy(src, r, sl).start()

    def wait_gathers(sl):
        for r in range(rows):
            gather_copy(0, r, sl).wait()

    def start_scatters(blk, n_valid, sl, lo=0, hi=None):
        src = blk_src_ref[blk]
        for r in range(lo, rows if hi is None else hi):
            scatter_copy(src, n_valid, r, sl).start()

    def wait_scatters(sl):
        for r in range(rows):
            scatter_copy(0, 0, r, sl).wait()

    @pl.when(b == 0)
    def _():
        ybuf[...] = jnp.zeros(ybuf.shape, ybuf.dtype)
        start_gathers(0, 0)
        start_gathers(1, 1)
        for sl in range(N_SLOTS - 1):
            start_scatters(0, 0, sl)
        for sl in range(N_SLOTS - 1):
            wait_scatters(sl)

    @pl.when(cnt > 0)
    def _():
        @pl.when(b >= 2)
        def _():
            wait_scatters(slot)

        wait_gathers(slot)
        x = _load_row_slabs(xbuf.at[slot], 0, rows, spr).astype(BF16)
        f = w1_ref.shape[2]
        n_f = f // MXU_WIDTH
        n_part = 3 * n_f
        bounds = [rows * i // n_part for i in range(n_part + 1)]
        part = iter(range(n_part))

        def issue_part():
            i = next(part)
            start_gathers(jnp.minimum(b + 2, nb - 1), slot_m1, bounds[i], bounds[i + 1])
            start_scatters(prev, prev_cnt, slot_m1, bounds[i], bounds[i + 1])

        acts = []
        for c in range(n_f):
            cols = slice(c * MXU_WIDTH, (c + 1) * MXU_WIDTH)
            issue_part()
            h1 = jnp.dot(x, w1_ref[0, :, cols].astype(BF16), preferred_element_type=F32)
            issue_part()
            h3 = jnp.dot(x, w3_ref[0, :, cols].astype(BF16), preferred_element_type=F32)
            acts.append((h1 * _sigmoid(h1) * h3).astype(BF16))
        y = None
        for c in range(n_f):
            issue_part()
            yc = jnp.dot(acts[c], w2_ref[0, c * MXU_WIDTH:(c + 1) * MXU_WIDTH, :].astype(BF16),
                         preferred_element_type=F32)
            y = yc if y is None else y + yc
        _store_row_slabs(ybuf.at[slot], 0, y)

    @pl.when((cnt == 0) & (prev_cnt > 0))
    def _():
        wait_gathers(slot)
        wait_gathers(slot_p1)

        @pl.when(b >= 2)
        def _():
            wait_scatters(slot)

        wait_scatters(slot_p1)
        start_scatters(prev, prev_cnt, slot_m1)
        wait_scatters(slot_m1)


def _experts(h, w1, w3, w2, row_tok, row_dst, blk_e, blk_cnt, blk_src, rows):
    d, f = w1.shape[1], w1.shape[2]
    spr = d // SLAB_COLS
    n_blk = blk_e.shape[0]
    wmap = lambda b, be, bc, bs, rt, rd: (be[b], 0, 0)
    grid_spec = pltpu.PrefetchScalarGridSpec(
        num_scalar_prefetch=5,
        grid=(n_blk,),
        in_specs=[
            pl.BlockSpec(memory_space=pl.ANY),
            pl.BlockSpec((1, d, f), wmap),
            pl.BlockSpec((1, d, f), wmap),
            pl.BlockSpec((1, f, d), wmap),
        ],
        out_specs=pl.BlockSpec(memory_space=pl.ANY),
        scratch_shapes=[
            pltpu.VMEM((N_SLOTS, rows * spr, LANES), jnp.uint32),
            pltpu.VMEM((N_SLOTS, rows * spr, LANES), jnp.uint32),
            pltpu.SemaphoreType.DMA((N_SLOTS,)),
            pltpu.SemaphoreType.DMA((N_SLOTS,)),
        ],
    )
    return pl.pallas_call(
        _expert_kernel,
        grid_spec=grid_spec,
        out_shape=jax.ShapeDtypeStruct((h.shape[0] * TOP_K + N_SLOTS * rows * spr, LANES), jnp.uint32),
        compiler_params=_params(("arbitrary",), disable_bounds_checks=True),
        name="experts",
    )(blk_e, blk_cnt, blk_src, row_tok, row_dst, h, w1, w3, w2)


def _ple_kernel(x1_ref, y0_ref, y1_ref, w_ref, p_ref, gp_ref, wg_ref, wp_ref, gf_ref, o_ref, *, final):
    w = w_ref[...]
    tm, d = x1_ref.shape
    spr = d // SLAB_COLS
    y0 = _load_row_slabs(y0_ref, 0, tm, spr)
    y1 = _load_row_slabs(y1_ref, 0, tm, spr)
    x2 = x1_ref[...] + w[:, 0:1] * y0 + w[:, 1:2] * y1
    hn = _rms(x2, gp_ref[...], 1e-6).astype(BF16)
    gate = _sigmoid(jnp.dot(hn, wg_ref[...], preferred_element_type=F32))
    pp = jnp.dot(p_ref[...].astype(BF16), wp_ref[...], preferred_element_type=F32)
    x3 = x2 + gate * pp
    o_ref[...] = _rms(x3, gf_ref[...], 1e-6) if final else x3


def _ple(x1, y2, w, p2d, g_ple, w_gate, w_proj, g_final, final, tm=512):
    n, d = x1.shape
    pd = p2d.shape[1]
    row = lambda i: (i, 0)
    fixed = lambda i: (0, 0)
    return pl.pallas_call(
        functools.partial(_ple_kernel, final=final),
        grid=(n // tm,),
        in_specs=[
            pl.BlockSpec((tm, d), row),
            pl.BlockSpec((tm * (d // SLAB_COLS), LANES), row),
            pl.BlockSpec((tm * (d // SLAB_COLS), LANES), lambda i: (n // tm + i, 0)),
            pl.BlockSpec((tm, TOP_K), row),
            pl.BlockSpec((tm, pd), row),
            pl.BlockSpec((1, d), fixed),
            pl.BlockSpec((d, d), fixed, pipeline_mode=pl.Buffered(1)),
            pl.BlockSpec((pd, d), fixed, pipeline_mode=pl.Buffered(1)),
            pl.BlockSpec((1, d), fixed),
        ],
        out_specs=pl.BlockSpec((tm, d), row),
        out_shape=jax.ShapeDtypeStruct((n, d), F32),
        compiler_params=_params(("parallel",)),
        name="ple_final",
    )(x1, y2, y2, w, p2d, g_ple.reshape(1, d).astype(F32), w_gate.astype(BF16), w_proj.astype(BF16),
      g_final.reshape(1, d).astype(F32))


def kernel(x, p, rel_bias, g_mix, w_in, lam_q1, lam_k1, lam_q2, lam_k2, subln_g, ssm_lam_re, ssm_lam_im, ssm_log_dt, ssm_b_re, ssm_b_im, ssm_c_re, ssm_c_im, ssm_d, w_glu, b_glu, ssm_norm_g, w_o, g_ffn, w_router_g, b_router_g, w_router_e, b_router_e, w1, w3, w2, g_ple, w_ple_gate, w_ple_proj, g_final):
    bt, s_len, d = x.shape
    n = bt * s_len
    depth = g_mix.shape[0]
    attn_w = N_HEADS * DV
    xc = x.reshape(n, d).astype(F32)
    for i in range(depth):
        lam_init = 0.8 - 0.6 * math.exp(-0.3 * i)
        col_scale = jnp.concatenate([jnp.full((attn_w,), LOG2E * DK ** -0.5, F32),
                                     jnp.ones((w_in.shape[2] - attn_w,), F32)])
        w_in_b = (w_in[i].astype(F32) * col_scale[None, :]).astype(BF16)
        z, u = _inproj(xc, g_mix[i].astype(F32), w_in_b)
        z3 = z.reshape(bt, s_len, z.shape[1])

        lam = (jnp.exp(jnp.sum(lam_q1[i].astype(F32) * lam_k1[i].astype(F32)))
               - jnp.exp(jnp.sum(lam_q2[i].astype(F32) * lam_k2[i].astype(F32))) + lam_init)
        a = _attention(z3, rel_bias, lam, subln_g[i], lam_init)

        ops = _ssm_operators(ssm_lam_re[i], ssm_lam_im[i], ssm_log_dt[i], ssm_b_re[i], ssm_b_im[i],
                             ssm_c_re[i], ssm_c_im[i], ssm_d[i])
        s = _ssm(u, ops, bt)

        w_r = jnp.zeros((d, ROUTER_LANES), F32)
        w_r = w_r.at[:, :N_EXPERT_GROUPS].set(w_router_g[i].astype(F32))
        w_r = w_r.at[:, N_EXPERT_GROUPS:N_EXPERT_GROUPS + N_EXPERTS].set(w_router_e[i].astype(F32))
        b_r = jnp.zeros((ROUTER_LANES,), F32)
        b_r = b_r.at[:N_EXPERT_GROUPS].set(b_router_g[i].astype(F32))
        b_r = b_r.at[N_EXPERT_GROUPS:N_EXPERT_GROUPS + N_EXPERTS].set(b_router_e[i].astype(F32))
        x1, h, logits = _postmix(a.reshape(n, attn_w), s, xc, w_glu[i], b_glu[i],
                                 ssm_norm_g[i], w_o[i], g_ffn[i], w_r, b_r)

        gate_w, row_tok, row_dst, blk_e, blk_cnt, blk_src = _route(logits, EXPERT_ROWS, d // SLAB_COLS)
        y2 = _experts(h, w1[i], w3[i], w2[i],
                      row_tok, row_dst, blk_e, blk_cnt, blk_src, EXPERT_ROWS)
        xc = _ple(x1, y2, gate_w.astype(F32), p[i].reshape(n, -1).astype(F32),
                  g_ple[i], w_ple_gate[i], w_ple_proj[i], g_final, final=(i == depth - 1))
    return xc.reshape(bt, s_len, d)
```
